```python
import math
import jax
import jax.numpy as jnp
from jax import lax
import numpy as np

D_MODEL = 1024
BATCH = 16
SEQ = 4096
DEPTH = 1

CHUNK = 64
N_META = 16
D_MIX = D_MODEL
CONV_WIDTH = 4
LRU_WIDTH = D_MIX // 2
LRU_BLOCKS = 8
LRU_BLOCK = LRU_WIDTH // LRU_BLOCKS
LRU_C = 8.0
DN_HEADS = 4
DN_HEAD_DIM = (D_MIX - LRU_WIDTH) // DN_HEADS
DN_WIDTH = DN_HEADS * DN_HEAD_DIM
N_GROUPS = 4
EXPERTS_PER_GROUP = 8
N_EXPERTS = N_GROUPS * EXPERTS_PER_GROUP
TOP_K = 2
D_EXPERT = D_MODEL // 4
MOE_BLOCK = 256
EPS = 1e-6
PROJ_WIDTH = 2 * LRU_WIDTH + 4 * DN_WIDTH + 2 * DN_HEADS

kernel_name = 'hymba_rglru_gdn_hier_moe_block'


def rmsnorm(x, g):
    xf = x.astype(jnp.float32)
    y = xf * lax.rsqrt(jnp.mean(xf * xf, axis=-1, keepdims=True) + EPS)
    return (y * g.astype(jnp.float32)).astype(x.dtype)


def l2norm(t):
    return t * lax.rsqrt(jnp.sum(t * t, axis=-1, keepdims=True) + EPS)


def causal_depthwise_conv(x, w):
    c = x.shape[-1]
    return lax.conv_general_dilated(
        x, w[:, None, :].astype(x.dtype), window_strides=(1,),
        padding=[(CONV_WIDTH - 1, 0)], dimension_numbers=('NWC', 'WIO', 'NWC'),
        feature_group_count=c)


def rglru_group(xb, gb, conv_w, conv_b, w_r, b_r, w_i, b_i, lam):
    bsz, t, _ = xb.shape
    xc = (causal_depthwise_conv(xb, conv_w) + conv_b.astype(xb.dtype)).astype(jnp.float32)
    xblk = xc.reshape(bsz, t, LRU_BLOCKS, LRU_BLOCK)
    r = jax.nn.sigmoid(jnp.einsum('btnc,ncd->btnd', xblk, w_r.astype(jnp.float32)).reshape(bsz, t, LRU_WIDTH) + b_r.astype(jnp.float32))
    i = jax.nn.sigmoid(jnp.einsum('btnc,ncd->btnd', xblk, w_i.astype(jnp.float32)).reshape(bsz, t, LRU_WIDTH) + b_i.astype(jnp.float32))
    log_a = -LRU_C * r * jax.nn.softplus(-lam.astype(jnp.float32))
    a = jnp.exp(log_a)
    b = jnp.sqrt(-jnp.expm1(2.0 * log_a)) * (i * xc)

    def combine(c1, c2):
        a1, b1 = c1
        a2, b2 = c2
        return a1 * a2, a2 * b1 + b2

    _, h = lax.associative_scan(combine, (a, b), axis=1)
    return (h * jax.nn.gelu(gb.astype(jnp.float32))).astype(xb.dtype)


def to_chunks(t, n_chunks):
    b, _, h = t.shape[:3]
    t = t.reshape((b, n_chunks, CHUNK, h) + t.shape[3:])
    return jnp.moveaxis(t, 3, 1)


def gated_deltanet_group(q, k, v, z, beta_logit, a_logit, conv_w, a_log, dt_bias, out_norm):
    bsz, t, _ = q.shape
    f32 = jnp.float32
    qkv = jax.nn.silu(causal_depthwise_conv(jnp.concatenate([q, k, v], axis=-1), conv_w)).astype(f32)
    q, k, v = jnp.split(qkv, 3, axis=-1)
    heads = lambda u: u.reshape(bsz, t, DN_HEADS, DN_HEAD_DIM)
    q = l2norm(heads(q)) * (DN_HEAD_DIM ** -0.5)
    k = l2norm(heads(k))
    v = heads(v)
    beta = jax.nn.sigmoid(beta_logit.astype(f32))
    g = -jnp.exp(a_log.astype(f32)) * jax.nn.softplus(a_logit.astype(f32) + dt_bias.astype(f32))

    pad = (-t) % CHUNK
    padf = lambda u: jnp.pad(u, [(0, 0), (pad, 0)] + [(0, 0)] * (u.ndim - 2))
    n_chunks = (t + pad) // CHUNK
    q, k, v = (to_chunks(padf(u), n_chunks) for u in (q, k, v))
    beta, g = (to_chunks(padf(u), n_chunks) for u in (beta, g))

    G = jnp.cumsum(g, axis=-1)
    causal = jnp.tril(jnp.ones((CHUNK, CHUNK), dtype=bool))
    strict = jnp.tril(jnp.ones((CHUNK, CHUNK), dtype=bool), k=-1)
    decay = jnp.exp(jnp.where(causal, G[..., :, None] - G[..., None, :], -jnp.inf))
    kk = jnp.einsum('bhncd,bhnsd->bhncs', k, k)
    A = jnp.where(strict, kk * decay * beta[..., :, None], 0.0)
    rhs = jnp.concatenate([v * beta[..., None], k * (beta * jnp.exp(G))[..., None]], axis=-1)
    sol = lax.linalg.triangular_solve(A, rhs, left_side=True, lower=True, unit_diagonal=True)
    u_c, w_c = sol[..., :DN_HEAD_DIM], sol[..., DN_HEAD_DIM:]
    qk = jnp.einsum('bhncd,bhnsd->bhncs', q, k) * decay
    q_dec = q * jnp.exp(G)[..., None]
    G_last = G[..., -1]
    k_dec = k * jnp.exp(G_last[..., None] - G)[..., None]

    def step(S, inp):
        qk_i, qd_i, kd_i, u_i, w_i, gl_i = inp
        v_new = u_i - jnp.einsum('bhcd,bhde->bhce', w_i, S)
        o = jnp.einsum('bhcd,bhde->bhce', qd_i, S) + jnp.einsum('bhcs,bhse->bhce', qk_i, v_new)
        S = S * jnp.exp(gl_i)[..., None, None] + jnp.einsum('bhcd,bhce->bhde', kd_i, v_new)
        return S, o

    xs = tuple(jnp.moveaxis(u, 2, 0) for u in (qk, q_dec, k_dec, u_c, w_c, G_last))
    S0 = jnp.zeros((bsz, DN_HEADS, DN_HEAD_DIM, DN_HEAD_DIM), f32)
    _, o = lax.scan(step, S0, xs)
    o = jnp.transpose(o, (1, 0, 3, 2, 4)).reshape(bsz, n_chunks * CHUNK, DN_HEADS, DN_HEAD_DIM)[:, pad:]
    o = rmsnorm(o, out_norm) * jax.nn.silu(heads(z.astype(f32)))
    return o.reshape(bsz, t, DN_WIDTH).astype(z.dtype)


def hybrid_mixer(u, w_in, lru_conv_w, lru_conv_b, lru_w_r, lru_b_r, lru_w_i, lru_b_i, lru_lambda,
                 lru_out_norm, dn_conv_w, dn_a_log, dn_dt_bias, dn_out_norm, w_out):
    proj = u @ w_in.astype(u.dtype)
    sizes = [LRU_WIDTH, LRU_WIDTH, DN_WIDTH, DN_WIDTH, DN_WIDTH, DN_WIDTH, DN_HEADS]
    xb, gb, q, k, v, z, b_logit, a_logit = jnp.split(proj, [int(s) for s in np.cumsum(sizes)], axis=-1)
    y_lru = rmsnorm(rglru_group(xb, gb, lru_conv_w, lru_conv_b, lru_w_r, lru_b_r, lru_w_i, lru_b_i, lru_lambda), lru_out_norm)
    y_dn = gated_deltanet_group(q, k, v, z, b_logit, a_logit, dn_conv_w, dn_a_log, dn_dt_bias, dn_out_norm)
    return jnp.concatenate([y_lru, y_dn], axis=-1) @ w_out.astype(u.dtype)


def hierarchical_moe(u, group_w, group_b, expert_w, expert_b, w_gate, w_up, w_down):
    bsz, t, d = u.shape
    n = bsz * t
    xf = u.reshape(n, d)
    p_group = jax.nn.softmax((xf @ group_w.astype(xf.dtype)).astype(jnp.float32) + group_b.astype(jnp.float32), axis=-1)
    p_sel, g_sel = lax.top_k(p_group, 1)
    e_logits = ((xf @ expert_w.astype(xf.dtype)).astype(jnp.float32) + expert_b.astype(jnp.float32)).reshape(n, N_GROUPS, EXPERTS_PER_GROUP)
    sel_logits = e_logits[jnp.arange(n), g_sel[:, 0]]
    top_logits, top_idx = lax.top_k(sel_logits, TOP_K)
    gates = (p_sel * jax.nn.softmax(top_logits, axis=-1)).reshape(-1)
    expert_ids = (g_sel * EXPERTS_PER_GROUP + top_idx).reshape(-1).astype(jnp.int32)
    token_ids = jnp.repeat(jnp.arange(n, dtype=jnp.int32), TOP_K)

    n_assign = n * TOP_K
    one_hot = (expert_ids[:, None] == jnp.arange(N_EXPERTS, dtype=jnp.int32)[None, :]).astype(jnp.int32)
    counts = jnp.sum(one_hot, axis=0)
    rank = jnp.cumsum(one_hot, axis=0)[jnp.arange(n_assign), expert_ids] - 1
    padded = (counts + MOE_BLOCK - 1) // MOE_BLOCK * MOE_BLOCK
    ends = jnp.cumsum(padded)
    dest = (ends - padded)[expert_ids] + rank
    n_blocks = -(-(n_assign + N_EXPERTS * (MOE_BLOCK - 1)) // MOE_BLOCK)
    cap = n_blocks * MOE_BLOCK
    buf_tok = jnp.full((cap,), n, jnp.int32).at[dest].set(token_ids)
    buf_gate = jnp.zeros((cap,), jnp.float32).at[dest].set(gates)
    block_start = jnp.arange(n_blocks, dtype=jnp.int32) * MOE_BLOCK
    block_expert = jnp.minimum(jnp.sum(block_start[:, None] >= ends[None, :], axis=1), N_EXPERTS - 1)
    x_pad = jnp.concatenate([xf, jnp.zeros((1, d), xf.dtype)], axis=0)
    x_buf = x_pad[buf_tok].reshape(n_blocks, MOE_BLOCK, d)

    def expert_block(args):
        xb, e = args
        h = jax.nn.silu(xb @ w_gate[e].astype(xb.dtype)) * (xb @ w_up[e].astype(xb.dtype))
        return h @ w_down[e].astype(xb.dtype)

    y = lax.map(expert_block, (x_buf, block_expert)).reshape(cap, d)
    y = y * buf_gate[:, None].astype(y.dtype)
    out = jnp.zeros((n + 1, d), y.dtype).at[buf_tok].add(y)[:n]
    return out.reshape(bsz, t, d)


def setup_inputs(seed: int = 0) -> dict:
    key = jax.random.key(seed)
    ks = iter(jax.random.split(key, 40))
    f32 = jnp.float32
    nrm = lambda shape, scale: scale * jax.random.normal(next(ks), shape, f32)
    L = DEPTH
    x = nrm((BATCH, SEQ, D_MODEL), 1.0)
    meta_tokens = nrm((N_META, D_MODEL), 1.0)
    mix_norm = 1.0 + nrm((L, D_MODEL), 0.02)
    w_in = nrm((L, D_MODEL, PROJ_WIDTH), D_MODEL ** -0.5)
    lru_conv_w = nrm((L, CONV_WIDTH, LRU_WIDTH), CONV_WIDTH ** -0.5)
    lru_conv_b = nrm((L, LRU_WIDTH), 0.02)
    lru_w_r = nrm((L, LRU_BLOCKS, LRU_BLOCK, LRU_BLOCK), LRU_BLOCK ** -0.5)
    lru_b_r = nrm((L, LRU_WIDTH), 0.02)
    lru_w_i = nrm((L, LRU_BLOCKS, LRU_BLOCK, LRU_BLOCK), LRU_BLOCK ** -0.5)
    lru_b_i = nrm((L, LRU_WIDTH), 0.02)
    a_pow = jax.random.uniform(next(ks), (L, LRU_WIDTH), f32, 0.9, 0.999)
    a_base = a_pow ** (1.0 / LRU_C)
    lru_lambda = jnp.log(a_base) - jnp.log1p(-a_base)
    lru_out_norm = 1.0 + nrm((L, LRU_WIDTH), 0.02)
    dn_conv_w = nrm((L, CONV_WIDTH, 3 * DN_WIDTH), CONV_WIDTH ** -0.5)
    dn_a_log = jnp.log(jax.random.uniform(next(ks), (L, DN_HEADS), f32, 1.0, 16.0))
    dt = jnp.exp(jax.random.uniform(next(ks), (L, DN_HEADS), f32, math.log(1e-3), math.log(1e-1)))
    dn_dt_bias = dt + jnp.log(-jnp.expm1(-dt))
    dn_out_norm = 1.0 + nrm((L, DN_HEAD_DIM), 0.02)
    w_out = nrm((L, D_MIX, D_MODEL), D_MIX ** -0.5)
    ffn_norm = 1.0 + nrm((L, D_MODEL), 0.02)
    router_group_w = nrm((L, D_MODEL, N_GROUPS), D_MODEL ** -0.5)
    router_group_b = nrm((L, N_GROUPS), 0.01)
    router_expert_w = nrm((L, D_MODEL, N_EXPERTS), D_MODEL ** -0.5)
    router_expert_b = nrm((L, N_EXPERTS), 0.01)
    moe_w_gate = nrm((L, N_EXPERTS, D_MODEL, D_EXPERT), D_MODEL ** -0.5)
    moe_w_up = nrm((L, N_EXPERTS, D_MODEL, D_EXPERT), D_MODEL ** -0.5)
    moe_w_down = nrm((L, N_EXPERTS, D_EXPERT, D_MODEL), D_EXPERT ** -0.5)
    final_norm = 1.0 + nrm((D_MODEL,), 0.02)
    return {'x': x, 'meta_tokens': meta_tokens, 'mix_norm': mix_norm, 'w_in': w_in,
            'lru_conv_w': lru_conv_w, 'lru_conv_b': lru_conv_b, 'lru_w_r': lru_w_r, 'lru_b_r': lru_b_r,
            'lru_w_i': lru_w_i, 'lru_b_i': lru_b_i, 'lru_lambda': lru_lambda, 'lru_out_norm': lru_out_norm,
            'dn_conv_w': dn_conv_w, 'dn_a_log': dn_a_log, 'dn_dt_bias': dn_dt_bias, 'dn_out_norm': dn_out_norm,
            'w_out': w_out, 'ffn_norm': ffn_norm, 'router_group_w': router_group_w, 'router_group_b': router_group_b,
            'router_expert_w': router_expert_w, 'router_expert_b': router_expert_b,
            'moe_w_gate': moe_w_gate, 'moe_w_up': moe_w_up, 'moe_w_down': moe_w_down, 'final_norm': final_norm}


def reference(x, meta_tokens, mix_norm, w_in, lru_conv_w, lru_conv_b, lru_w_r, lru_b_r, lru_w_i, lru_b_i,
              lru_lambda, lru_out_norm, dn_conv_w, dn_a_log, dn_dt_bias, dn_out_norm, w_out, ffn_norm,
              router_group_w, router_group_b, router_expert_w, router_expert_b, moe_w_gate, moe_w_up,
              moe_w_down, final_norm):
    bsz = x.shape[0]
    meta = jnp.broadcast_to(meta_tokens.astype(x.dtype)[None], (bsz, N_META, D_MODEL))
    h = jnp.concatenate([meta, x], axis=1)
    for l in range(DEPTH):
        h = h + hybrid_mixer(rmsnorm(h, mix_norm[l]), w_in[l], lru_conv_w[l], lru_conv_b[l], lru_w_r[l],
                             lru_b_r[l], lru_w_i[l], lru_b_i[l], lru_lambda[l], lru_out_norm[l],
                             dn_conv_w[l], dn_a_log[l], dn_dt_bias[l], dn_out_norm[l], w_out[l])
        h = h + hierarchical_moe(rmsnorm(h, ffn_norm[l]), router_group_w[l], router_group_b[l],
                                 router_expert_w[l], router_expert_b[l], moe_w_gate[l], moe_w_up[l], moe_w_down[l])
    h = rmsnorm(h, final_norm)
    return h[:, N_META:]
```

```python
import functools

import jax
import jax.numpy as jnp
from jax import lax
from jax.experimental import pallas as pl
from jax.experimental.pallas import tpu as pltpu

F32 = jnp.float32
BF16 = jnp.bfloat16

D_MODEL = 1024
N_META = 16
CHUNK = 64
CONV_WIDTH = 4
LRU_WIDTH = 512
LRU_BLOCKS = 8
LRU_C = 8.0
DN_HEADS = 4
DN_HEAD_DIM = 128
DN_WIDTH = DN_HEADS * DN_HEAD_DIM
N_GROUPS = 4
EXPERTS_PER_GROUP = 8
N_EXPERTS = N_GROUPS * EXPERTS_PER_GROUP
D_EXPERT = 256
EPS = 1e-6
PROJ_MAIN = 2 * LRU_WIDTH + 4 * DN_WIDTH
LANES = 128
SUBLANES = 8
INV_BLOCK = 16
MOE_ROWS = 256
VMEM_LIMIT = 48 * 1024 * 1024


def _cparams(*sem):
    return pltpu.CompilerParams(dimension_semantics=sem, vmem_limit_bytes=VMEM_LIMIT)


def _sigmoid(x):
    return 1.0 / (1.0 + jnp.exp(-x))


def _softplus(x):
    return jnp.maximum(x, 0.0) + jnp.log(1.0 + jnp.exp(-jnp.abs(x)))


def _gelu_tanh(x):
    return 0.5 * x * (1.0 + jnp.tanh(0.7978845608028654 * (x + 0.044715 * (x * x * x))))


def _mm(a, b):
    return jnp.dot(a.astype(BF16), b.astype(BF16), preferred_element_type=F32)


def _mm_nt(a, b):
    return lax.dot_general(a.astype(BF16), b.astype(BF16), (((1,), (1,)), ((), ())),
                           preferred_element_type=F32)


def _mm_tn(a, b):
    return lax.dot_general(a.astype(BF16), b.astype(BF16), (((0,), (0,)), ((), ())),
                           preferred_element_type=F32)


def _delay_rows(x, prev8, k):
    xr = pltpu.roll(x, k, axis=0)
    pr = pltpu.roll(prev8, k, axis=0)
    row = lax.broadcasted_iota(jnp.int32, prev8.shape, 0)
    head = jnp.where(row < k, pr, xr[:SUBLANES])
    if x.shape[0] == SUBLANES:
        return head
    return jnp.concatenate([head, xr[SUBLANES:]], axis=0)


def _causal_conv(x, prev8, w):
    acc = x * w[3:4, :]
    for d in range(1, CONV_WIDTH):
        acc = acc + _delay_rows(x, prev8, d) * w[3 - d:4 - d, :]
    return acc


def _inproj_kernel(x_ref, g_ref, w_ref, wba_ref, proj_ref, ba_ref):
    x = x_ref[...]
    ms = jnp.mean(x * x, axis=-1, keepdims=True)
    u = ((x * lax.rsqrt(ms + EPS)) * g_ref[...]).astype(BF16)
    for n in range(0, PROJ_MAIN, 512):
        proj_ref[:, n:n + 512] = jnp.dot(u, w_ref[:, n:n + 512],
                                         preferred_element_type=F32).astype(proj_ref.dtype)
    ba_ref[...] = jnp.dot(u, wba_ref[...], preferred_element_type=F32)


def _inproj(x2, norm_g, w_main, w_ba, tm):
    n = x2.shape[0]
    return pl.pallas_call(
        _inproj_kernel,
        grid=(n // tm,),
        in_specs=[pl.BlockSpec((tm, D_MODEL), lambda i: (i, 0)),
                  pl.BlockSpec((1, D_MODEL), lambda i: (0, 0)),
                  pl.BlockSpec((D_MODEL, PROJ_MAIN), lambda i: (0, 0)),
                  pl.BlockSpec((D_MODEL, LANES), lambda i: (0, 0))],
        out_specs=[pl.BlockSpec((tm, PROJ_MAIN), lambda i: (i, 0)),
                   pl.BlockSpec((tm, LANES), lambda i: (i, 0))],
        out_shape=[jax.ShapeDtypeStruct((n, PROJ_MAIN), BF16),
                   jax.ShapeDtypeStruct((n, LANES), F32)],
        compiler_params=_cparams("parallel"),
        name="inproj",
    )(x2, norm_g, w_main, w_ba)


def _lru_kernel(xb_ref, gb_ref, tail_ref, h0_ref, cw_ref, p_ref, wg_ref, y_ref, hl_ref,
                h_sc, xt_sc, *, tb):
    t = pl.program_id(1)

    @pl.when(t == 0)
    def _():
        h_sc[...] = h0_ref[...]
        xt_sc[...] = tail_ref[...]

    x = xb_ref[...].astype(F32)
    conv_b, b_r, b_i = p_ref[0:1, :], p_ref[1:2, :], p_ref[2:3, :]
    lam, out_g = p_ref[3:4, :], p_ref[4:5, :]
    xc = _causal_conv(x, xt_sc[...], cw_ref[...]) + conv_b
    xt_sc[...] = x[tb - SUBLANES:, :]

    half = LRU_WIDTH // 2
    xcb = xc.astype(BF16)
    g0 = jnp.dot(xcb[:, :half], wg_ref[0], preferred_element_type=F32)
    g1 = jnp.dot(xcb[:, half:], wg_ref[1], preferred_element_type=F32)
    r = _sigmoid(jnp.concatenate([g0[:, :half], g1[:, :half]], axis=1) + b_r)
    i = _sigmoid(jnp.concatenate([g0[:, half:], g1[:, half:]], axis=1) + b_i)
    log_a = (-LRU_C) * r * _softplus(-lam)
    a = jnp.exp(log_a)
    b = jnp.sqrt(1.0 - jnp.exp(2.0 * log_a)) * (i * xc)

    sub = lax.broadcasted_iota(jnp.int32, (tb, LRU_WIDTH), 0) & (SUBLANES - 1)
    for s in (1, 2, 4):
        keep = sub >= s
        b = jnp.where(keep, a * pltpu.roll(b, s, axis=0), 0.0) + b
        a = jnp.where(keep, a * pltpu.roll(a, s, axis=0), a)
    h = h_sc[...]
    hs = []
    for g in range(tb // SUBLANES):
        hg = a[g * SUBLANES:(g + 1) * SUBLANES] * h + b[g * SUBLANES:(g + 1) * SUBLANES]
        hs.append(hg)
        h = jnp.broadcast_to(hg[SUBLANES - 1:SUBLANES, :], (SUBLANES, LRU_WIDTH))
    h_sc[...] = h
    hseq = jnp.concatenate(hs, axis=0)

    out = hseq * _gelu_tanh(gb_ref[...].astype(F32))
    ms = jnp.mean(out * out, axis=-1, keepdims=True)
    y_ref[...] = ((out * lax.rsqrt(ms + EPS)) * out_g).astype(y_ref.dtype)

    @pl.when(t == pl.num_programs(1) - 1)
    def _():
        hl_ref[...] = h


def _lru(proj3, tail8, h0, conv_w8, params8, w_gate, tb):
    bsz, t, _ = proj3.shape
    bcast = lambda b, i: (0, 0)
    return pl.pallas_call(
        functools.partial(_lru_kernel, tb=tb),
        grid=(bsz, t // tb),
        in_specs=[pl.BlockSpec((None, tb, LRU_WIDTH), lambda b, i: (b, i, 0)),
                  pl.BlockSpec((None, tb, LRU_WIDTH), lambda b, i: (b, i, 1)),
                  pl.BlockSpec((SUBLANES, LRU_WIDTH), bcast),
                  pl.BlockSpec((SUBLANES, LRU_WIDTH), bcast),
                  pl.BlockSpec((SUBLANES, LRU_WIDTH), bcast),
                  pl.BlockSpec((SUBLANES, LRU_WIDTH), bcast),
                  pl.BlockSpec((2, LRU_WIDTH // 2, LRU_WIDTH), lambda b, i: (0, 0, 0))],
        out_specs=[pl.BlockSpec((None, tb, LRU_WIDTH), lambda b, i: (b, i, 0)),
                   pl.BlockSpec((None, SUBLANES, LRU_WIDTH), lambda b, i: (b, 0, 0))],
        out_shape=[jax.ShapeDtypeStruct((bsz, t, LRU_WIDTH), BF16),
                   jax.ShapeDtypeStruct((bsz, SUBLANES, LRU_WIDTH), F32)],
        scratch_shapes=[pltpu.VMEM((SUBLANES, LRU_WIDTH), F32),
                        pltpu.VMEM((SUBLANES, LRU_WIDTH), F32)],
        compiler_params=_cparams("parallel", "arbitrary"),
        name="lru",
    )(proj3, proj3, tail8, h0, conv_w8, params8, w_gate)


def _unit_lower_inverse(a_mat, ri, ci, c):
    eye = (ri == ci).astype(F32)
    if c == INV_BLOCK:
        d = a_mat
    else:
        same = (ri // INV_BLOCK) == (ci // INV_BLOCK)
        d = jnp.where(same, a_mat, 0.0)
    d2 = _mm(d, d)
    d4 = _mm(d2, d2)
    d8 = _mm(d4, d4)
    p = _mm(_mm(_mm(eye - d, eye + d2), eye + d4), eye + d8)
    if c == INV_BLOCK:
        return p
    m = _mm(p, a_mat - d)
    steps = c // INV_BLOCK
    q = eye - m
    mp = m
    k = 2
    while k < steps:
        mp = _mm(mp, mp)
        q = _mm(q, eye + mp)
        k *= 2
    return _mm(q, p)


def _dn_kernel(q_ref, k_ref, v_ref, z_ref, ba_ref, tail_ref, s0_ref, cw_ref, hp_ref, on_ref,
               y_ref, sl_ref, s_sc, xt_sc, c_sc, bg_sc, *, tb, c):
    t = pl.program_id(1)

    @pl.when(t == 0)
    def _():
        s_sc[...] = s0_ref[...]
        xt_sc[...] = tail_ref[...]

    qkv = jnp.concatenate([q_ref[...], k_ref[...], v_ref[...]], axis=1).astype(F32)
    act = _causal_conv(qkv, xt_sc[...], cw_ref[...])
    xt_sc[...] = qkv[tb - SUBLANES:, :]
    act = act * _sigmoid(act)
    for hd in range(2 * DN_HEADS):
        seg = act[:, hd * DN_HEAD_DIM:(hd + 1) * DN_HEAD_DIM]
        nrm = lax.rsqrt(jnp.sum(seg * seg, axis=-1, keepdims=True) + EPS)
        if hd < DN_HEADS:
            nrm = nrm * (DN_HEAD_DIM ** -0.5)
        c_sc[:, hd * DN_HEAD_DIM:(hd + 1) * DN_HEAD_DIM] = seg * nrm
    c_sc[:, 2 * DN_WIDTH:] = act[:, 2 * DN_WIDTH:]

    ba = ba_ref[...]
    lane = lax.broadcasted_iota(jnp.int32, ba.shape, 1)
    beta_all = _sigmoid(ba)
    g_all = -jnp.exp(hp_ref[0:1, :]) * _softplus(ba + hp_ref[1:2, :])
    bg_sc[...] = jnp.where(lane < DN_HEADS, beta_all, g_all)

    ri = lax.broadcasted_iota(jnp.int32, (c, c), 0)
    ci = lax.broadcasted_iota(jnp.int32, (c, c), 1)
    causal = ri >= ci
    strict = ri > ci
    out_g = on_ref[...]

    def chunk(off):
        bg = bg_sc[pl.ds(off, c), :]
        for hd in range(DN_HEADS):
            lo = hd * DN_HEAD_DIM
            q = c_sc[pl.ds(off, c), lo:lo + DN_HEAD_DIM]
            k = c_sc[pl.ds(off, c), DN_WIDTH + lo:DN_WIDTH + lo + DN_HEAD_DIM]
            v = c_sc[pl.ds(off, c), 2 * DN_WIDTH + lo:2 * DN_WIDTH + lo + DN_HEAD_DIM]
            beta = bg[:, hd:hd + 1]
            gcol = jnp.broadcast_to(bg[:, DN_HEADS + hd:DN_HEADS + hd + 1], (c, c))
            g_row = jnp.sum(jnp.where(ri == ci, gcol, 0.0), axis=0, keepdims=True)
            cum_row = jnp.sum(jnp.where(ri <= ci, gcol, 0.0), axis=0, keepdims=True)
            cum_col = jnp.sum(jnp.where(causal, jnp.broadcast_to(g_row, (c, c)), 0.0),
                              axis=1, keepdims=True)
            decay = jnp.where(causal, jnp.exp(jnp.where(causal, cum_col - cum_row, 0.0)), 0.0)
            e_g = jnp.exp(cum_col)
            g_last = cum_col[c - 1:c, :]
            kk = _mm_nt(k, k)
            qk = _mm_nt(q, k)
            a_mat = jnp.where(strict, kk * decay, 0.0) * beta
            t_inv = _unit_lower_inverse(a_mat, ri, ci, c)
            rhs = jnp.concatenate([v * beta, k * (beta * e_g)], axis=1)
            sol = _mm(t_inv, rhs)
            u_c, w_c = sol[:, :DN_HEAD_DIM], sol[:, DN_HEAD_DIM:]
            s = s_sc[hd]
            ws = _mm(jnp.concatenate([w_c, q * e_g], axis=0), s)
            v_new = u_c - ws[:c]
            o = ws[c:] + _mm(qk * decay, v_new)
            k_dec = k * jnp.exp(g_last - cum_col)
            s_sc[hd] = s * jnp.exp(g_last) + _mm_tn(k_dec, v_new)
            ms = jnp.mean(o * o, axis=-1, keepdims=True)
            z = z_ref[pl.ds(off, c), lo:lo + DN_HEAD_DIM].astype(F32)
            res = ((o * lax.rsqrt(ms + EPS)) * out_g) * (z * _sigmoid(z))
            y_ref[pl.ds(off, c), lo:lo + DN_HEAD_DIM] = res.astype(y_ref.dtype)

    n_chunks = tb // c
    if n_chunks == 1:
        chunk(0)
    else:
        def body(i, carry):
            chunk(pl.multiple_of(i * c, c))
            return carry
        lax.fori_loop(0, n_chunks, body, 0)

    @pl.when(t == pl.num_programs(1) - 1)
    def _():
        sl_ref[...] = s_sc[...]


def _deltanet(proj3, ba3, tail8, s0, conv_w8, head_p, out_norm, tb, c):
    bsz, t, _ = proj3.shape
    bcast = lambda b, i: (0, 0)
    col = lambda j: (lambda b, i: (b, i, j))
    return pl.pallas_call(
        functools.partial(_dn_kernel, tb=tb, c=c),
        grid=(bsz, t // tb),
        in_specs=[pl.BlockSpec((None, tb, DN_WIDTH), col(2)),
                  pl.BlockSpec((None, tb, DN_WIDTH), col(3)),
                  pl.BlockSpec((None, tb, DN_WIDTH), col(4)),
                  pl.BlockSpec((None, tb, DN_WIDTH), col(5)),
                  pl.BlockSpec((None, tb, LANES), lambda b, i: (b, i, 0)),
                  pl.BlockSpec((SUBLANES, 3 * DN_WIDTH), bcast),
                  pl.BlockSpec((DN_HEADS, DN_HEAD_DIM, DN_HEAD_DIM), lambda b, i: (0, 0, 0)),
                  pl.BlockSpec((SUBLANES, 3 * DN_WIDTH), bcast),
                  pl.BlockSpec((SUBLANES, LANES), bcast),
                  pl.BlockSpec((1, DN_HEAD_DIM), bcast)],
        out_specs=[pl.BlockSpec((None, tb, DN_WIDTH), lambda b, i: (b, i, 0)),
                   pl.BlockSpec((None, DN_HEADS, DN_HEAD_DIM, DN_HEAD_DIM),
                                lambda b, i: (b, 0, 0, 0))],
        out_shape=[jax.ShapeDtypeStruct((bsz, t, DN_WIDTH), BF16),
                   jax.ShapeDtypeStruct((bsz, DN_HEADS, DN_HEAD_DIM, DN_HEAD_DIM), F32)],
        scratch_shapes=[pltpu.VMEM((DN_HEADS, DN_HEAD_DIM, DN_HEAD_DIM), F32),
                        pltpu.VMEM((SUBLANES, 3 * DN_WIDTH), F32),
                        pltpu.VMEM((tb, 3 * DN_WIDTH), F32),
                        pltpu.VMEM((tb, LANES), F32)],
        compiler_params=_cparams("parallel", "arbitrary"),
        name="deltanet",
    )(proj3, proj3, proj3, proj3, ba3, tail8, s0, conv_w8, head_p, out_norm)


def _outproj_kernel(yl_ref, yd_ref, x_ref, wo_ref, fg_ref, wr_ref, br_ref,
                    h1_ref, xn_ref, id_ref, gt_ref):
    mix = jnp.dot(yl_ref[...], wo_ref[:LRU_WIDTH, :], preferred_element_type=F32)
    mix = mix + jnp.dot(yd_ref[...], wo_ref[LRU_WIDTH:, :], preferred_element_type=F32)
    h1 = x_ref[...] + mix
    h1_ref[...] = h1
    ms = jnp.mean(h1 * h1, axis=-1, keepdims=True)
    xn = (h1 * lax.rsqrt(ms + EPS)) * fg_ref[...]
    xn_ref[...] = xn.astype(xn_ref.dtype)

    logits = jnp.dot(xn.astype(BF16), wr_ref[...], preferred_element_type=F32) + br_ref[...]
    lane = lax.broadcasted_iota(jnp.int32, logits.shape, 1)
    lanef = lane.astype(F32)
    neg = -jnp.inf
    big = 1e9
    lg = jnp.where(lane < N_GROUPS, logits, neg)
    mg = jnp.max(lg, axis=-1, keepdims=True)
    p_sel = 1.0 / jnp.sum(jnp.exp(lg - mg), axis=-1, keepdims=True)
    g_sel = jnp.min(jnp.where(lg == mg, lanef, big), axis=-1, keepdims=True)
    e_grp = ((lane - N_GROUPS) >> 3).astype(F32)
    in_grp = (lane >= N_GROUPS) & (lane < N_GROUPS + N_EXPERTS) & (e_grp == g_sel)
    le = jnp.where(in_grp, logits, neg)
    m1 = jnp.max(le, axis=-1, keepdims=True)
    i1 = jnp.min(jnp.where(le == m1, lanef, big), axis=-1, keepdims=True)
    le2 = jnp.where(lanef == i1, neg, le)
    m2 = jnp.max(le2, axis=-1, keepdims=True)
    i2 = jnp.min(jnp.where(le2 == m2, lanef, big), axis=-1, keepdims=True)
    e21 = jnp.exp(m2 - m1)
    w1 = p_sel / (1.0 + e21)
    w2 = p_sel * e21 / (1.0 + e21)
    ids = jnp.where(lane == 0, i1, jnp.where(lane == 1, i2, float(N_GROUPS))) - float(N_GROUPS)
    id_ref[...] = ids.astype(jnp.int32)
    gt_ref[...] = jnp.where(lane == 0, w1, jnp.where(lane == 1, w2, 0.0))


def _outproj(y_lru, y_dn, x2, w_out, ffn_g, w_router, b_router, tm):
    n = x2.shape[0]
    row = lambda i: (i, 0)
    fix = lambda i: (0, 0)
    return pl.pallas_call(
        _outproj_kernel,
        grid=(n // tm,),
        in_specs=[pl.BlockSpec((tm, LRU_WIDTH), row),
                  pl.BlockSpec((tm, DN_WIDTH), row),
                  pl.BlockSpec((tm, D_MODEL), row),
                  pl.BlockSpec((D_MODEL, D_MODEL), fix),
                  pl.BlockSpec((1, D_MODEL), fix),
                  pl.BlockSpec((D_MODEL, LANES), fix),
                  pl.BlockSpec((1, LANES), fix)],
        out_specs=[pl.BlockSpec((tm, D_MODEL), row),
                   pl.BlockSpec((tm, D_MODEL), row),
                   pl.BlockSpec((tm, LANES), row),
                   pl.BlockSpec((tm, LANES), row)],
        out_shape=[jax.ShapeDtypeStruct((n, D_MODEL), F32),
                   jax.ShapeDtypeStruct((n, D_MODEL), F32),
                   jax.ShapeDtypeStruct((n, LANES), jnp.int32),
                   jax.ShapeDtypeStruct((n, LANES), F32)],
        compiler_params=_cparams("parallel"),
        name="outproj_router",
    )(y_lru, y_dn, x2, w_out, ffn_g, w_router, b_router)


def _expert_kernel(be_ref, x_ref, wgu_ref, wd_ref, y_ref):
    del be_ref
    gu = jnp.dot(x_ref[...].astype(BF16), wgu_ref[...], preferred_element_type=F32)
    g, u = gu[:, :D_EXPERT], gu[:, D_EXPERT:]
    hmid = (g * _sigmoid(g)) * u
    y_ref[...] = jnp.dot(hmid.astype(BF16), wd_ref[...],
                         preferred_element_type=F32).astype(y_ref.dtype)


def _experts(block_expert, x_buf, w_gu, w_down):
    cap = x_buf.shape[0]
    grid_spec = pltpu.PrefetchScalarGridSpec(
        num_scalar_prefetch=1,
        grid=(cap // MOE_ROWS,),
        in_specs=[pl.BlockSpec((MOE_ROWS, D_MODEL), lambda i, be: (i, 0)),
                  pl.BlockSpec((None, D_MODEL, 2 * D_EXPERT), lambda i, be: (be[i], 0, 0)),
                  pl.BlockSpec((None, D_EXPERT, D_MODEL), lambda i, be: (be[i], 0, 0))],
        out_specs=pl.BlockSpec((MOE_ROWS, D_MODEL), lambda i, be: (i, 0)),
    )
    return pl.pallas_call(
        _expert_kernel,
        grid_spec=grid_spec,
        out_shape=jax.ShapeDtypeStruct((cap, D_MODEL), F32),
        compiler_params=_cparams("arbitrary"),
        name="experts",
    )(block_expert, x_buf, w_gu, w_down)


def _combine_kernel(h1_ref, y0_ref, y1_ref, gt_ref, fg_ref, o_ref):
    gt = gt_ref[...]
    h = h1_ref[...] + gt[:, 0:1] * y0_ref[...] + gt[:, 1:2] * y1_ref[...]
    ms = jnp.mean(h * h, axis=-1, keepdims=True)
    o_ref[...] = (h * lax.rsqrt(ms + EPS)) * fg_ref[...]


def _combine(h1, y_tok, gates, final_g, tm):
    n = h1.shape[0]
    row = lambda i: (i, 0)
    return pl.pallas_call(
        _combine_kernel,
        grid=(n // tm,),
        in_specs=[pl.BlockSpec((tm, D_MODEL), row),
                  pl.BlockSpec((None, tm, D_MODEL), lambda i: (0, i, 0)),
                  pl.BlockSpec((None, tm, D_MODEL), lambda i: (1, i, 0)),
                  pl.BlockSpec((tm, LANES), row),
                  pl.BlockSpec((1, D_MODEL), lambda i: (0, 0))],
        out_specs=pl.BlockSpec((tm, D_MODEL), row),
        out_shape=jax.ShapeDtypeStruct((n, D_MODEL), F32),
        compiler_params=_cparams("parallel"),
        name="combine",
    )(h1, y_tok, y_tok, gates, final_g)


def _block_diag(blocks):
    n, r, c = blocks.shape
    out = jnp.zeros((n * r, n * c), blocks.dtype)
    for i in range(n):
        out = out.at[i * r:(i + 1) * r, i * c:(i + 1) * c].set(blocks[i])
    return out


def _pad_rows(a, rows):
    return jnp.pad(a, ((0, rows - a.shape[0]), (0, 0)))


def _tail8(rows3):
    return jnp.pad(rows3.astype(F32), ((SUBLANES - rows3.shape[0], 0), (0, 0)))


def _dispatch_tables(ids2):
    n = ids2.shape[0]
    n_assign = 2 * n
    eflat = ids2.reshape(-1)
    order = jnp.argsort(eflat, stable=True).astype(jnp.int32)
    sorted_e = eflat[order]
    counts = jnp.sum((eflat[:, None] == jnp.arange(N_EXPERTS, dtype=jnp.int32)[None, :])
                     .astype(jnp.int32), axis=0)
    starts = jnp.cumsum(counts) - counts
    padded = (counts + MOE_ROWS - 1) // MOE_ROWS * MOE_ROWS
    pend = jnp.cumsum(padded)
    pstart = pend - padded
    dest_sorted = pstart[sorted_e] + (jnp.arange(n_assign, dtype=jnp.int32) - starts[sorted_e])
    n_blocks = -(-(n_assign + N_EXPERTS * (MOE_ROWS - 1)) // MOE_ROWS)
    cap = n_blocks * MOE_ROWS
    buf_tok = jnp.zeros((cap,), jnp.int32).at[dest_sorted].set(order // 2)
    dest = jnp.zeros((n_assign,), jnp.int32).at[order].set(dest_sorted)
    block_start = jnp.arange(n_blocks, dtype=jnp.int32) * MOE_ROWS
    block_expert = jnp.minimum(jnp.sum(block_start[:, None] >= pend[None, :], axis=1),
                               N_EXPERTS - 1).astype(jnp.int32)
    return buf_tok, dest.reshape(n, 2), block_expert


def kernel(x, meta_tokens, mix_norm, w_in, lru_conv_w, lru_conv_b, lru_w_r, lru_b_r, lru_w_i,
           lru_b_i, lru_lambda, lru_out_norm, dn_conv_w, dn_a_log, dn_dt_bias, dn_out_norm, w_out,
           ffn_norm, router_group_w, router_group_b, router_expert_w, router_expert_b, moe_w_gate,
           moe_w_up, moe_w_down, final_norm):
    bsz, seq, d = x.shape
    n = bsz * seq
    l = 0

    w_main = w_in[l][:, :PROJ_MAIN].astype(BF16)
    w_ba = jnp.pad(w_in[l][:, PROJ_MAIN:], ((0, 0), (0, LANES - 2 * DN_HEADS))).astype(BF16)
    mix_g = mix_norm[l][None, :]
    lru_cw = _pad_rows(lru_conv_w[l], SUBLANES)
    lru_p = _pad_rows(jnp.stack([lru_conv_b[l], lru_b_r[l], lru_b_i[l], lru_lambda[l],
                                 lru_out_norm[l]]), SUBLANES)
    hb = LRU_BLOCKS // 2
    w_gate = jnp.stack([
        jnp.concatenate([_block_diag(lru_w_r[l][h * hb:(h + 1) * hb]),
                         _block_diag(lru_w_i[l][h * hb:(h + 1) * hb])], axis=1)
        for h in range(2)]).astype(BF16)
    dn_cw = _pad_rows(dn_conv_w[l], SUBLANES)
    head_p = jnp.zeros((SUBLANES, LANES), F32)
    head_p = head_p.at[0, DN_HEADS:2 * DN_HEADS].set(dn_a_log[l])
    head_p = head_p.at[1, DN_HEADS:2 * DN_HEADS].set(dn_dt_bias[l])
    dn_on = dn_out_norm[l][None, :]
    w_o = w_out[l].astype(BF16)
    ffn_g = ffn_norm[l][None, :]
    w_router = jnp.pad(jnp.concatenate([router_group_w[l], router_expert_w[l]], axis=1),
                       ((0, 0), (0, LANES - N_GROUPS - N_EXPERTS))).astype(BF16)
    b_router = jnp.pad(jnp.concatenate([router_group_b[l], router_expert_b[l]]),
                       (0, LANES - N_GROUPS - N_EXPERTS))[None, :]
    w_gu = jnp.concatenate([moe_w_gate[l], moe_w_up[l]], axis=-1).astype(BF16)
    w_dn = moe_w_down[l].astype(BF16)
    final_g = final_norm[None, :]

    proj_m, ba_m = _inproj(meta_tokens, mix_g, w_main, w_ba, N_META)
    zeros_lru = jnp.zeros((SUBLANES, LRU_WIDTH), F32)
    _, h_meta = _lru(proj_m[None], zeros_lru, zeros_lru, lru_cw, lru_p, w_gate, N_META)
    _, s_meta = _deltanet(proj_m[None], ba_m[None], jnp.zeros((SUBLANES, 3 * DN_WIDTH), F32),
                          jnp.zeros((DN_HEADS, DN_HEAD_DIM, DN_HEAD_DIM), F32),
                          dn_cw, head_p, dn_on, N_META, N_META)
    last = proj_m[N_META - (CONV_WIDTH - 1):]
    lru_tail = _tail8(last[:, :LRU_WIDTH])
    dn_tail = _tail8(last[:, 2 * LRU_WIDTH:2 * LRU_WIDTH + 3 * DN_WIDTH])

    x2 = x.reshape(n, d)
    proj, ba = _inproj(x2, mix_g, w_main, w_ba, 512)
    proj3 = proj.reshape(bsz, seq, PROJ_MAIN)
    ba3 = ba.reshape(bsz, seq, LANES)
    y_lru, _ = _lru(proj3, lru_tail, h_meta[0], lru_cw, lru_p, w_gate, 256)
    y_dn, _ = _deltanet(proj3, ba3, dn_tail, s_meta[0], dn_cw, head_p, dn_on, 256, CHUNK)
    h1, xn, ids, gates = _outproj(y_lru.reshape(n, LRU_WIDTH), y_dn.reshape(n, DN_WIDTH), x2,
                                  w_o, ffn_g, w_router, b_router, 512)

    buf_tok, dest, block_expert = _dispatch_tables(ids[:, :2])
    x_buf = xn[buf_tok]
    y_buf = _experts(block_expert, x_buf, w_gu, w_dn)
    y_tok = jnp.stack([y_buf[dest[:, 0]], y_buf[dest[:, 1]]])
    out = _combine(h1, y_tok, gates, final_g, 512)
    return out.reshape(bsz, seq, d)
```

```python
import functools

import jax
import jax.numpy as jnp
from jax import lax
from jax.experimental import pallas as pl
from jax.experimental.pallas import tpu as pltpu

F32 = jnp.float32
BF16 = jnp.bfloat16

D_MODEL = 1024
N_META = 16
CHUNK = 64
CONV_WIDTH = 4
LRU_WIDTH = 512
LRU_BLOCKS = 8
LRU_C = 8.0
DN_HEADS = 4
DN_HEAD_DIM = 128
DN_WIDTH = DN_HEADS * DN_HEAD_DIM
N_GROUPS = 4
EXPERTS_PER_GROUP = 8
N_EXPERTS = N_GROUPS * EXPERTS_PER_GROUP
D_EXPERT = 256
EPS = 1e-6
PROJ_MAIN = 2 * LRU_WIDTH + 4 * DN_WIDTH
LANES = 128
SUBLANES = 8
INV_BLOCK = 16
MOE_ROWS = 256
VMEM_LIMIT = 48 * 1024 * 1024


def _cparams(*sem):
    return pltpu.CompilerParams(dimension_semantics=sem, vmem_limit_bytes=VMEM_LIMIT)


def _sigmoid(x):
    return 1.0 / (1.0 + jnp.exp(-x))


def _softplus(x):
    return jnp.maximum(x, 0.0) + jnp.log(1.0 + jnp.exp(-jnp.abs(x)))


def _gelu_tanh(x):
    return 0.5 * x * (1.0 + jnp.tanh(0.7978845608028654 * (x + 0.044715 * (x * x * x))))


def _mm(a, b):
    return jnp.dot(a.astype(BF16), b.astype(BF16), preferred_element_type=F32)


def _mm_nt(a, b):
    return lax.dot_general(a.astype(BF16), b.astype(BF16), (((1,), (1,)), ((), ())),
                           preferred_element_type=F32)


def _mm_tn(a, b):
    return lax.dot_general(a.astype(BF16), b.astype(BF16), (((0,), (0,)), ((), ())),
                           preferred_element_type=F32)


def _delay_rows(x, prev8, k):
    xr = pltpu.roll(x, k, axis=0)
    pr = pltpu.roll(prev8, k, axis=0)
    row = lax.broadcasted_iota(jnp.int32, prev8.shape, 0)
    head = jnp.where(row < k, pr, xr[:SUBLANES])
    if x.shape[0] == SUBLANES:
        return head
    return jnp.concatenate([head, xr[SUBLANES:]], axis=0)


def _causal_conv(x, prev8, w):
    acc = x * w[3:4, :]
    for d in range(1, CONV_WIDTH):
        acc = acc + _delay_rows(x, prev8, d) * w[3 - d:4 - d, :]
    return acc


def _inproj_kernel(x_ref, g_ref, w_ref, wba_ref, proj_ref, ba_ref):
    x = x_ref[...]
    ms = jnp.mean(x * x, axis=-1, keepdims=True)
    u = ((x * lax.rsqrt(ms + EPS)) * g_ref[...]).astype(BF16)
    for n in range(0, PROJ_MAIN, 512):
        proj_ref[:, n:n + 512] = jnp.dot(u, w_ref[:, n:n + 512],
                                         preferred_element_type=F32).astype(proj_ref.dtype)
    ba_ref[...] = jnp.dot(u, wba_ref[...], preferred_element_type=F32)


def _inproj(x2, norm_g, w_main, w_ba, tm):
    n = x2.shape[0]
    return pl.pallas_call(
        _inproj_kernel,
        grid=(n // tm,),
        in_specs=[pl.BlockSpec((tm, D_MODEL), lambda i: (i, 0)),
                  pl.BlockSpec((1, D_MODEL), lambda i: (0, 0)),
                  pl.BlockSpec((D_MODEL, PROJ_MAIN), lambda i: (0, 0)),
                  pl.BlockSpec((D_MODEL, LANES), lambda i: (0, 0))],
        out_specs=[pl.BlockSpec((tm, PROJ_MAIN), lambda i: (i, 0)),
                   pl.BlockSpec((tm, LANES), lambda i: (i, 0))],
        out_shape=[jax.ShapeDtypeStruct((n, PROJ_MAIN), BF16),
                   jax.ShapeDtypeStruct((n, LANES), F32)],
        compiler_params=_cparams("parallel"),
        name="inproj",
    )(x2, norm_g, w_main, w_ba)


def _lru_kernel(xb_ref, gb_ref, tail_ref, h0_ref, cw_ref, p_ref, wg_ref, y_ref, hl_ref,
                h_sc, xt_sc, *, tb):
    t = pl.program_id(1)

    @pl.when(t == 0)
    def _():
        h_sc[...] = h0_ref[...]
        xt_sc[...] = tail_ref[...]

    x = xb_ref[...].astype(F32)
    conv_b, b_r, b_i = p_ref[0:1, :], p_ref[1:2, :], p_ref[2:3, :]
    lam, out_g = p_ref[3:4, :], p_ref[4:5, :]
    xc = _causal_conv(x, xt_sc[...], cw_ref[...]) + conv_b
    xt_sc[...] = x[tb - SUBLANES:, :]

    half = LRU_WIDTH // 2
    xcb = xc.astype(BF16)
    g0 = jnp.dot(xcb[:, :half], wg_ref[0], preferred_element_type=F32)
    g1 = jnp.dot(xcb[:, half:], wg_ref[1], preferred_element_type=F32)
    r = _sigmoid(jnp.concatenate([g0[:, :half], g1[:, :half]], axis=1) + b_r)
    i = _sigmoid(jnp.concatenate([g0[:, half:], g1[:, half:]], axis=1) + b_i)
    log_a = (-LRU_C) * r * _softplus(-lam)
    a = jnp.exp(log_a)
    b = jnp.sqrt(1.0 - jnp.exp(2.0 * log_a)) * (i * xc)

    sub = lax.broadcasted_iota(jnp.int32, (tb, LRU_WIDTH), 0) & (SUBLANES - 1)
    for s in (1, 2, 4):
        keep = sub >= s
        b = jnp.where(keep, a * pltpu.roll(b, s, axis=0), 0.0) + b
        a = jnp.where(keep, a * pltpu.roll(a, s, axis=0), a)
    h = h_sc[...]
    hs = []
    for g in range(tb // SUBLANES):
        hg = a[g * SUBLANES:(g + 1) * SUBLANES] * h + b[g * SUBLANES:(g + 1) * SUBLANES]
        hs.append(hg)
        h = jnp.broadcast_to(hg[SUBLANES - 1:SUBLANES, :], (SUBLANES, LRU_WIDTH))
    h_sc[...] = h
    hseq = jnp.concatenate(hs, axis=0)

    out = hseq * _gelu_tanh(gb_ref[...].astype(F32))
    ms = jnp.mean(out * out, axis=-1, keepdims=True)
    y_ref[...] = ((out * lax.rsqrt(ms + EPS)) * out_g).astype(y_ref.dtype)

    @pl.when(t == pl.num_programs(1) - 1)
    def _():
        hl_ref[...] = h


def _lru(proj3, tail8, h0, conv_w8, params8, w_gate, tb):
    bsz, t, _ = proj3.shape
    bcast = lambda b, i: (0, 0)
    return pl.pallas_call(
        functools.partial(_lru_kernel, tb=tb),
        grid=(bsz, t // tb),
        in_specs=[pl.BlockSpec((None, tb, LRU_WIDTH), lambda b, i: (b, i, 0)),
                  pl.BlockSpec((None, tb, LRU_WIDTH), lambda b, i: (b, i, 1)),
                  pl.BlockSpec((SUBLANES, LRU_WIDTH), bcast),
                  pl.BlockSpec((SUBLANES, LRU_WIDTH), bcast),
                  pl.BlockSpec((SUBLANES, LRU_WIDTH), bcast),
                  pl.BlockSpec((SUBLANES, LRU_WIDTH), bcast),
                  pl.BlockSpec((2, LRU_WIDTH // 2, LRU_WIDTH), lambda b, i: (0, 0, 0))],
        out_specs=[pl.BlockSpec((None, tb, LRU_WIDTH), lambda b, i: (b, i, 0)),
                   pl.BlockSpec((None, SUBLANES, LRU_WIDTH), lambda b, i: (b, 0, 0))],
        out_shape=[jax.ShapeDtypeStruct((bsz, t, LRU_WIDTH), BF16),
                   jax.ShapeDtypeStruct((bsz, SUBLANES, LRU_WIDTH), F32)],
        scratch_shapes=[pltpu.VMEM((SUBLANES, LRU_WIDTH), F32),
                        pltpu.VMEM((SUBLANES, LRU_WIDTH), F32)],
        compiler_params=_cparams("parallel", "arbitrary"),
        name="lru",
    )(proj3, proj3, tail8, h0, conv_w8, params8, w_gate)


def _pair_rhs(y, lo_half):
    return jnp.concatenate([jnp.where(lo_half, y, 0.0), jnp.where(lo_half, 0.0, y)], axis=0)


def _pair_mm(x, y, lo_half):
    return _mm(x, _pair_rhs(y, lo_half))


def _pair_inverse(a_list, eye, same16, lo_half):
    def mm(xs, ys):
        return [_pair_mm(x, y, lo_half) for x, y in zip(xs, ys)]

    def plus(xs):
        return [eye + x for x in xs]

    def minus(xs):
        return [eye - x for x in xs]

    d = [jnp.where(same16, a, 0.0) for a in a_list]
    d2 = mm(d, d)
    d4 = mm(d2, d2)
    d8 = mm(d4, d4)
    p = mm(mm(mm(minus(d), plus(d2)), plus(d4)), plus(d8))
    m = mm(p, [a - x for a, x in zip(a_list, d)])
    q = mm(minus(m), plus(mm(m, m)))
    return mm(q, p)


def _dn_kernel(q_ref, k_ref, v_ref, z_ref, ba_ref, tail_ref, s0_ref, cw_ref, hp_ref, on_ref,
               y_ref, sl_ref, s_sc, xt_sc, *, tb, pad):
    t = pl.program_id(1)
    hd = DN_HEAD_DIM
    zero_hd = jnp.zeros((hd, hd), F32)

    @pl.when(t == 0)
    def _():
        for hp in range(DN_HEADS // 2):
            s_sc[hp] = jnp.concatenate(
                [jnp.concatenate([s0_ref[2 * hp], zero_hd], axis=1),
                 jnp.concatenate([zero_hd, s0_ref[2 * hp + 1]], axis=1)], axis=0)
        xt_sc[...] = tail_ref[...]

    qkv = jnp.concatenate([q_ref[...], k_ref[...], v_ref[...]], axis=1).astype(F32)
    act = _causal_conv(qkv, xt_sc[...], cw_ref[...])
    xt_sc[...] = qkv[tb - SUBLANES:, :]
    act = act * _sigmoid(act)
    ba = ba_ref[...]
    beta_t = _sigmoid(ba)
    g_t = pltpu.roll(-jnp.exp(hp_ref[0:1, :]) * _softplus(ba + hp_ref[1:2, :]),
                     LANES - DN_HEADS, axis=1)
    if pad:
        valid = lax.broadcasted_iota(jnp.int32, (tb, 1), 0) >= pad
        act = jnp.where(valid, act, 0.0)
        beta_t = jnp.where(valid, beta_t, 0.0)
        g_t = jnp.where(valid, g_t, 0.0)

    heads = []
    for i in range(3 * DN_HEADS):
        seg = act[:, i * hd:(i + 1) * hd]
        if i < 2 * DN_HEADS:
            nrm = lax.rsqrt(jnp.sum(seg * seg, axis=-1, keepdims=True) + EPS)
            if i < DN_HEADS:
                nrm = nrm * (hd ** -0.5)
            seg = seg * nrm
        heads.append(seg)
    qn, kn, vv = heads[:DN_HEADS], heads[DN_HEADS:2 * DN_HEADS], heads[2 * DN_HEADS:]

    row_c = lax.broadcasted_iota(jnp.int32, (tb, LANES), 0) & (CHUNK - 1)
    cum = g_t
    s = 1
    while s < CHUNK:
        cum = cum + jnp.where(row_c >= s, pltpu.roll(cum, s, axis=0), 0.0)
        s *= 2
    e_cum = jnp.exp(cum)

    ri = lax.broadcasted_iota(jnp.int32, (CHUNK, LANES), 0)
    li = lax.broadcasted_iota(jnp.int32, (CHUNK, LANES), 1)
    lo_half = li < CHUNK
    cj = li & (CHUNK - 1)
    eye_b = ri == cj
    eye = eye_b.astype(F32)
    causal = ri >= cj
    strict = ri > cj
    same16 = (ri // INV_BLOCK) == (cj // INV_BLOCK)
    bd_mask = ((lax.broadcasted_iota(jnp.int32, (2 * hd, 2 * hd), 0) >= hd)
               == (lax.broadcasted_iota(jnp.int32, (2 * hd, 2 * hd), 1) >= hd))
    zero_c = jnp.zeros((CHUNK, hd), F32)
    out_g = on_ref[...]

    def bd_rows(x0, x1):
        z0 = jnp.zeros_like(x0)
        return jnp.concatenate([jnp.concatenate([x0, z0], axis=1),
                                jnp.concatenate([z0, x1], axis=1)], axis=0)

    n_ch = tb // CHUNK
    n_hp = DN_HEADS // 2
    probs = [(ch, hp) for ch in range(n_ch) for hp in range(n_hp)]
    qkm, q_dec, k_dec, rhs, a_list, dec_row = [], [], [], [], [], []
    for ch, hp in probs:
        rows = slice(ch * CHUNK, (ch + 1) * CHUNK)
        h0, h1 = 2 * hp, 2 * hp + 1
        cum_c, beta_c, ecum_c = cum[rows], beta_t[rows], e_cum[rows]
        last = cum_c[CHUNK - 1:CHUNK, :]

        def tile(arr):
            return jnp.where(lo_half, jnp.broadcast_to(arr[:, h0:h0 + 1], (CHUNK, LANES)),
                             jnp.broadcast_to(arr[:, h1:h1 + 1], (CHUNK, LANES)))

        def wide(arr):
            return jnp.concatenate([jnp.broadcast_to(arr[:, h0:h0 + 1], (arr.shape[0], hd)),
                                    jnp.broadcast_to(arr[:, h1:h1 + 1], (arr.shape[0], hd))],
                                   axis=1)

        cum_cp = tile(cum_c)
        cum_rp = jnp.sum(jnp.where(eye_b, cum_cp, 0.0), axis=0, keepdims=True)
        decay = jnp.where(causal, jnp.exp(jnp.where(causal, cum_cp - cum_rp, 0.0)), 0.0)
        q_p = jnp.concatenate([qn[h0][rows], qn[h1][rows]], axis=1)
        k_p = jnp.concatenate([kn[h0][rows], kn[h1][rows]], axis=1)
        v_p = jnp.concatenate([vv[h0][rows], vv[h1][rows]], axis=1)
        qkk = _mm_nt(jnp.concatenate([q_p, k_p], axis=0),
                     bd_rows(kn[h0][rows], kn[h1][rows]))
        qkm.append(qkk[:CHUNK] * decay)
        a_list.append(jnp.where(strict, qkk[CHUNK:] * decay, 0.0) * tile(beta_c))
        vb = v_p * wide(beta_c)
        kb = k_p * wide(beta_c * ecum_c)
        rhs.append(jnp.concatenate(
            [jnp.concatenate([vb[:, :hd], zero_c, kb[:, :hd], zero_c], axis=1),
             jnp.concatenate([zero_c, vb[:, hd:], zero_c, kb[:, hd:]], axis=1)], axis=0))
        q_dec.append(q_p * wide(ecum_c))
        k_dec.append(k_p * wide(jnp.exp(last - cum_c)))
        dec_row.append(wide(jnp.exp(last)))

    t_inv = _pair_inverse(a_list, eye, same16, lo_half)
    sol = [_mm(ti, r) for ti, r in zip(t_inv, rhs)]
    ktuw = [_mm_tn(kd, so) for kd, so in zip(k_dec, sol)]
    quw = [_mm(qm, jnp.concatenate(
               [jnp.concatenate([so[:, :hd], zero_c, so[:, 2 * hd:3 * hd], zero_c], axis=1),
                jnp.concatenate([zero_c, so[:, hd:2 * hd], zero_c, so[:, 3 * hd:]], axis=1)], axis=0))
           for qm, so in zip(qkm, sol)]

    state = [s_sc[hp] for hp in range(n_hp)]
    for ch in range(n_ch):
        rows = slice(ch * CHUNK, (ch + 1) * CHUNK)
        res = []
        for hp in range(n_hp):
            i = ch * n_hp + hp
            k_w = jnp.where(bd_mask, ktuw[i][:, 2 * hd:], 0.0)
            q_eff = q_dec[i] - quw[i][:, 2 * hd:]
            res.append(_mm(jnp.concatenate([k_w, q_eff], axis=0), state[hp]))
        for hp in range(n_hp):
            i = ch * n_hp + hp
            o_p = res[hp][2 * hd:] + quw[i][:, :2 * hd]
            state[hp] = (state[hp] * dec_row[i] - res[hp][:2 * hd]
                         + jnp.where(bd_mask, ktuw[i][:, :2 * hd], 0.0))
            for e in range(2):
                lo = (2 * hp + e) * hd
                o = o_p[:, e * hd:(e + 1) * hd]
                ms = jnp.mean(o * o, axis=-1, keepdims=True)
                z = z_ref[rows, lo:lo + hd].astype(F32)
                out = ((o * lax.rsqrt(ms + EPS)) * out_g) * (z * _sigmoid(z))
                y_ref[rows, lo:lo + hd] = out.astype(y_ref.dtype)
    for hp in range(n_hp):
        s_sc[hp] = state[hp]

    @pl.when(t == pl.num_programs(1) - 1)
    def _():
        for h in range(DN_HEADS):
            e = h % 2
            sl_ref[h] = s_sc[h // 2][e * hd:(e + 1) * hd, e * hd:(e + 1) * hd]


def _deltanet(proj3, ba3, tail8, s0, conv_w8, head_p, out_norm, tb, pad):
    bsz, t, _ = proj3.shape
    assert tb % CHUNK == 0 and (pad == 0 or t == tb)
    bcast = lambda b, i: (0, 0)
    col = lambda j: (lambda b, i: (b, i, j))
    return pl.pallas_call(
        functools.partial(_dn_kernel, tb=tb, pad=pad),
        grid=(bsz, t // tb),
        in_specs=[pl.BlockSpec((None, tb, DN_WIDTH), col(2)),
                  pl.BlockSpec((None, tb, DN_WIDTH), col(3)),
                  pl.BlockSpec((None, tb, DN_WIDTH), col(4)),
                  pl.BlockSpec((None, tb, DN_WIDTH), col(5)),
                  pl.BlockSpec((None, tb, LANES), lambda b, i: (b, i, 0)),
                  pl.BlockSpec((SUBLANES, 3 * DN_WIDTH), bcast),
                  pl.BlockSpec((DN_HEADS, DN_HEAD_DIM, DN_HEAD_DIM), lambda b, i: (0, 0, 0)),
                  pl.BlockSpec((SUBLANES, 3 * DN_WIDTH), bcast),
                  pl.BlockSpec((SUBLANES, LANES), bcast),
                  pl.BlockSpec((1, DN_HEAD_DIM), bcast)],
        out_specs=[pl.BlockSpec((None, tb, DN_WIDTH), lambda b, i: (b, i, 0)),
                   pl.BlockSpec((None, DN_HEADS, DN_HEAD_DIM, DN_HEAD_DIM),
                                lambda b, i: (b, 0, 0, 0))],
        out_shape=[jax.ShapeDtypeStruct((bsz, t, DN_WIDTH), BF16),
                   jax.ShapeDtypeStruct((bsz, DN_HEADS, DN_HEAD_DIM, DN_HEAD_DIM), F32)],
        scratch_shapes=[pltpu.VMEM((DN_HEADS // 2, 2 * DN_HEAD_DIM, 2 * DN_HEAD_DIM), F32),
                        pltpu.VMEM((SUBLANES, 3 * DN_WIDTH), F32)],
        compiler_params=_cparams("parallel", "arbitrary"),
        name="deltanet",
    )(proj3, proj3, proj3, proj3, ba3, tail8, s0, conv_w8, head_p, out_norm)


def _outproj_kernel(yl_ref, yd_ref, x_ref, wo_ref, fg_ref, wr_ref, br_ref,
                    h1_ref, xn_ref, id_ref, gt_ref):
    mix = jnp.dot(yl_ref[...], wo_ref[:LRU_WIDTH, :], preferred_element_type=F32)
    mix = mix + jnp.dot(yd_ref[...], wo_ref[LRU_WIDTH:, :], preferred_element_type=F32)
    h1 = x_ref[...] + mix
    h1_ref[...] = h1
    ms = jnp.mean(h1 * h1, axis=-1, keepdims=True)
    xn = (h1 * lax.rsqrt(ms + EPS)) * fg_ref[...]
    xn_ref[...] = xn.astype(xn_ref.dtype)

    logits = jnp.dot(xn.astype(BF16), wr_ref[...], preferred_element_type=F32) + br_ref[...]
    lane = lax.broadcasted_iota(jnp.int32, logits.shape, 1)
    lanef = lane.astype(F32)
    neg = -jnp.inf
    big = 1e9
    lg = jnp.where(lane < N_GROUPS, logits, neg)
    mg = jnp.max(lg, axis=-1, keepdims=True)
    p_sel = 1.0 / jnp.sum(jnp.exp(lg - mg), axis=-1, keepdims=True)
    g_sel = jnp.min(jnp.where(lg == mg, lanef, big), axis=-1, keepdims=True)
    e_grp = ((lane - N_GROUPS) >> 3).astype(F32)
    in_grp = (lane >= N_GROUPS) & (lane < N_GROUPS + N_EXPERTS) & (e_grp == g_sel)
    le = jnp.where(in_grp, logits, neg)
    m1 = jnp.max(le, axis=-1, keepdims=True)
    i1 = jnp.min(jnp.where(le == m1, lanef, big), axis=-1, keepdims=True)
    le2 = jnp.where(lanef == i1, neg, le)
    m2 = jnp.max(le2, axis=-1, keepdims=True)
    i2 = jnp.min(jnp.where(le2 == m2, lanef, big), axis=-1, keepdims=True)
    e21 = jnp.exp(m2 - m1)
    w1 = p_sel / (1.0 + e21)
    w2 = p_sel * e21 / (1.0 + e21)
    ids = jnp.where(lane == 0, i1, jnp.where(lane == 1, i2, float(N_GROUPS))) - float(N_GROUPS)
    id_ref[...] = ids.astype(jnp.int32)
    gt_ref[...] = jnp.where(lane == 0, w1, jnp.where(lane == 1, w2, 0.0))


def _outproj(y_lru, y_dn, x2, w_out, ffn_g, w_router, b_router, tm):
    n = x2.shape[0]
    row = lambda i: (i, 0)
    fix = lambda i: (0, 0)
    return pl.pallas_call(
        _outproj_kernel,
        grid=(n // tm,),
        in_specs=[pl.BlockSpec((tm, LRU_WIDTH), row),
                  pl.BlockSpec((tm, DN_WIDTH), row),
                  pl.BlockSpec((tm, D_MODEL), row),
                  pl.BlockSpec((D_MODEL, D_MODEL), fix),
                  pl.BlockSpec((1, D_MODEL), fix),
                  pl.BlockSpec((D_MODEL, LANES), fix),
                  pl.BlockSpec((1, LANES), fix)],
        out_specs=[pl.BlockSpec((tm, D_MODEL), row),
                   pl.BlockSpec((tm, D_MODEL), row),
                   pl.BlockSpec((tm, LANES), row),
                   pl.BlockSpec((tm, LANES), row)],
        out_shape=[jax.ShapeDtypeStruct((n, D_MODEL), F32),
                   jax.ShapeDtypeStruct((n, D_MODEL), F32),
                   jax.ShapeDtypeStruct((n, LANES), jnp.int32),
                   jax.ShapeDtypeStruct((n, LANES), F32)],
        compiler_params=_cparams("parallel"),
        name="outproj_router",
    )(y_lru, y_dn, x2, w_out, ffn_g, w_router, b_router)


def _expert_kernel(be_ref, x_ref, wgu_ref, wd_ref, y_ref):
    del be_ref
    gu = jnp.dot(x_ref[...].astype(BF16), wgu_ref[...], preferred_element_type=F32)
    g, u = gu[:, :D_EXPERT], gu[:, D_EXPERT:]
    hmid = (g * _sigmoid(g)) * u
    y_ref[...] = jnp.dot(hmid.astype(BF16), wd_ref[...],
                         preferred_element_type=F32).astype(y_ref.dtype)


def _experts(block_expert, x_buf, w_gu, w_down):
    cap = x_buf.shape[0]
    grid_spec = pltpu.PrefetchScalarGridSpec(
        num_scalar_prefetch=1,
        grid=(cap // MOE_ROWS,),
        in_specs=[pl.BlockSpec((MOE_ROWS, D_MODEL), lambda i, be: (i, 0)),
                  pl.BlockSpec((None, D_MODEL, 2 * D_EXPERT), lambda i, be: (be[i], 0, 0)),
                  pl.BlockSpec((None, D_EXPERT, D_MODEL), lambda i, be: (be[i], 0, 0))],
        out_specs=pl.BlockSpec((MOE_ROWS, D_MODEL), lambda i, be: (i, 0)),
    )
    return pl.pallas_call(
        _expert_kernel,
        grid_spec=grid_spec,
        out_shape=jax.ShapeDtypeStruct((cap, D_MODEL), F32),
        compiler_params=_cparams("arbitrary"),
        name="experts",
    )(block_expert, x_buf, w_gu, w_down)


def _combine_kernel(h1_ref, y0_ref, y1_ref, gt_ref, fg_ref, o_ref):
    gt = gt_ref[...]
    h = h1_ref[...] + gt[:, 0:1] * y0_ref[...] + gt[:, 1:2] * y1_ref[...]
    ms = jnp.mean(h * h, axis=-1, keepdims=True)
    o_ref[...] = (h * lax.rsqrt(ms + EPS)) * fg_ref[...]


def _combine(h1, y_tok, gates, final_g, tm):
    n = h1.shape[0]
    row = lambda i: (i, 0)
    return pl.pallas_call(
        _combine_kernel,
        grid=(n // tm,),
        in_specs=[pl.BlockSpec((tm, D_MODEL), row),
                  pl.BlockSpec((None, tm, D_MODEL), lambda i: (0, i, 0)),
                  pl.BlockSpec((None, tm, D_MODEL), lambda i: (1, i, 0)),
                  pl.BlockSpec((tm, LANES), row),
                  pl.BlockSpec((1, D_MODEL), lambda i: (0, 0))],
        out_specs=pl.BlockSpec((tm, D_MODEL), row),
        out_shape=jax.ShapeDtypeStruct((n, D_MODEL), F32),
        compiler_params=_cparams("parallel"),
        name="combine",
    )(h1, y_tok, y_tok, gates, final_g)


def _block_diag(blocks):
    n, r, c = blocks.shape
    out = jnp.zeros((n * r, n * c), blocks.dtype)
    for i in range(n):
        out = out.at[i * r:(i + 1) * r, i * c:(i + 1) * c].set(blocks[i])
    return out


def _pad_rows(a, rows):
    return jnp.pad(a, ((0, rows - a.shape[0]), (0, 0)))


def _tail8(rows3):
    return jnp.pad(rows3.astype(F32), ((SUBLANES - rows3.shape[0], 0), (0, 0)))


def _dispatch_tables(ids2):
    n = ids2.shape[0]
    n_assign = 2 * n
    eflat = ids2.reshape(-1)
    order = jnp.argsort(eflat, stable=True).astype(jnp.int32)
    sorted_e = eflat[order]
    counts = jnp.sum((eflat[:, None] == jnp.arange(N_EXPERTS, dtype=jnp.int32)[None, :])
                     .astype(jnp.int32), axis=0)
    starts = jnp.cumsum(counts) - counts
    padded = (counts + MOE_ROWS - 1) // MOE_ROWS * MOE_ROWS
    pend = jnp.cumsum(padded)
    pstart = pend - padded
    dest_sorted = pstart[sorted_e] + (jnp.arange(n_assign, dtype=jnp.int32) - starts[sorted_e])
    n_blocks = -(-(n_assign + N_EXPERTS * (MOE_ROWS - 1)) // MOE_ROWS)
    cap = n_blocks * MOE_ROWS
    buf_tok = jnp.zeros((cap,), jnp.int32).at[dest_sorted].set(order // 2)
    dest = jnp.zeros((n_assign,), jnp.int32).at[order].set(dest_sorted)
    block_start = jnp.arange(n_blocks, dtype=jnp.int32) * MOE_ROWS
    block_expert = jnp.minimum(jnp.sum(block_start[:, None] >= pend[None, :], axis=1),
                               N_EXPERTS - 1).astype(jnp.int32)
    return buf_tok, dest.reshape(n, 2), block_expert


def kernel(x, meta_tokens, mix_norm, w_in, lru_conv_w, lru_conv_b, lru_w_r, lru_b_r, lru_w_i,
           lru_b_i, lru_lambda, lru_out_norm, dn_conv_w, dn_a_log, dn_dt_bias, dn_out_norm, w_out,
           ffn_norm, router_group_w, router_group_b, router_expert_w, router_expert_b, moe_w_gate,
           moe_w_up, moe_w_down, final_norm):
    bsz, seq, d = x.shape
    n = bsz * seq
    l = 0

    w_main = w_in[l][:, :PROJ_MAIN].astype(BF16)
    w_ba = jnp.pad(w_in[l][:, PROJ_MAIN:], ((0, 0), (0, LANES - 2 * DN_HEADS))).astype(BF16)
    mix_g = mix_norm[l][None, :]
    lru_cw = _pad_rows(lru_conv_w[l], SUBLANES)
    lru_p = _pad_rows(jnp.stack([lru_conv_b[l], lru_b_r[l], lru_b_i[l], lru_lambda[l],
                                 lru_out_norm[l]]), SUBLANES)
    hb = LRU_BLOCKS // 2
    w_gate = jnp.stack([
        jnp.concatenate([_block_diag(lru_w_r[l][h * hb:(h + 1) * hb]),
                         _block_diag(lru_w_i[l][h * hb:(h + 1) * hb])], axis=1)
        for h in range(2)]).astype(BF16)
    dn_cw = _pad_rows(dn_conv_w[l], SUBLANES)
    head_p = jnp.zeros((SUBLANES, LANES), F32)
    head_p = head_p.at[0, DN_HEADS:2 * DN_HEADS].set(dn_a_log[l])
    head_p = head_p.at[1, DN_HEADS:2 * DN_HEADS].set(dn_dt_bias[l])
    dn_on = dn_out_norm[l][None, :]
    w_o = w_out[l].astype(BF16)
    ffn_g = ffn_norm[l][None, :]
    w_router = jnp.pad(jnp.concatenate([router_group_w[l], router_expert_w[l]], axis=1),
                       ((0, 0), (0, LANES - N_GROUPS - N_EXPERTS))).astype(BF16)
    b_router = jnp.pad(jnp.concatenate([router_group_b[l], router_expert_b[l]]),
                       (0, LANES - N_GROUPS - N_EXPERTS))[None, :]
    w_gu = jnp.concatenate([moe_w_gate[l], moe_w_up[l]], axis=-1).astype(BF16)
    w_dn = moe_w_down[l].astype(BF16)
    final_g = final_norm[None, :]

    proj_m, ba_m = _inproj(meta_tokens, mix_g, w_main, w_ba, N_META)
    zeros_lru = jnp.zeros((SUBLANES, LRU_WIDTH), F32)
    _, h_meta = _lru(proj_m[None], zeros_lru, zeros_lru, lru_cw, lru_p, w_gate, N_META)
    meta_pad = CHUNK - N_META
    _, s_meta = _deltanet(jnp.pad(proj_m, ((meta_pad, 0), (0, 0)))[None],
                          jnp.pad(ba_m, ((meta_pad, 0), (0, 0)))[None],
                          jnp.zeros((SUBLANES, 3 * DN_WIDTH), F32),
                          jnp.zeros((DN_HEADS, DN_HEAD_DIM, DN_HEAD_DIM), F32),
                          dn_cw, head_p, dn_on, CHUNK, meta_pad)
    last = proj_m[N_META - (CONV_WIDTH - 1):]
    lru_tail = _tail8(last[:, :LRU_WIDTH])
    dn_tail = _tail8(last[:, 2 * LRU_WIDTH:2 * LRU_WIDTH + 3 * DN_WIDTH])

    x2 = x.reshape(n, d)
    proj, ba = _inproj(x2, mix_g, w_main, w_ba, 512)
    proj3 = proj.reshape(bsz, seq, PROJ_MAIN)
    ba3 = ba.reshape(bsz, seq, LANES)
    y_lru, _ = _lru(proj3, lru_tail, h_meta[0], lru_cw, lru_p, w_gate, 256)
    y_dn, _ = _deltanet(proj3, ba3, dn_tail, s_meta[0], dn_cw, head_p, dn_on, 256, 0)
    h1, xn, ids, gates = _outproj(y_lru.reshape(n, LRU_WIDTH), y_dn.reshape(n, DN_WIDTH), x2,
                                  w_o, ffn_g, w_router, b_router, 512)

    buf_tok, dest, block_expert = _dispatch_tables(ids[:, :2])
    x_buf = xn[buf_tok]
    y_buf = _experts(block_expert, x_buf, w_gu, w_dn)
    y_tok = jnp.stack([y_buf[dest[:, 0]], y_buf[dest[:, 1]]])
    out = _combine(h1, y_tok, gates, final_g, 512)
    return out.reshape(bsz, seq, d)
```

```python
import functools

import jax
import jax.numpy as jnp
from jax import lax
from jax.experimental import pallas as pl
from jax.experimental.pallas import tpu as pltpu
from jax.experimental.pallas import tpu_sc as plsc

F32 = jnp.float32
BF16 = jnp.bfloat16

D_MODEL = 1024
N_META = 16
CHUNK = 64
CONV_WIDTH = 4
LRU_WIDTH = 512
LRU_BLOCKS = 8
LRU_C = 8.0
DN_HEADS = 4
DN_HEAD_DIM = 128
DN_WIDTH = DN_HEADS * DN_HEAD_DIM
N_GROUPS = 4
EXPERTS_PER_GROUP = 8
N_EXPERTS = N_GROUPS * EXPERTS_PER_GROUP
D_EXPERT = 256
EPS = 1e-6
PROJ_MAIN = 2 * LRU_WIDTH + 4 * DN_WIDTH
LANES = 128
SUBLANES = 8
INV_BLOCK = 16
MOE_ROWS = 256
VMEM_LIMIT = 48 * 1024 * 1024
GATHER_WINDOW = 32


def _cparams(*sem):
    return pltpu.CompilerParams(dimension_semantics=sem, vmem_limit_bytes=VMEM_LIMIT)


def _sigmoid(x):
    return 1.0 / (1.0 + jnp.exp(-x))


def _softplus(x):
    return jnp.maximum(x, 0.0) + jnp.log(1.0 + jnp.exp(-jnp.abs(x)))


def _gelu_tanh(x):
    return 0.5 * x * (1.0 + jnp.tanh(0.7978845608028654 * (x + 0.044715 * (x * x * x))))


def _mm(a, b):
    return jnp.dot(a.astype(BF16), b.astype(BF16), preferred_element_type=F32)


def _mm_nt(a, b):
    return lax.dot_general(a.astype(BF16), b.astype(BF16), (((1,), (1,)), ((), ())),
                           preferred_element_type=F32)


def _mm_tn(a, b):
    return lax.dot_general(a.astype(BF16), b.astype(BF16), (((0,), (0,)), ((), ())),
                           preferred_element_type=F32)


def _delay_rows(x, prev8, k):
    xr = pltpu.roll(x, k, axis=0)
    pr = pltpu.roll(prev8, k, axis=0)
    row = lax.broadcasted_iota(jnp.int32, prev8.shape, 0)
    head = jnp.where(row < k, pr, xr[:SUBLANES])
    if x.shape[0] == SUBLANES:
        return head
    return jnp.concatenate([head, xr[SUBLANES:]], axis=0)


def _causal_conv(x, prev8, w):
    acc = x * w[3:4, :]
    for d in range(1, CONV_WIDTH):
        acc = acc + _delay_rows(x, prev8, d) * w[3 - d:4 - d, :]
    return acc


def _inproj_kernel(x_ref, g_ref, w_ref, wba_ref, proj_ref, ba_ref):
    x = x_ref[...]
    ms = jnp.mean(x * x, axis=-1, keepdims=True)
    u = ((x * lax.rsqrt(ms + EPS)) * g_ref[...]).astype(BF16)
    for n in range(0, PROJ_MAIN, 512):
        proj_ref[:, n:n + 512] = jnp.dot(u, w_ref[:, n:n + 512],
                                         preferred_element_type=F32).astype(proj_ref.dtype)
    ba_ref[...] = jnp.dot(u, wba_ref[...], preferred_element_type=F32)


def _inproj(x2, norm_g, w_main, w_ba, tm):
    n = x2.shape[0]
    return pl.pallas_call(
        _inproj_kernel,
        grid=(n // tm,),
        in_specs=[pl.BlockSpec((tm, D_MODEL), lambda i: (i, 0)),
                  pl.BlockSpec((1, D_MODEL), lambda i: (0, 0)),
                  pl.BlockSpec((D_MODEL, PROJ_MAIN), lambda i: (0, 0)),
                  pl.BlockSpec((D_MODEL, LANES), lambda i: (0, 0))],
        out_specs=[pl.BlockSpec((tm, PROJ_MAIN), lambda i: (i, 0)),
                   pl.BlockSpec((tm, LANES), lambda i: (i, 0))],
        out_shape=[jax.ShapeDtypeStruct((n, PROJ_MAIN), BF16),
                   jax.ShapeDtypeStruct((n, LANES), F32)],
        compiler_params=_cparams("parallel"),
        name="inproj",
    )(x2, norm_g, w_main, w_ba)


def _lru_kernel(xb_ref, gb_ref, tail_ref, h0_ref, cw_ref, p_ref, wg_ref, y_ref, hl_ref,
                h_sc, xt_sc, *, tb):
    t = pl.program_id(1)

    @pl.when(t == 0)
    def _():
        h_sc[...] = h0_ref[...]
        xt_sc[...] = tail_ref[...]

    x = xb_ref[...].astype(F32)
    conv_b, b_r, b_i = p_ref[0:1, :], p_ref[1:2, :], p_ref[2:3, :]
    lam, out_g = p_ref[3:4, :], p_ref[4:5, :]
    xc = _causal_conv(x, xt_sc[...], cw_ref[...]) + conv_b
    xt_sc[...] = x[tb - SUBLANES:, :]

    half = LRU_WIDTH // 2
    xcb = xc.astype(BF16)
    g0 = jnp.dot(xcb[:, :half], wg_ref[0], preferred_element_type=F32)
    g1 = jnp.dot(xcb[:, half:], wg_ref[1], preferred_element_type=F32)
    r = _sigmoid(jnp.concatenate([g0[:, :half], g1[:, :half]], axis=1) + b_r)
    i = _sigmoid(jnp.concatenate([g0[:, half:], g1[:, half:]], axis=1) + b_i)
    log_a = (-LRU_C) * r * _softplus(-lam)
    a = jnp.exp(log_a)
    b = jnp.sqrt(1.0 - jnp.exp(2.0 * log_a)) * (i * xc)

    sub = lax.broadcasted_iota(jnp.int32, (tb, LRU_WIDTH), 0) & (SUBLANES - 1)
    for s in (1, 2, 4):
        keep = sub >= s
        b = jnp.where(keep, a * pltpu.roll(b, s, axis=0), 0.0) + b
        a = jnp.where(keep, a * pltpu.roll(a, s, axis=0), a)
    h = h_sc[...]
    hs = []
    for g in range(tb // SUBLANES):
        hg = a[g * SUBLANES:(g + 1) * SUBLANES] * h + b[g * SUBLANES:(g + 1) * SUBLANES]
        hs.append(hg)
        h = jnp.broadcast_to(hg[SUBLANES - 1:SUBLANES, :], (SUBLANES, LRU_WIDTH))
    h_sc[...] = h
    hseq = jnp.concatenate(hs, axis=0)

    out = hseq * _gelu_tanh(gb_ref[...].astype(F32))
    ms = jnp.mean(out * out, axis=-1, keepdims=True)
    y_ref[...] = ((out * lax.rsqrt(ms + EPS)) * out_g).astype(y_ref.dtype)

    @pl.when(t == pl.num_programs(1) - 1)
    def _():
        hl_ref[...] = h


def _lru(proj3, tail8, h0, conv_w8, params8, w_gate, tb):
    bsz, t, _ = proj3.shape
    bcast = lambda b, i: (0, 0)
    return pl.pallas_call(
        functools.partial(_lru_kernel, tb=tb),
        grid=(bsz, t // tb),
        in_specs=[pl.BlockSpec((None, tb, LRU_WIDTH), lambda b, i: (b, i, 0)),
                  pl.BlockSpec((None, tb, LRU_WIDTH), lambda b, i: (b, i, 1)),
                  pl.BlockSpec((SUBLANES, LRU_WIDTH), bcast),
                  pl.BlockSpec((SUBLANES, LRU_WIDTH), bcast),
                  pl.BlockSpec((SUBLANES, LRU_WIDTH), bcast),
                  pl.BlockSpec((SUBLANES, LRU_WIDTH), bcast),
                  pl.BlockSpec((2, LRU_WIDTH // 2, LRU_WIDTH), lambda b, i: (0, 0, 0))],
        out_specs=[pl.BlockSpec((None, tb, LRU_WIDTH), lambda b, i: (b, i, 0)),
                   pl.BlockSpec((None, SUBLANES, LRU_WIDTH), lambda b, i: (b, 0, 0))],
        out_shape=[jax.ShapeDtypeStruct((bsz, t, LRU_WIDTH), BF16),
                   jax.ShapeDtypeStruct((bsz, SUBLANES, LRU_WIDTH), F32)],
        scratch_shapes=[pltpu.VMEM((SUBLANES, LRU_WIDTH), F32),
                        pltpu.VMEM((SUBLANES, LRU_WIDTH), F32)],
        compiler_params=_cparams("parallel", "arbitrary"),
        name="lru",
    )(proj3, proj3, tail8, h0, conv_w8, params8, w_gate)


def _pair_rhs(y, lo_half):
    return jnp.concatenate([jnp.where(lo_half, y, 0.0), jnp.where(lo_half, 0.0, y)], axis=0)


def _pair_mm(x, y, lo_half):
    return _mm(x, _pair_rhs(y, lo_half))


def _pair_inverse(a_list, eye, same16, lo_half):
    def mm(xs, ys):
        return [_pair_mm(x, y, lo_half) for x, y in zip(xs, ys)]

    def plus(xs):
        return [eye + x for x in xs]

    def minus(xs):
        return [eye - x for x in xs]

    d = [jnp.where(same16, a, 0.0) for a in a_list]
    d2 = mm(d, d)
    d4 = mm(d2, d2)
    d8 = mm(d4, d4)
    p = mm(mm(mm(minus(d), plus(d2)), plus(d4)), plus(d8))
    m = mm(p, [a - x for a, x in zip(a_list, d)])
    q = mm(minus(m), plus(mm(m, m)))
    return mm(q, p)


def _dn_kernel(q_ref, k_ref, v_ref, z_ref, ba_ref, tail_ref, s0_ref, cw_ref, hp_ref, on_ref,
               y_ref, sl_ref, s_sc, xt_sc, *, tb, pad):
    t = pl.program_id(1)
    hd = DN_HEAD_DIM
    zero_hd = jnp.zeros((hd, hd), F32)

    @pl.when(t == 0)
    def _():
        for hp in range(DN_HEADS // 2):
            s_sc[hp] = jnp.concatenate(
                [jnp.concatenate([s0_ref[2 * hp], zero_hd], axis=1),
                 jnp.concatenate([zero_hd, s0_ref[2 * hp + 1]], axis=1)], axis=0)
        xt_sc[...] = tail_ref[...]

    qkv = jnp.concatenate([q_ref[...], k_ref[...], v_ref[...]], axis=1).astype(F32)
    act = _causal_conv(qkv, xt_sc[...], cw_ref[...])
    xt_sc[...] = qkv[tb - SUBLANES:, :]
    act = act * _sigmoid(act)
    ba = ba_ref[...]
    beta_t = _sigmoid(ba)
    g_t = pltpu.roll(-jnp.exp(hp_ref[0:1, :]) * _softplus(ba + hp_ref[1:2, :]),
                     LANES - DN_HEADS, axis=1)
    if pad:
        valid = lax.broadcasted_iota(jnp.int32, (tb, 1), 0) >= pad
        act = jnp.where(valid, act, 0.0)
        beta_t = jnp.where(valid, beta_t, 0.0)
        g_t = jnp.where(valid, g_t, 0.0)

    heads = []
    for i in range(3 * DN_HEADS):
        seg = act[:, i * hd:(i + 1) * hd]
        if i < 2 * DN_HEADS:
            nrm = lax.rsqrt(jnp.sum(seg * seg, axis=-1, keepdims=True) + EPS)
            if i < DN_HEADS:
                nrm = nrm * (hd ** -0.5)
            seg = seg * nrm
        heads.append(seg)
    qn, kn, vv = heads[:DN_HEADS], heads[DN_HEADS:2 * DN_HEADS], heads[2 * DN_HEADS:]

    row_c = lax.broadcasted_iota(jnp.int32, (tb, LANES), 0) & (CHUNK - 1)
    cum = g_t
    s = 1
    while s < CHUNK:
        cum = cum + jnp.where(row_c >= s, pltpu.roll(cum, s, axis=0), 0.0)
        s *= 2
    e_cum = jnp.exp(cum)

    ri = lax.broadcasted_iota(jnp.int32, (CHUNK, LANES), 0)
    li = lax.broadcasted_iota(jnp.int32, (CHUNK, LANES), 1)
    lo_half = li < CHUNK
    cj = li & (CHUNK - 1)
    eye_b = ri == cj
    eye = eye_b.astype(F32)
    causal = ri >= cj
    strict = ri > cj
    same16 = (ri // INV_BLOCK) == (cj // INV_BLOCK)
    bd_mask = ((lax.broadcasted_iota(jnp.int32, (2 * hd, 2 * hd), 0) >= hd)
               == (lax.broadcasted_iota(jnp.int32, (2 * hd, 2 * hd), 1) >= hd))
    zero_c = jnp.zeros((CHUNK, hd), F32)
    out_g = on_ref[...]

    def bd_rows(x0, x1):
        z0 = jnp.zeros_like(x0)
        return jnp.concatenate([jnp.concatenate([x0, z0], axis=1),
                                jnp.concatenate([z0, x1], axis=1)], axis=0)

    n_ch = tb // CHUNK
    n_hp = DN_HEADS // 2
    probs = [(ch, hp) for ch in range(n_ch) for hp in range(n_hp)]
    qkm, q_dec, k_dec, rhs, a_list, dec_row = [], [], [], [], [], []
    for ch, hp in probs:
        rows = slice(ch * CHUNK, (ch + 1) * CHUNK)
        h0, h1 = 2 * hp, 2 * hp + 1
        cum_c, beta_c, ecum_c = cum[rows], beta_t[rows], e_cum[rows]
        last = cum_c[CHUNK - 1:CHUNK, :]

        def tile(arr):
            return jnp.where(lo_half, jnp.broadcast_to(arr[:, h0:h0 + 1], (CHUNK, LANES)),
                             jnp.broadcast_to(arr[:, h1:h1 + 1], (CHUNK, LANES)))

        def wide(arr):
            return jnp.concatenate([jnp.broadcast_to(arr[:, h0:h0 + 1], (arr.shape[0], hd)),
                                    jnp.broadcast_to(arr[:, h1:h1 + 1], (arr.shape[0], hd))],
                                   axis=1)

        cum_cp = tile(cum_c)
        cum_rp = jnp.sum(jnp.where(eye_b, cum_cp, 0.0), axis=0, keepdims=True)
        decay = jnp.where(causal, jnp.exp(jnp.where(causal, cum_cp - cum_rp, 0.0)), 0.0)
        q_p = jnp.concatenate([qn[h0][rows], qn[h1][rows]], axis=1)
        k_p = jnp.concatenate([kn[h0][rows], kn[h1][rows]], axis=1)
        v_p = jnp.concatenate([vv[h0][rows], vv[h1][rows]], axis=1)
        qkk = _mm_nt(jnp.concatenate([q_p, k_p], axis=0),
                     bd_rows(kn[h0][rows], kn[h1][rows]))
        qkm.append(qkk[:CHUNK] * decay)
        a_list.append(jnp.where(strict, qkk[CHUNK:] * decay, 0.0) * tile(beta_c))
        vb = v_p * wide(beta_c)
        kb = k_p * wide(beta_c * ecum_c)
        rhs.append(jnp.concatenate(
            [jnp.concatenate([vb[:, :hd], zero_c, kb[:, :hd], zero_c], axis=1),
             jnp.concatenate([zero_c, vb[:, hd:], zero_c, kb[:, hd:]], axis=1)], axis=0))
        q_dec.append(q_p * wide(ecum_c))
        k_dec.append(k_p * wide(jnp.exp(last - cum_c)))
        dec_row.append(wide(jnp.exp(last)))

    t_inv = _pair_inverse(a_list, eye, same16, lo_half)
    sol = [_mm(ti, r) for ti, r in zip(t_inv, rhs)]
    ktuw = [_mm_tn(kd, so) for kd, so in zip(k_dec, sol)]
    quw = [_mm(qm, jnp.concatenate(
               [jnp.concatenate([so[:, :hd], zero_c, so[:, 2 * hd:3 * hd], zero_c], axis=1),
                jnp.concatenate([zero_c, so[:, hd:2 * hd], zero_c, so[:, 3 * hd:]], axis=1)], axis=0))
           for qm, so in zip(qkm, sol)]

    state = [s_sc[hp] for hp in range(n_hp)]
    for ch in range(n_ch):
        rows = slice(ch * CHUNK, (ch + 1) * CHUNK)
        res = []
        for hp in range(n_hp):
            i = ch * n_hp + hp
            k_w = jnp.where(bd_mask, ktuw[i][:, 2 * hd:], 0.0)
            q_eff = q_dec[i] - quw[i][:, 2 * hd:]
            res.append(_mm(jnp.concatenate([k_w, q_eff], axis=0), state[hp]))
        for hp in range(n_hp):
            i = ch * n_hp + hp
            o_p = res[hp][2 * hd:] + quw[i][:, :2 * hd]
            state[hp] = (state[hp] * dec_row[i] - res[hp][:2 * hd]
                         + jnp.where(bd_mask, ktuw[i][:, :2 * hd], 0.0))
            for e in range(2):
                lo = (2 * hp + e) * hd
                o = o_p[:, e * hd:(e + 1) * hd]
                ms = jnp.mean(o * o, axis=-1, keepdims=True)
                z = z_ref[rows, lo:lo + hd].astype(F32)
                out = ((o * lax.rsqrt(ms + EPS)) * out_g) * (z * _sigmoid(z))
                y_ref[rows, lo:lo + hd] = out.astype(y_ref.dtype)
    for hp in range(n_hp):
        s_sc[hp] = state[hp]

    @pl.when(t == pl.num_programs(1) - 1)
    def _():
        for h in range(DN_HEADS):
            e = h % 2
            sl_ref[h] = s_sc[h // 2][e * hd:(e + 1) * hd, e * hd:(e + 1) * hd]


def _deltanet(proj3, ba3, tail8, s0, conv_w8, head_p, out_norm, tb, pad):
    bsz, t, _ = proj3.shape
    assert tb % CHUNK == 0 and (pad == 0 or t == tb)
    bcast = lambda b, i: (0, 0)
    col = lambda j: (lambda b, i: (b, i, j))
    return pl.pallas_call(
        functools.partial(_dn_kernel, tb=tb, pad=pad),
        grid=(bsz, t // tb),
        in_specs=[pl.BlockSpec((None, tb, DN_WIDTH), col(2)),
                  pl.BlockSpec((None, tb, DN_WIDTH), col(3)),
                  pl.BlockSpec((None, tb, DN_WIDTH), col(4)),
                  pl.BlockSpec((None, tb, DN_WIDTH), col(5)),
                  pl.BlockSpec((None, tb, LANES), lambda b, i: (b, i, 0)),
                  pl.BlockSpec((SUBLANES, 3 * DN_WIDTH), bcast),
                  pl.BlockSpec((DN_HEADS, DN_HEAD_DIM, DN_HEAD_DIM), lambda b, i: (0, 0, 0)),
                  pl.BlockSpec((SUBLANES, 3 * DN_WIDTH), bcast),
                  pl.BlockSpec((SUBLANES, LANES), bcast),
                  pl.BlockSpec((1, DN_HEAD_DIM), bcast)],
        out_specs=[pl.BlockSpec((None, tb, DN_WIDTH), lambda b, i: (b, i, 0)),
                   pl.BlockSpec((None, DN_HEADS, DN_HEAD_DIM, DN_HEAD_DIM),
                                lambda b, i: (b, 0, 0, 0))],
        out_shape=[jax.ShapeDtypeStruct((bsz, t, DN_WIDTH), BF16),
                   jax.ShapeDtypeStruct((bsz, DN_HEADS, DN_HEAD_DIM, DN_HEAD_DIM), F32)],
        scratch_shapes=[pltpu.VMEM((DN_HEADS // 2, 2 * DN_HEAD_DIM, 2 * DN_HEAD_DIM), F32),
                        pltpu.VMEM((SUBLANES, 3 * DN_WIDTH), F32)],
        compiler_params=_cparams("parallel", "arbitrary"),
        name="deltanet",
    )(proj3, proj3, proj3, proj3, ba3, tail8, s0, conv_w8, head_p, out_norm)


def _outproj_kernel(yl_ref, yd_ref, x_ref, wo_ref, fg_ref, wr_ref, br_ref,
                    h1_ref, xn_ref, id_ref, gt_ref):
    mix = jnp.dot(yl_ref[...], wo_ref[:LRU_WIDTH, :], preferred_element_type=F32)
    mix = mix + jnp.dot(yd_ref[...], wo_ref[LRU_WIDTH:, :], preferred_element_type=F32)
    h1 = x_ref[...] + mix
    h1_ref[...] = h1
    ms = jnp.mean(h1 * h1, axis=-1, keepdims=True)
    xn = (h1 * lax.rsqrt(ms + EPS)) * fg_ref[...]
    xn_ref[...] = xn.astype(xn_ref.dtype)

    logits = jnp.dot(xn.astype(BF16), wr_ref[...], preferred_element_type=F32) + br_ref[...]
    lane = lax.broadcasted_iota(jnp.int32, logits.shape, 1)
    lanef = lane.astype(F32)
    neg = -jnp.inf
    big = 1e9
    lg = jnp.where(lane < N_GROUPS, logits, neg)
    mg = jnp.max(lg, axis=-1, keepdims=True)
    p_sel = 1.0 / jnp.sum(jnp.exp(lg - mg), axis=-1, keepdims=True)
    g_sel = jnp.min(jnp.where(lg == mg, lanef, big), axis=-1, keepdims=True)
    e_grp = ((lane - N_GROUPS) >> 3).astype(F32)
    in_grp = (lane >= N_GROUPS) & (lane < N_GROUPS + N_EXPERTS) & (e_grp == g_sel)
    le = jnp.where(in_grp, logits, neg)
    m1 = jnp.max(le, axis=-1, keepdims=True)
    i1 = jnp.min(jnp.where(le == m1, lanef, big), axis=-1, keepdims=True)
    le2 = jnp.where(lanef == i1, neg, le)
    m2 = jnp.max(le2, axis=-1, keepdims=True)
    i2 = jnp.min(jnp.where(le2 == m2, lanef, big), axis=-1, keepdims=True)
    e21 = jnp.exp(m2 - m1)
    w1 = p_sel / (1.0 + e21)
    w2 = p_sel * e21 / (1.0 + e21)
    ids = jnp.where(lane == 0, i1, jnp.where(lane == 1, i2, float(N_GROUPS))) - float(N_GROUPS)
    id_ref[...] = ids.astype(jnp.int32)
    gt_ref[...] = jnp.where(lane == 0, w1, jnp.where(lane == 1, w2, 0.0))


def _outproj(y_lru, y_dn, x2, w_out, ffn_g, w_router, b_router, tm):
    n = x2.shape[0]
    row = lambda i: (i, 0)
    fix = lambda i: (0, 0)
    return pl.pallas_call(
        _outproj_kernel,
        grid=(n // tm,),
        in_specs=[pl.BlockSpec((tm, LRU_WIDTH), row),
                  pl.BlockSpec((tm, DN_WIDTH), row),
                  pl.BlockSpec((tm, D_MODEL), row),
                  pl.BlockSpec((D_MODEL, D_MODEL), fix),
                  pl.BlockSpec((1, D_MODEL), fix),
                  pl.BlockSpec((D_MODEL, LANES), fix),
                  pl.BlockSpec((1, LANES), fix)],
        out_specs=[pl.BlockSpec((tm, D_MODEL), row),
                   pl.BlockSpec((tm, D_MODEL), row),
                   pl.BlockSpec((tm, LANES), row),
                   pl.BlockSpec((tm, LANES), row)],
        out_shape=[jax.ShapeDtypeStruct((n, D_MODEL), F32),
                   jax.ShapeDtypeStruct((n, D_MODEL), F32),
                   jax.ShapeDtypeStruct((n, LANES), jnp.int32),
                   jax.ShapeDtypeStruct((n, LANES), F32)],
        compiler_params=_cparams("parallel"),
        name="outproj_router",
    )(y_lru, y_dn, x2, w_out, ffn_g, w_router, b_router)


def _expert_kernel(be_ref, x_ref, wgu_ref, wd_ref, y_ref):
    del be_ref
    gu = jnp.dot(x_ref[...].astype(BF16), wgu_ref[...], preferred_element_type=F32)
    g, u = gu[:, :D_EXPERT], gu[:, D_EXPERT:]
    hmid = (g * _sigmoid(g)) * u
    y_ref[...] = jnp.dot(hmid.astype(BF16), wd_ref[...],
                         preferred_element_type=F32).astype(y_ref.dtype)


def _experts(block_expert, x_buf, w_gu, w_down):
    cap = x_buf.shape[0]
    grid_spec = pltpu.PrefetchScalarGridSpec(
        num_scalar_prefetch=1,
        grid=(cap // MOE_ROWS,),
        in_specs=[pl.BlockSpec((MOE_ROWS, D_MODEL), lambda i, be: (i, 0)),
                  pl.BlockSpec((None, D_MODEL, 2 * D_EXPERT), lambda i, be: (be[i], 0, 0)),
                  pl.BlockSpec((None, D_EXPERT, D_MODEL), lambda i, be: (be[i], 0, 0))],
        out_specs=pl.BlockSpec((MOE_ROWS, D_MODEL), lambda i, be: (i, 0)),
    )
    return pl.pallas_call(
        _expert_kernel,
        grid_spec=grid_spec,
        out_shape=jax.ShapeDtypeStruct((cap, D_MODEL), F32),
        compiler_params=_cparams("arbitrary"),
        name="experts",
    )(block_expert, x_buf, w_gu, w_down)


def _row_gather(table, idx):
    n_idx = idx.shape[0]
    d = table.shape[1]
    sc = plsc.get_sparse_core_info()
    n_workers = sc.num_cores * sc.num_subcores
    w = GATHER_WINDOW
    per_w = n_idx // n_workers
    n_steps = per_w // w
    assert per_w * n_workers == n_idx and n_steps * w == per_w and n_steps % 2 == 0
    mesh = plsc.VectorSubcoreMesh(core_axis_name="core", subcore_axis_name="subcore")

    @functools.partial(
        pl.kernel, out_type=jax.ShapeDtypeStruct((n_idx, d), table.dtype), mesh=mesh,
        scratch_types=[pltpu.VMEM((per_w,), jnp.int32),
                       pltpu.VMEM((2, w, d), table.dtype),
                       pltpu.SemaphoreType.DMA((2,)),
                       pltpu.SemaphoreType.DMA((2,))])
    def gather(x_hbm, i_hbm, o_hbm, idx_v, rows_v, g_sem, w_sem):
        wid = lax.axis_index("subcore") * sc.num_cores + lax.axis_index("core")
        base = wid * per_w
        pltpu.sync_copy(i_hbm.at[pl.ds(base, per_w)], idx_v)

        def fetch(s, b):
            return pltpu.make_async_copy(x_hbm.at[idx_v.at[pl.ds(s * w, w)]], rows_v.at[b],
                                         g_sem.at[b])

        def flush(s, b):
            return pltpu.make_async_copy(rows_v.at[b], o_hbm.at[pl.ds(base + s * w, w)],
                                         w_sem.at[b])

        fetch(0, 0).start()

        @pl.loop(0, n_steps, step=2)
        def _(s2):
            for b in range(2):
                s = s2 + b

                @pl.when(s + 1 < n_steps)
                def _():
                    @pl.when(s >= 1)
                    def _():
                        flush(s - 1, 1 - b).wait()
                    fetch(s + 1, 1 - b).start()

                fetch(s, b).wait()
                flush(s, b).start()

        flush(n_steps - 2, 0).wait()
        flush(n_steps - 1, 1).wait()

    return gather(table, idx)


def _combine_kernel(h1_ref, y0_ref, y1_ref, gt_ref, fg_ref, o_ref):
    gt = gt_ref[...]
    h = h1_ref[...] + gt[:, 0:1] * y0_ref[...] + gt[:, 1:2] * y1_ref[...]
    ms = jnp.mean(h * h, axis=-1, keepdims=True)
    o_ref[...] = (h * lax.rsqrt(ms + EPS)) * fg_ref[...]


def _combine(h1, y_tok, gates, final_g, tm):
    n = h1.shape[0]
    row = lambda i: (i, 0)
    return pl.pallas_call(
        _combine_kernel,
        grid=(n // tm,),
        in_specs=[pl.BlockSpec((tm, D_MODEL), row),
                  pl.BlockSpec((tm, D_MODEL), row),
                  pl.BlockSpec((tm, D_MODEL), lambda i: (i + n // tm, 0)),
                  pl.BlockSpec((tm, LANES), row),
                  pl.BlockSpec((1, D_MODEL), lambda i: (0, 0))],
        out_specs=pl.BlockSpec((tm, D_MODEL), row),
        out_shape=jax.ShapeDtypeStruct((n, D_MODEL), F32),
        compiler_params=_cparams("parallel"),
        name="combine",
    )(h1, y_tok, y_tok, gates, final_g)


def _block_diag(blocks):
    n, r, c = blocks.shape
    out = jnp.zeros((n * r, n * c), blocks.dtype)
    for i in range(n):
        out = out.at[i * r:(i + 1) * r, i * c:(i + 1) * c].set(blocks[i])
    return out


def _pad_rows(a, rows):
    return jnp.pad(a, ((0, rows - a.shape[0]), (0, 0)))


def _tail8(rows3):
    return jnp.pad(rows3.astype(F32), ((SUBLANES - rows3.shape[0], 0), (0, 0)))


def _dispatch_tables(ids2):
    n = ids2.shape[0]
    n_assign = 2 * n
    eflat = ids2.T.reshape(-1)
    iota = jnp.arange(n_assign, dtype=jnp.int32)
    _, order = lax.sort((eflat, iota), num_keys=1, is_stable=True)
    _, rank = lax.sort((order, iota), num_keys=1, is_stable=False)
    counts = jnp.sum((eflat[:, None] == jnp.arange(N_EXPERTS, dtype=jnp.int32)[None, :])
                     .astype(jnp.int32), axis=0)
    starts = jnp.cumsum(counts) - counts
    padded = (counts + MOE_ROWS - 1) // MOE_ROWS * MOE_ROWS
    pend = jnp.cumsum(padded)
    pstart = pend - padded
    n_blocks = -(-(n_assign + N_EXPERTS * (MOE_ROWS - 1)) // MOE_ROWS)
    cap = n_blocks * MOE_ROWS
    block_start = jnp.arange(n_blocks, dtype=jnp.int32) * MOE_ROWS
    block_expert = jnp.minimum(jnp.sum(block_start[:, None] >= pend[None, :], axis=1),
                               N_EXPERTS - 1).astype(jnp.int32)
    slot = jnp.arange(cap, dtype=jnp.int32)
    slot_e = jnp.repeat(block_expert, MOE_ROWS)
    j = jnp.clip(slot - pstart[slot_e] + starts[slot_e], 0, n_assign - 1)
    buf_tok = order[j] % n
    dest = pstart[eflat] + rank - starts[eflat]
    return buf_tok, dest, block_expert


def kernel(x, meta_tokens, mix_norm, w_in, lru_conv_w, lru_conv_b, lru_w_r, lru_b_r, lru_w_i,
           lru_b_i, lru_lambda, lru_out_norm, dn_conv_w, dn_a_log, dn_dt_bias, dn_out_norm, w_out,
           ffn_norm, router_group_w, router_group_b, router_expert_w, router_expert_b, moe_w_gate,
           moe_w_up, moe_w_down, final_norm):
    bsz, seq, d = x.shape
    n = bsz * seq
    l = 0

    w_main = w_in[l][:, :PROJ_MAIN].astype(BF16)
    w_ba = jnp.pad(w_in[l][:, PROJ_MAIN:], ((0, 0), (0, LANES - 2 * DN_HEADS))).astype(BF16)
    mix_g = mix_norm[l][None, :]
    lru_cw = _pad_rows(lru_conv_w[l], SUBLANES)
    lru_p = _pad_rows(jnp.stack([lru_conv_b[l], lru_b_r[l], lru_b_i[l], lru_lambda[l],
                                 lru_out_norm[l]]), SUBLANES)
    hb = LRU_BLOCKS // 2
    w_gate = jnp.stack([
        jnp.concatenate([_block_diag(lru_w_r[l][h * hb:(h + 1) * hb]),
                         _block_diag(lru_w_i[l][h * hb:(h + 1) * hb])], axis=1)
        for h in range(2)]).astype(BF16)
    dn_cw = _pad_rows(dn_conv_w[l], SUBLANES)
    head_p = jnp.zeros((SUBLANES, LANES), F32)
    head_p = head_p.at[0, DN_HEADS:2 * DN_HEADS].set(dn_a_log[l])
    head_p = head_p.at[1, DN_HEADS:2 * DN_HEADS].set(dn_dt_bias[l])
    dn_on = dn_out_norm[l][None, :]
    w_o = w_out[l].astype(BF16)
    ffn_g = ffn_norm[l][None, :]
    w_router = jnp.pad(jnp.concatenate([router_group_w[l], router_expert_w[l]], axis=1),
                       ((0, 0), (0, LANES - N_GROUPS - N_EXPERTS))).astype(BF16)
    b_router = jnp.pad(jnp.concatenate([router_group_b[l], router_expert_b[l]]),
                       (0, LANES - N_GROUPS - N_EXPERTS))[None, :]
    w_gu = jnp.concatenate([moe_w_gate[l], moe_w_up[l]], axis=-1).astype(BF16)
    w_dn = moe_w_down[l].astype(BF16)
    final_g = final_norm[None, :]

    proj_m, ba_m = _inproj(meta_tokens, mix_g, w_main, w_ba, N_META)
    zeros_lru = jnp.zeros((SUBLANES, LRU_WIDTH), F32)
    _, h_meta = _lru(proj_m[None], zeros_lru, zeros_lru, lru_cw, lru_p, w_gate, N_META)
    meta_pad = CHUNK - N_META
    _, s_meta = _deltanet(jnp.pad(proj_m, ((meta_pad, 0), (0, 0)))[None],
                          jnp.pad(ba_m, ((meta_pad, 0), (0, 0)))[None],
                          jnp.zeros((SUBLANES, 3 * DN_WIDTH), F32),
                          jnp.zeros((DN_HEADS, DN_HEAD_DIM, DN_HEAD_DIM), F32),
                          dn_cw, head_p, dn_on, CHUNK, meta_pad)
    last = proj_m[N_META - (CONV_WIDTH - 1):]
    lru_tail = _tail8(last[:, :LRU_WIDTH])
    dn_tail = _tail8(last[:, 2 * LRU_WIDTH:2 * LRU_WIDTH + 3 * DN_WIDTH])

    x2 = x.reshape(n, d)
    proj, ba = _inproj(x2, mix_g, w_main, w_ba, 512)
    proj3 = proj.reshape(bsz, seq, PROJ_MAIN)
    ba3 = ba.reshape(bsz, seq, LANES)
    y_lru, _ = _lru(proj3, lru_tail, h_meta[0], lru_cw, lru_p, w_gate, 256)
    y_dn, _ = _deltanet(proj3, ba3, dn_tail, s_meta[0], dn_cw, head_p, dn_on, 256, 0)
    h1, xn, ids, gates = _outproj(y_lru.reshape(n, LRU_WIDTH), y_dn.reshape(n, DN_WIDTH), x2,
                                  w_o, ffn_g, w_router, b_router, 512)

    buf_tok, dest, block_expert = _dispatch_tables(ids[:, :2])
    x_buf = _row_gather(xn, buf_tok)
    y_buf = _experts(block_expert, x_buf, w_gu, w_dn)
    y_tok = _row_gather(y_buf, dest)
    out = _combine(h1, y_tok, gates, final_g, 512)
    return out.reshape(bsz, seq, d)
```

```python
import functools

import jax
import jax.numpy as jnp
from jax import lax
from jax.experimental import pallas as pl
from jax.experimental.pallas import tpu as pltpu
from jax.experimental.pallas import tpu_sc as plsc

F32 = jnp.float32
BF16 = jnp.bfloat16

D_MODEL = 1024
N_META = 16
CHUNK = 64
CONV_WIDTH = 4
LRU_WIDTH = 512
LRU_BLOCKS = 8
LRU_C = 8.0
DN_HEADS = 4
DN_HEAD_DIM = 128
DN_WIDTH = DN_HEADS * DN_HEAD_DIM
N_GROUPS = 4
EXPERTS_PER_GROUP = 8
N_EXPERTS = N_GROUPS * EXPERTS_PER_GROUP
D_EXPERT = 256
EPS = 1e-6
PROJ_MAIN = 2 * LRU_WIDTH + 4 * DN_WIDTH
LANES = 128
SUBLANES = 8
INV_BLOCK = 16
MOE_ROWS = 256
VMEM_LIMIT = 48 * 1024 * 1024
GATHER_WINDOW = 64


def _cparams(*sem):
    return pltpu.CompilerParams(dimension_semantics=sem, vmem_limit_bytes=VMEM_LIMIT)


def _sigmoid(x):
    return 0.5 * jnp.tanh(0.5 * x) + 0.5


def _pack_bf16_pairs(x):
    c = x.shape[1] // 2

    def rne(v):
        b = lax.bitcast_convert_type(v, jnp.uint32)
        return b + jnp.uint32(0x7FFF) + ((b >> 16) & jnp.uint32(1))

    return (rne(x[:, :c]) >> 16) | (rne(x[:, c:]) & jnp.uint32(0xFFFF0000))


def _unpack_bf16_pairs(w):
    lo = lax.bitcast_convert_type(w << 16, F32)
    hi = lax.bitcast_convert_type(w & jnp.uint32(0xFFFF0000), F32)
    return jnp.concatenate([lo, hi], axis=1)


def _softplus(x):
    return jnp.maximum(x, 0.0) + jnp.log(1.0 + jnp.exp(-jnp.abs(x)))


def _gelu_tanh(x):
    return 0.5 * x * (1.0 + jnp.tanh(0.7978845608028654 * (x + 0.044715 * (x * x * x))))


def _mm(a, b):
    return jnp.dot(a.astype(BF16), b.astype(BF16), preferred_element_type=F32)


def _mm_nt(a, b):
    return lax.dot_general(a.astype(BF16), b.astype(BF16), (((1,), (1,)), ((), ())),
                           preferred_element_type=F32)


def _mm_tn(a, b):
    return lax.dot_general(a.astype(BF16), b.astype(BF16), (((0,), (0,)), ((), ())),
                           preferred_element_type=F32)


def _delay_rows(x, prev8, k):
    xr = pltpu.roll(x, k, axis=0)
    pr = pltpu.roll(prev8, k, axis=0)
    row = lax.broadcasted_iota(jnp.int32, prev8.shape, 0)
    head = jnp.where(row < k, pr, xr[:SUBLANES])
    if x.shape[0] == SUBLANES:
        return head
    return jnp.concatenate([head, xr[SUBLANES:]], axis=0)


def _causal_conv(x, prev8, w):
    acc = x * w[3:4, :]
    for d in range(1, CONV_WIDTH):
        acc = acc + _delay_rows(x, prev8, d) * w[3 - d:4 - d, :]
    return acc


def _inproj_kernel(x_ref, g_ref, w_ref, wba_ref, proj_ref, ba_ref):
    x = x_ref[...]
    ms = jnp.mean(x * x, axis=-1, keepdims=True)
    u = ((x * lax.rsqrt(ms + EPS)) * g_ref[...]).astype(BF16)
    for n in range(0, PROJ_MAIN, 512):
        proj_ref[:, n:n + 512] = jnp.dot(u, w_ref[:, n:n + 512],
                                         preferred_element_type=F32).astype(proj_ref.dtype)
    ba_ref[...] = jnp.dot(u, wba_ref[...], preferred_element_type=F32)


def _inproj(x2, norm_g, w_main, w_ba, tm):
    n = x2.shape[0]
    return pl.pallas_call(
        _inproj_kernel,
        grid=(n // tm,),
        in_specs=[pl.BlockSpec((tm, D_MODEL), lambda i: (i, 0)),
                  pl.BlockSpec((1, D_MODEL), lambda i: (0, 0)),
                  pl.BlockSpec((D_MODEL, PROJ_MAIN), lambda i: (0, 0)),
                  pl.BlockSpec((D_MODEL, LANES), lambda i: (0, 0))],
        out_specs=[pl.BlockSpec((tm, PROJ_MAIN), lambda i: (i, 0)),
                   pl.BlockSpec((tm, LANES), lambda i: (i, 0))],
        out_shape=[jax.ShapeDtypeStruct((n, PROJ_MAIN), BF16),
                   jax.ShapeDtypeStruct((n, LANES), F32)],
        compiler_params=_cparams("parallel"),
        name="inproj",
    )(x2, norm_g, w_main, w_ba)


def _lru_kernel(xb_ref, gb_ref, tail_ref, h0_ref, cw_ref, p_ref, wg_ref, y_ref, hl_ref,
                h_sc, xt_sc, *, tb):
    t = pl.program_id(1)

    @pl.when(t == 0)
    def _():
        h_sc[...] = h0_ref[...]
        xt_sc[...] = tail_ref[...]

    x = xb_ref[...].astype(F32)
    conv_b, b_r, b_i = p_ref[0:1, :], p_ref[1:2, :], p_ref[2:3, :]
    lam, out_g = p_ref[3:4, :], p_ref[4:5, :]
    xc = _causal_conv(x, xt_sc[...], cw_ref[...]) + conv_b
    xt_sc[...] = x[tb - SUBLANES:, :]

    half = LRU_WIDTH // 2
    xcb = xc.astype(BF16)
    g0 = jnp.dot(xcb[:, :half], wg_ref[0], preferred_element_type=F32)
    g1 = jnp.dot(xcb[:, half:], wg_ref[1], preferred_element_type=F32)
    r = _sigmoid(jnp.concatenate([g0[:, :half], g1[:, :half]], axis=1) + b_r)
    i = _sigmoid(jnp.concatenate([g0[:, half:], g1[:, half:]], axis=1) + b_i)
    log_a = (-LRU_C) * r * _softplus(-lam)
    a = jnp.exp(log_a)
    b = jnp.sqrt(1.0 - jnp.exp(2.0 * log_a)) * (i * xc)

    sub = lax.broadcasted_iota(jnp.int32, (tb, LRU_WIDTH), 0) & (SUBLANES - 1)
    for s in (1, 2, 4):
        keep = sub >= s
        b = jnp.where(keep, a * pltpu.roll(b, s, axis=0), 0.0) + b
        a = jnp.where(keep, a * pltpu.roll(a, s, axis=0), a)
    h = h_sc[...]
    hs = []
    for g in range(tb // SUBLANES):
        hg = a[g * SUBLANES:(g + 1) * SUBLANES] * h + b[g * SUBLANES:(g + 1) * SUBLANES]
        hs.append(hg)
        h = jnp.broadcast_to(hg[SUBLANES - 1:SUBLANES, :], (SUBLANES, LRU_WIDTH))
    h_sc[...] = h
    hseq = jnp.concatenate(hs, axis=0)

    out = hseq * _gelu_tanh(gb_ref[...].astype(F32))
    ms = jnp.mean(out * out, axis=-1, keepdims=True)
    y_ref[...] = ((out * lax.rsqrt(ms + EPS)) * out_g).astype(y_ref.dtype)

    @pl.when(t == pl.num_programs(1) - 1)
    def _():
        hl_ref[...] = h


def _lru(proj3, tail8, h0, conv_w8, params8, w_gate, tb):
    bsz, t, _ = proj3.shape
    bcast = lambda b, i: (0, 0)
    return pl.pallas_call(
        functools.partial(_lru_kernel, tb=tb),
        grid=(bsz, t // tb),
        in_specs=[pl.BlockSpec((None, tb, LRU_WIDTH), lambda b, i: (b, i, 0)),
                  pl.BlockSpec((None, tb, LRU_WIDTH), lambda b, i: (b, i, 1)),
                  pl.BlockSpec((SUBLANES, LRU_WIDTH), bcast),
                  pl.BlockSpec((SUBLANES, LRU_WIDTH), bcast),
                  pl.BlockSpec((SUBLANES, LRU_WIDTH), bcast),
                  pl.BlockSpec((SUBLANES, LRU_WIDTH), bcast),
                  pl.BlockSpec((2, LRU_WIDTH // 2, LRU_WIDTH), lambda b, i: (0, 0, 0))],
        out_specs=[pl.BlockSpec((None, tb, LRU_WIDTH), lambda b, i: (b, i, 0)),
                   pl.BlockSpec((None, SUBLANES, LRU_WIDTH), lambda b, i: (b, 0, 0))],
        out_shape=[jax.ShapeDtypeStruct((bsz, t, LRU_WIDTH), BF16),
                   jax.ShapeDtypeStruct((bsz, SUBLANES, LRU_WIDTH), F32)],
        scratch_shapes=[pltpu.VMEM((SUBLANES, LRU_WIDTH), F32),
                        pltpu.VMEM((SUBLANES, LRU_WIDTH), F32)],
        compiler_params=_cparams("parallel", "arbitrary"),
        name="lru",
    )(proj3, proj3, tail8, h0, conv_w8, params8, w_gate)


def _pair_mm(x, y, half_masks):
    yb = y.astype(BF16)
    rhs = jnp.concatenate([yb * half_masks[0], yb * half_masks[1]], axis=0)
    return jnp.dot(x.astype(BF16), rhs, preferred_element_type=F32)


def _pair_inverse(a_list, eye, same16, half_masks):
    def mm(xs, ys):
        return [_pair_mm(x, y, half_masks) for x, y in zip(xs, ys)]

    def plus(xs):
        return [eye + x for x in xs]

    def minus(xs):
        return [eye - x for x in xs]

    d = [jnp.where(same16, a, 0.0) for a in a_list]
    d2 = mm(d, d)
    d4 = mm(d2, d2)
    d8 = mm(d4, d4)
    p = mm(mm(mm(minus(d), plus(d2)), plus(d4)), plus(d8))
    m = mm(p, [a - x for a, x in zip(a_list, d)])
    q = mm(minus(m), plus(mm(m, m)))
    return mm(q, p)


def _dn_kernel(q_ref, k_ref, v_ref, z_ref, ba_ref, tail_ref, s0_ref, cw_ref, hp_ref, on_ref,
               y_ref, sl_ref, s_sc, xt_sc, *, tb, pad):
    t = pl.program_id(1)
    hd = DN_HEAD_DIM
    zero_hd = jnp.zeros((hd, hd), F32)

    @pl.when(t == 0)
    def _():
        for hp in range(DN_HEADS // 2):
            s_sc[hp] = jnp.concatenate(
                [jnp.concatenate([s0_ref[2 * hp], zero_hd], axis=1),
                 jnp.concatenate([zero_hd, s0_ref[2 * hp + 1]], axis=1)], axis=0)
        xt_sc[...] = tail_ref[...]

    qkv = jnp.concatenate([q_ref[...], k_ref[...], v_ref[...]], axis=1).astype(F32)
    act = _causal_conv(qkv, xt_sc[...], cw_ref[...])
    xt_sc[...] = qkv[tb - SUBLANES:, :]
    act = act * _sigmoid(act)
    ba = ba_ref[...]
    beta_t = _sigmoid(ba)
    g_t = pltpu.roll(-jnp.exp(hp_ref[0:1, :]) * _softplus(ba + hp_ref[1:2, :]),
                     LANES - DN_HEADS, axis=1)
    if pad:
        valid = lax.broadcasted_iota(jnp.int32, (tb, 1), 0) >= pad
        act = jnp.where(valid, act, 0.0)
        beta_t = jnp.where(valid, beta_t, 0.0)
        g_t = jnp.where(valid, g_t, 0.0)

    heads = []
    for i in range(3 * DN_HEADS):
        seg = act[:, i * hd:(i + 1) * hd]
        if i < 2 * DN_HEADS:
            nrm = lax.rsqrt(jnp.sum(seg * seg, axis=-1, keepdims=True) + EPS)
            if i < DN_HEADS:
                nrm = nrm * (hd ** -0.5)
            seg = seg * nrm
        heads.append(seg)
    qn, kn, vv = heads[:DN_HEADS], heads[DN_HEADS:2 * DN_HEADS], heads[2 * DN_HEADS:]

    row_c = lax.broadcasted_iota(jnp.int32, (tb, LANES), 0) & (CHUNK - 1)
    cum = g_t
    s = 1
    while s < CHUNK:
        cum = cum + jnp.where(row_c >= s, pltpu.roll(cum, s, axis=0), 0.0)
        s *= 2
    e_cum = jnp.exp(cum)

    ri = lax.broadcasted_iota(jnp.int32, (CHUNK, LANES), 0)
    li = lax.broadcasted_iota(jnp.int32, (CHUNK, LANES), 1)
    lo_half = li < CHUNK
    half_masks = (lo_half.astype(BF16), (li >= CHUNK).astype(BF16))
    cj = li & (CHUNK - 1)
    eye_b = ri == cj
    eye = eye_b.astype(F32)
    causal = ri >= cj
    strict = ri > cj
    same16 = (ri // INV_BLOCK) == (cj // INV_BLOCK)
    bd_mask = ((lax.broadcasted_iota(jnp.int32, (2 * hd, 2 * hd), 0) >= hd)
               == (lax.broadcasted_iota(jnp.int32, (2 * hd, 2 * hd), 1) >= hd))
    zero_c = jnp.zeros((CHUNK, hd), F32)
    out_g = on_ref[...]

    def bd_rows(x0, x1):
        z0 = jnp.zeros_like(x0)
        return jnp.concatenate([jnp.concatenate([x0, z0], axis=1),
                                jnp.concatenate([z0, x1], axis=1)], axis=0)

    n_ch = tb // CHUNK
    n_hp = DN_HEADS // 2
    probs = [(ch, hp) for ch in range(n_ch) for hp in range(n_hp)]
    qkm, q_dec, k_dec, rhs, a_list, dec_row = [], [], [], [], [], []
    for ch, hp in probs:
        rows = slice(ch * CHUNK, (ch + 1) * CHUNK)
        h0, h1 = 2 * hp, 2 * hp + 1
        cum_c, beta_c, ecum_c = cum[rows], beta_t[rows], e_cum[rows]
        last = cum_c[CHUNK - 1:CHUNK, :]

        def tile(arr):
            return jnp.where(lo_half, jnp.broadcast_to(arr[:, h0:h0 + 1], (CHUNK, LANES)),
                             jnp.broadcast_to(arr[:, h1:h1 + 1], (CHUNK, LANES)))

        def wide(arr):
            return jnp.concatenate([jnp.broadcast_to(arr[:, h0:h0 + 1], (arr.shape[0], hd)),
                                    jnp.broadcast_to(arr[:, h1:h1 + 1], (arr.shape[0], hd))],
                                   axis=1)

        cum_cp = tile(cum_c)
        cum_rp = jnp.sum(jnp.where(eye_b, cum_cp, 0.0), axis=0, keepdims=True)
        decay = jnp.where(causal, jnp.exp(jnp.where(causal, cum_cp - cum_rp, 0.0)), 0.0)
        q_p = jnp.concatenate([qn[h0][rows], qn[h1][rows]], axis=1)
        k_p = jnp.concatenate([kn[h0][rows], kn[h1][rows]], axis=1)
        v_p = jnp.concatenate([vv[h0][rows], vv[h1][rows]], axis=1)
        qkk = _mm_nt(jnp.concatenate([q_p, k_p], axis=0),
                     bd_rows(kn[h0][rows], kn[h1][rows]))
        qkm.append(qkk[:CHUNK] * decay)
        a_list.append(jnp.where(strict, qkk[CHUNK:] * decay, 0.0) * tile(beta_c))
        vb = v_p * wide(beta_c)
        kb = k_p * wide(beta_c * ecum_c)
        rhs.append(jnp.concatenate(
            [jnp.concatenate([vb[:, :hd], zero_c, kb[:, :hd], zero_c], axis=1),
             jnp.concatenate([zero_c, vb[:, hd:], zero_c, kb[:, hd:]], axis=1)], axis=0))
        q_dec.append(q_p * wide(ecum_c))
        k_dec.append(k_p * wide(jnp.exp(last - cum_c)))
        dec_row.append(wide(jnp.exp(last)))

    t_inv = _pair_inverse(a_list, eye, same16, half_masks)
    sol = [_mm(ti, r) for ti, r in zip(t_inv, rhs)]
    ktuw = [_mm_tn(kd, so) for kd, so in zip(k_dec, sol)]
    quw = [_mm(qm, jnp.concatenate(
               [jnp.concatenate([so[:, :hd], zero_c, so[:, 2 * hd:3 * hd], zero_c], axis=1),
                jnp.concatenate([zero_c, so[:, hd:2 * hd], zero_c, so[:, 3 * hd:]], axis=1)], axis=0))
           for qm, so in zip(qkm, sol)]

    state = [s_sc[hp] for hp in range(n_hp)]
    for ch in range(n_ch):
        rows = slice(ch * CHUNK, (ch + 1) * CHUNK)
        res = []
        for hp in range(n_hp):
            i = ch * n_hp + hp
            k_w = jnp.where(bd_mask, ktuw[i][:, 2 * hd:], 0.0)
            q_eff = q_dec[i] - quw[i][:, 2 * hd:]
            res.append(_mm(jnp.concatenate([k_w, q_eff], axis=0), state[hp]))
        for hp in range(n_hp):
            i = ch * n_hp + hp
            o_p = res[hp][2 * hd:] + quw[i][:, :2 * hd]
            state[hp] = (state[hp] * dec_row[i] - res[hp][:2 * hd]
                         + jnp.where(bd_mask, ktuw[i][:, :2 * hd], 0.0))
            for e in range(2):
                lo = (2 * hp + e) * hd
                o = o_p[:, e * hd:(e + 1) * hd]
                ms = jnp.mean(o * o, axis=-1, keepdims=True)
                z = z_ref[rows, lo:lo + hd].astype(F32)
                out = ((o * lax.rsqrt(ms + EPS)) * out_g) * (z * _sigmoid(z))
                y_ref[rows, lo:lo + hd] = out.astype(y_ref.dtype)
    for hp in range(n_hp):
        s_sc[hp] = state[hp]

    @pl.when(t == pl.num_programs(1) - 1)
    def _():
        for h in range(DN_HEADS):
            e = h % 2
            sl_ref[h] = s_sc[h // 2][e * hd:(e + 1) * hd, e * hd:(e + 1) * hd]


def _deltanet(proj3, ba3, tail8, s0, conv_w8, head_p, out_norm, tb, pad):
    bsz, t, _ = proj3.shape
    assert tb % CHUNK == 0 and (pad == 0 or t == tb)
    bcast = lambda b, i: (0, 0)
    col = lambda j: (lambda b, i: (b, i, j))
    return pl.pallas_call(
        functools.partial(_dn_kernel, tb=tb, pad=pad),
        grid=(bsz, t // tb),
        in_specs=[pl.BlockSpec((None, tb, DN_WIDTH), col(2)),
                  pl.BlockSpec((None, tb, DN_WIDTH), col(3)),
                  pl.BlockSpec((None, tb, DN_WIDTH), col(4)),
                  pl.BlockSpec((None, tb, DN_WIDTH), col(5)),
                  pl.BlockSpec((None, tb, LANES), lambda b, i: (b, i, 0)),
                  pl.BlockSpec((SUBLANES, 3 * DN_WIDTH), bcast),
                  pl.BlockSpec((DN_HEADS, DN_HEAD_DIM, DN_HEAD_DIM), lambda b, i: (0, 0, 0)),
                  pl.BlockSpec((SUBLANES, 3 * DN_WIDTH), bcast),
                  pl.BlockSpec((SUBLANES, LANES), bcast),
                  pl.BlockSpec((1, DN_HEAD_DIM), bcast)],
        out_specs=[pl.BlockSpec((None, tb, DN_WIDTH), lambda b, i: (b, i, 0)),
                   pl.BlockSpec((None, DN_HEADS, DN_HEAD_DIM, DN_HEAD_DIM),
                                lambda b, i: (b, 0, 0, 0))],
        out_shape=[jax.ShapeDtypeStruct((bsz, t, DN_WIDTH), BF16),
                   jax.ShapeDtypeStruct((bsz, DN_HEADS, DN_HEAD_DIM, DN_HEAD_DIM), F32)],
        scratch_shapes=[pltpu.VMEM((DN_HEADS // 2, 2 * DN_HEAD_DIM, 2 * DN_HEAD_DIM), F32),
                        pltpu.VMEM((SUBLANES, 3 * DN_WIDTH), F32)],
        compiler_params=_cparams("parallel", "arbitrary"),
        name="deltanet",
    )(proj3, proj3, proj3, proj3, ba3, tail8, s0, conv_w8, head_p, out_norm)


def _outproj_kernel(yl_ref, yd_ref, x_ref, wo_ref, fg_ref, wr_ref, br_ref,
                    h1_ref, xn_ref, id_ref, gt_ref):
    mix = jnp.dot(yl_ref[...], wo_ref[:LRU_WIDTH, :], preferred_element_type=F32)
    mix = mix + jnp.dot(yd_ref[...], wo_ref[LRU_WIDTH:, :], preferred_element_type=F32)
    h1 = x_ref[...] + mix
    h1_ref[...] = h1
    ms = jnp.mean(h1 * h1, axis=-1, keepdims=True)
    xn = (h1 * lax.rsqrt(ms + EPS)) * fg_ref[...]
    xn_ref[...] = _pack_bf16_pairs(xn)

    logits = jnp.dot(xn.astype(BF16), wr_ref[...], preferred_element_type=F32) + br_ref[...]
    lane = lax.broadcasted_iota(jnp.int32, logits.shape, 1)
    lanef = lane.astype(F32)
    neg = -jnp.inf
    big = 1e9
    lg = jnp.where(lane < N_GROUPS, logits, neg)
    mg = jnp.max(lg, axis=-1, keepdims=True)
    p_sel = 1.0 / jnp.sum(jnp.exp(lg - mg), axis=-1, keepdims=True)
    g_sel = jnp.min(jnp.where(lg == mg, lanef, big), axis=-1, keepdims=True)
    e_grp = ((lane - N_GROUPS) >> 3).astype(F32)
    in_grp = (lane >= N_GROUPS) & (lane < N_GROUPS + N_EXPERTS) & (e_grp == g_sel)
    le = jnp.where(in_grp, logits, neg)
    m1 = jnp.max(le, axis=-1, keepdims=True)
    i1 = jnp.min(jnp.where(le == m1, lanef, big), axis=-1, keepdims=True)
    le2 = jnp.where(lanef == i1, neg, le)
    m2 = jnp.max(le2, axis=-1, keepdims=True)
    i2 = jnp.min(jnp.where(le2 == m2, lanef, big), axis=-1, keepdims=True)
    e21 = jnp.exp(m2 - m1)
    w1 = p_sel / (1.0 + e21)
    w2 = p_sel * e21 / (1.0 + e21)
    ids = jnp.where(lane == 0, i1, jnp.where(lane == 1, i2, float(N_GROUPS))) - float(N_GROUPS)
    id_ref[...] = ids.astype(jnp.int32)
    gt_ref[...] = jnp.where(lane == 0, w1, jnp.where(lane == 1, w2, 0.0))


def _outproj(y_lru, y_dn, x2, w_out, ffn_g, w_router, b_router, tm):
    n = x2.shape[0]
    row = lambda i: (i, 0)
    fix = lambda i: (0, 0)
    return pl.pallas_call(
        _outproj_kernel,
        grid=(n // tm,),
        in_specs=[pl.BlockSpec((tm, LRU_WIDTH), row),
                  pl.BlockSpec((tm, DN_WIDTH), row),
                  pl.BlockSpec((tm, D_MODEL), row),
                  pl.BlockSpec((D_MODEL, D_MODEL), fix),
                  pl.BlockSpec((1, D_MODEL), fix),
                  pl.BlockSpec((D_MODEL, LANES), fix),
                  pl.BlockSpec((1, LANES), fix)],
        out_specs=[pl.BlockSpec((tm, D_MODEL), row),
                   pl.BlockSpec((tm, D_MODEL // 2), row),
                   pl.BlockSpec((tm, LANES), row),
                   pl.BlockSpec((tm, LANES), row)],
        out_shape=[jax.ShapeDtypeStruct((n, D_MODEL), F32),
                   jax.ShapeDtypeStruct((n, D_MODEL // 2), jnp.uint32),
                   jax.ShapeDtypeStruct((n, LANES), jnp.int32),
                   jax.ShapeDtypeStruct((n, LANES), F32)],
        compiler_params=_cparams("parallel"),
        name="outproj_router",
    )(y_lru, y_dn, x2, w_out, ffn_g, w_router, b_router)


def _expert_kernel(be_ref, x_ref, wgu_ref, wd_ref, y_ref):
    del be_ref
    x = _unpack_bf16_pairs(x_ref[...]).astype(BF16)
    gu = jnp.dot(x, wgu_ref[...], preferred_element_type=F32)
    g, u = gu[:, :D_EXPERT], gu[:, D_EXPERT:]
    hmid = (g * _sigmoid(g)) * u
    y_ref[...] = _pack_bf16_pairs(jnp.dot(hmid.astype(BF16), wd_ref[...],
                                          preferred_element_type=F32))


def _experts(block_expert, x_buf, w_gu, w_down):
    cap = x_buf.shape[0]
    grid_spec = pltpu.PrefetchScalarGridSpec(
        num_scalar_prefetch=1,
        grid=(cap // MOE_ROWS,),
        in_specs=[pl.BlockSpec((MOE_ROWS, D_MODEL // 2), lambda i, be: (i, 0)),
                  pl.BlockSpec((None, D_MODEL, 2 * D_EXPERT), lambda i, be: (be[i], 0, 0)),
                  pl.BlockSpec((None, D_EXPERT, D_MODEL), lambda i, be: (be[i], 0, 0))],
        out_specs=pl.BlockSpec((MOE_ROWS, D_MODEL // 2), lambda i, be: (i, 0)),
    )
    return pl.pallas_call(
        _expert_kernel,
        grid_spec=grid_spec,
        out_shape=jax.ShapeDtypeStruct((cap, D_MODEL // 2), jnp.uint32),
        compiler_params=_cparams("arbitrary"),
        name="experts",
    )(block_expert, x_buf, w_gu, w_down)


def _row_gather(table, idx):
    n_idx = idx.shape[0]
    d = table.shape[1]
    sc = plsc.get_sparse_core_info()
    n_workers = sc.num_cores * sc.num_subcores
    w = GATHER_WINDOW
    per_w = n_idx // n_workers
    n_steps = per_w // w
    assert per_w * n_workers == n_idx and n_steps * w == per_w and n_steps % 2 == 0
    mesh = plsc.VectorSubcoreMesh(core_axis_name="core", subcore_axis_name="subcore")

    @functools.partial(
        pl.kernel, out_type=jax.ShapeDtypeStruct((n_idx, d), table.dtype), mesh=mesh,
        scratch_types=[pltpu.VMEM((per_w,), jnp.int32),
                       pltpu.VMEM((2, w, d), table.dtype),
                       pltpu.SemaphoreType.DMA((2,)),
                       pltpu.SemaphoreType.DMA((2,))])
    def gather(x_hbm, i_hbm, o_hbm, idx_v, rows_v, g_sem, w_sem):
        wid = lax.axis_index("subcore") * sc.num_cores + lax.axis_index("core")
        base = wid * per_w
        pltpu.sync_copy(i_hbm.at[pl.ds(base, per_w)], idx_v)

        def fetch(s, b):
            return pltpu.make_async_copy(x_hbm.at[idx_v.at[pl.ds(s * w, w)]], rows_v.at[b],
                                         g_sem.at[b])

        def flush(s, b):
            return pltpu.make_async_copy(rows_v.at[b], o_hbm.at[pl.ds(base + s * w, w)],
                                         w_sem.at[b])

        fetch(0, 0).start()

        @pl.loop(0, n_steps, step=2)
        def _(s2):
            for b in range(2):
                s = s2 + b

                @pl.when(s + 1 < n_steps)
                def _():
                    @pl.when(s >= 1)
                    def _():
                        flush(s - 1, 1 - b).wait()
                    fetch(s + 1, 1 - b).start()

                fetch(s, b).wait()
                flush(s, b).start()

        flush(n_steps - 2, 0).wait()
        flush(n_steps - 1, 1).wait()

    return gather(table, idx)


def _combine_kernel(h1_ref, y0_ref, y1_ref, gt_ref, fg_ref, o_ref):
    gt = gt_ref[...]
    h = (h1_ref[...] + gt[:, 0:1] * _unpack_bf16_pairs(y0_ref[...])
         + gt[:, 1:2] * _unpack_bf16_pairs(y1_ref[...]))
    ms = jnp.mean(h * h, axis=-1, keepdims=True)
    o_ref[...] = (h * lax.rsqrt(ms + EPS)) * fg_ref[...]


def _combine(h1, y_tok, gates, final_g, tm):
    n = h1.shape[0]
    row = lambda i: (i, 0)
    return pl.pallas_call(
        _combine_kernel,
        grid=(n // tm,),
        in_specs=[pl.BlockSpec((tm, D_MODEL), row),
                  pl.BlockSpec((tm, D_MODEL // 2), row),
                  pl.BlockSpec((tm, D_MODEL // 2), lambda i: (i + n // tm, 0)),
                  pl.BlockSpec((tm, LANES), row),
                  pl.BlockSpec((1, D_MODEL), lambda i: (0, 0))],
        out_specs=pl.BlockSpec((tm, D_MODEL), row),
        out_shape=jax.ShapeDtypeStruct((n, D_MODEL), F32),
        compiler_params=_cparams("parallel"),
        name="combine",
    )(h1, y_tok, y_tok, gates, final_g)


def _block_diag(blocks):
    n, r, c = blocks.shape
    out = jnp.zeros((n * r, n * c), blocks.dtype)
    for i in range(n):
        out = out.at[i * r:(i + 1) * r, i * c:(i + 1) * c].set(blocks[i])
    return out


def _pad_rows(a, rows):
    return jnp.pad(a, ((0, rows - a.shape[0]), (0, 0)))


def _tail8(rows3):
    return jnp.pad(rows3.astype(F32), ((SUBLANES - rows3.shape[0], 0), (0, 0)))


def _dispatch_tables(ids2):
    n = ids2.shape[0]
    n_assign = 2 * n
    eflat = ids2.T.reshape(-1)
    iota = jnp.arange(n_assign, dtype=jnp.int32)
    _, order = lax.sort((eflat, iota), num_keys=1, is_stable=True)
    _, rank = lax.sort((order, iota), num_keys=1, is_stable=False)
    counts = jnp.sum((eflat[:, None] == jnp.arange(N_EXPERTS, dtype=jnp.int32)[None, :])
                     .astype(jnp.int32), axis=0)
    starts = jnp.cumsum(counts) - counts
    padded = (counts + MOE_ROWS - 1) // MOE_ROWS * MOE_ROWS
    pend = jnp.cumsum(padded)
    pstart = pend - padded
    n_blocks = -(-(n_assign + N_EXPERTS * (MOE_ROWS - 1)) // MOE_ROWS)
    cap = n_blocks * MOE_ROWS
    block_start = jnp.arange(n_blocks, dtype=jnp.int32) * MOE_ROWS
    block_expert = jnp.minimum(jnp.sum(block_start[:, None] >= pend[None, :], axis=1),
                               N_EXPERTS - 1).astype(jnp.int32)
    slot = jnp.arange(cap, dtype=jnp.int32)
    slot_e = jnp.repeat(block_expert, MOE_ROWS)
    j = jnp.clip(slot - pstart[slot_e] + starts[slot_e], 0, n_assign - 1)
    buf_tok = order[j] % n
    dest = pstart[eflat] + rank - starts[eflat]
    return buf_tok, dest, block_expert


def kernel(x, meta_tokens, mix_norm, w_in, lru_conv_w, lru_conv_b, lru_w_r, lru_b_r, lru_w_i,
           lru_b_i, lru_lambda, lru_out_norm, dn_conv_w, dn_a_log, dn_dt_bias, dn_out_norm, w_out,
           ffn_norm, router_group_w, router_group_b, router_expert_w, router_expert_b, moe_w_gate,
           moe_w_up, moe_w_down, final_norm):
    bsz, seq, d = x.shape
    n = bsz * seq
    l = 0

    w_main = w_in[l][:, :PROJ_MAIN].astype(BF16)
    w_ba = jnp.pad(w_in[l][:, PROJ_MAIN:], ((0, 0), (0, LANES - 2 * DN_HEADS))).astype(BF16)
    mix_g = mix_norm[l][None, :]
    lru_cw = _pad_rows(lru_conv_w[l], SUBLANES)
    lru_p = _pad_rows(jnp.stack([lru_conv_b[l], lru_b_r[l], lru_b_i[l], lru_lambda[l],
                                 lru_out_norm[l]]), SUBLANES)
    hb = LRU_BLOCKS // 2
    w_gate = jnp.stack([
        jnp.concatenate([_block_diag(lru_w_r[l][h * hb:(h + 1) * hb]),
                         _block_diag(lru_w_i[l][h * hb:(h + 1) * hb])], axis=1)
        for h in range(2)]).astype(BF16)
    dn_cw = _pad_rows(dn_conv_w[l], SUBLANES)
    head_p = jnp.zeros((SUBLANES, LANES), F32)
    head_p = head_p.at[0, DN_HEADS:2 * DN_HEADS].set(dn_a_log[l])
    head_p = head_p.at[1, DN_HEADS:2 * DN_HEADS].set(dn_dt_bias[l])
    dn_on = dn_out_norm[l][None, :]
    w_o = w_out[l].astype(BF16)
    ffn_g = ffn_norm[l][None, :]
    w_router = jnp.pad(jnp.concatenate([router_group_w[l], router_expert_w[l]], axis=1),
                       ((0, 0), (0, LANES - N_GROUPS - N_EXPERTS))).astype(BF16)
    b_router = jnp.pad(jnp.concatenate([router_group_b[l], router_expert_b[l]]),
                       (0, LANES - N_GROUPS - N_EXPERTS))[None, :]
    w_gu = jnp.concatenate([moe_w_gate[l], moe_w_up[l]], axis=-1).astype(BF16)
    w_dn = moe_w_down[l].astype(BF16)
    final_g = final_norm[None, :]

    proj_m, ba_m = _inproj(meta_tokens, mix_g, w_main, w_ba, N_META)
    zeros_lru = jnp.zeros((SUBLANES, LRU_WIDTH), F32)
    _, h_meta = _lru(proj_m[None], zeros_lru, zeros_lru, lru_cw, lru_p, w_gate, N_META)
    meta_pad = CHUNK - N_META
    _, s_meta = _deltanet(jnp.pad(proj_m, ((meta_pad, 0), (0, 0)))[None],
                          jnp.pad(ba_m, ((meta_pad, 0), (0, 0)))[None],
                          jnp.zeros((SUBLANES, 3 * DN_WIDTH), F32),
                          jnp.zeros((DN_HEADS, DN_HEAD_DIM, DN_HEAD_DIM), F32),
                          dn_cw, head_p, dn_on, CHUNK, meta_pad)
    last = proj_m[N_META - (CONV_WIDTH - 1):]
    lru_tail = _tail8(last[:, :LRU_WIDTH])
    dn_tail = _tail8(last[:, 2 * LRU_WIDTH:2 * LRU_WIDTH + 3 * DN_WIDTH])

    x2 = x.reshape(n, d)
    proj, ba = _inproj(x2, mix_g, w_main, w_ba, 512)
    proj3 = proj.reshape(bsz, seq, PROJ_MAIN)
    ba3 = ba.reshape(bsz, seq, LANES)
    y_lru, _ = _lru(proj3, lru_tail, h_meta[0], lru_cw, lru_p, w_gate, 256)
    y_dn, _ = _deltanet(proj3, ba3, dn_tail, s_meta[0], dn_cw, head_p, dn_on, 256, 0)
    h1, xn, ids, gates = _outproj(y_lru.reshape(n, LRU_WIDTH), y_dn.reshape(n, DN_WIDTH), x2,
                                  w_o, ffn_g, w_router, b_router, 512)

    buf_tok, dest, block_expert = _dispatch_tables(ids[:, :2])
    x_buf = _row_gather(xn, buf_tok)
    y_buf = _experts(block_expert, x_buf, w_gu, w_dn)
    y_tok = _row_gather(y_buf, dest)
    out = _combine(h1, y_tok, gates, final_g, 512)
    return out.reshape(bsz, seq, d)
```

```python
import functools

import jax
import jax.numpy as jnp
from jax import lax
from jax.experimental import pallas as pl
from jax.experimental.pallas import tpu as pltpu
from jax.experimental.pallas import tpu_sc as plsc

F32 = jnp.float32
BF16 = jnp.bfloat16

D_MODEL = 1024
N_META = 16
CHUNK = 64
CONV_WIDTH = 4
LRU_WIDTH = 512
LRU_BLOCKS = 8
LRU_C = 8.0
DN_HEADS = 4
DN_HEAD_DIM = 128
DN_WIDTH = DN_HEADS * DN_HEAD_DIM
N_GROUPS = 4
EXPERTS_PER_GROUP = 8
N_EXPERTS = N_GROUPS * EXPERTS_PER_GROUP
D_EXPERT = 256
EPS = 1e-6
CONV_COLS = LRU_WIDTH + 3 * DN_WIDTH
GATE_COLS = LRU_WIDTH + DN_WIDTH
LANES = 128
SUBLANES = 8
INV_BLOCK = 16
MOE_ROWS = 512
VMEM_LIMIT = 56 * 1024 * 1024
GATHER_WINDOW = 64


def _cparams(*sem):
    return pltpu.CompilerParams(dimension_semantics=sem, vmem_limit_bytes=VMEM_LIMIT)


def _sigmoid(x):
    return 0.5 * jnp.tanh(0.5 * x) + 0.5


def _pack_bf16_pairs(x):
    c = x.shape[1] // 2

    def rne(v):
        b = lax.bitcast_convert_type(v, jnp.uint32)
        return b + jnp.uint32(0x7FFF) + ((b >> 16) & jnp.uint32(1))

    return (rne(x[:, :c]) >> 16) | (rne(x[:, c:]) & jnp.uint32(0xFFFF0000))


def _unpack_bf16_pairs(w):
    lo = lax.bitcast_convert_type(w << 16, F32)
    hi = lax.bitcast_convert_type(w & jnp.uint32(0xFFFF0000), F32)
    return jnp.concatenate([lo, hi], axis=1)


def _softplus(x):
    return jnp.maximum(x, 0.0) + jnp.log(1.0 + jnp.exp(-jnp.abs(x)))


def _gelu_tanh(x):
    return 0.5 * x * (1.0 + jnp.tanh(0.7978845608028654 * (x + 0.044715 * (x * x * x))))


def _mm(a, b):
    return jnp.dot(a.astype(BF16), b.astype(BF16), preferred_element_type=F32)


def _mm_nt(a, b):
    return lax.dot_general(a.astype(BF16), b.astype(BF16), (((1,), (1,)), ((), ())),
                           preferred_element_type=F32)


def _mm_tn(a, b):
    return lax.dot_general(a.astype(BF16), b.astype(BF16), (((0,), (0,)), ((), ())),
                           preferred_element_type=F32)


def _lru_branch(xc, gate, p_ref, wg_ref, h_sc, valid):
    tb = xc.shape[0]
    b_r, b_i = p_ref[1:2, :], p_ref[2:3, :]
    lam, out_g = p_ref[3:4, :], p_ref[4:5, :]
    half = LRU_WIDTH // 2
    xcb = xc.astype(BF16)
    g0 = jnp.dot(xcb[:, :half], wg_ref[0], preferred_element_type=F32)
    g1 = jnp.dot(xcb[:, half:], wg_ref[1], preferred_element_type=F32)
    r = _sigmoid(jnp.concatenate([g0[:, :half], g1[:, :half]], axis=1) + b_r)
    i = _sigmoid(jnp.concatenate([g0[:, half:], g1[:, half:]], axis=1) + b_i)
    log_a = (-LRU_C) * r * _softplus(-lam)
    a = jnp.exp(log_a)
    var = 1.0 - jnp.exp(2.0 * log_a)
    b = jnp.where(var > 0.0, var * lax.rsqrt(var), 0.0) * (i * xc)
    if valid is not None:
        b = jnp.where(valid, b, 0.0)
        a = jnp.where(valid, a, 1.0)

    sub = lax.broadcasted_iota(jnp.int32, (tb, LRU_WIDTH), 0) & (SUBLANES - 1)
    for s in (1, 2, 4):
        keep = sub >= s
        b = jnp.where(keep, a * pltpu.roll(b, s, axis=0), 0.0) + b
        a = jnp.where(keep, a * pltpu.roll(a, s, axis=0), a)
    h = h_sc[...]
    hs = []
    for g in range(tb // SUBLANES):
        hg = a[g * SUBLANES:(g + 1) * SUBLANES] * h + b[g * SUBLANES:(g + 1) * SUBLANES]
        hs.append(hg)
        h = jnp.broadcast_to(hg[SUBLANES - 1:SUBLANES, :], (SUBLANES, LRU_WIDTH))
    h_sc[...] = h
    out = jnp.concatenate(hs, axis=0) * _gelu_tanh(gate)
    ms = jnp.mean(out * out, axis=-1, keepdims=True)
    return (out * lax.rsqrt(ms + EPS)) * out_g


def _pair_mm(x, y, half_masks):
    yb = y.astype(BF16)
    rhs = jnp.concatenate([yb * half_masks[0], yb * half_masks[1]], axis=0)
    return jnp.dot(x.astype(BF16), rhs, preferred_element_type=F32)


def _pair_inverse(a_list, eye, same16, half_masks):
    def mm(xs, ys):
        return [_pair_mm(x, y, half_masks) for x, y in zip(xs, ys)]

    def plus(xs):
        return [eye + x for x in xs]

    def minus(xs):
        return [eye - x for x in xs]

    d = [jnp.where(same16, a, 0.0) for a in a_list]
    d2 = mm(d, d)
    d4 = mm(d2, d2)
    d8 = mm(d4, d4)
    p = mm(mm(mm(minus(d), plus(d2)), plus(d4)), plus(d8))
    m = mm(p, [a - x for a, x in zip(a_list, d)])
    q = mm(minus(m), plus(mm(m, m)))
    return mm(q, p)


def _dn_branch(act, z, ba, hp_ref, on_ref, s_sc, y_sc, valid):
    tb = act.shape[0]
    hd = DN_HEAD_DIM
    act = act * _sigmoid(act)
    beta_t = _sigmoid(ba)
    g_t = pltpu.roll(-jnp.exp(hp_ref[0:1, :]) * _softplus(ba + hp_ref[1:2, :]),
                     LANES - DN_HEADS, axis=1)
    if valid is not None:
        act = jnp.where(valid, act, 0.0)
        beta_t = jnp.where(valid, beta_t, 0.0)
        g_t = jnp.where(valid, g_t, 0.0)

    heads = []
    for i in range(3 * DN_HEADS):
        seg = act[:, i * hd:(i + 1) * hd]
        if i < 2 * DN_HEADS:
            nrm = lax.rsqrt(jnp.sum(seg * seg, axis=-1, keepdims=True) + EPS)
            if i < DN_HEADS:
                nrm = nrm * (hd ** -0.5)
            seg = seg * nrm
        heads.append(seg)
    qn, kn, vv = heads[:DN_HEADS], heads[DN_HEADS:2 * DN_HEADS], heads[2 * DN_HEADS:]

    row_c = lax.broadcasted_iota(jnp.int32, (tb, LANES), 0) & (CHUNK - 1)
    cum = g_t
    s = 1
    while s < CHUNK:
        cum = cum + jnp.where(row_c >= s, pltpu.roll(cum, s, axis=0), 0.0)
        s *= 2
    e_cum = jnp.exp(cum)

    ri = lax.broadcasted_iota(jnp.int32, (CHUNK, LANES), 0)
    li = lax.broadcasted_iota(jnp.int32, (CHUNK, LANES), 1)
    lo_half = li < CHUNK
    half_masks = (lo_half.astype(BF16), (li >= CHUNK).astype(BF16))
    cj = li & (CHUNK - 1)
    eye_b = ri == cj
    eye = eye_b.astype(F32)
    causal = ri >= cj
    strict = ri > cj
    same16 = (ri // INV_BLOCK) == (cj // INV_BLOCK)
    bd_mask = ((lax.broadcasted_iota(jnp.int32, (2 * hd, 2 * hd), 0) >= hd)
               == (lax.broadcasted_iota(jnp.int32, (2 * hd, 2 * hd), 1) >= hd))
    zero_c = jnp.zeros((CHUNK, hd), F32)
    out_g = on_ref[...]

    def bd_rows(x0, x1):
        z0 = jnp.zeros_like(x0)
        return jnp.concatenate([jnp.concatenate([x0, z0], axis=1),
                                jnp.concatenate([z0, x1], axis=1)], axis=0)

    n_ch = tb // CHUNK
    n_hp = DN_HEADS // 2
    probs = [(ch, hp) for ch in range(n_ch) for hp in range(n_hp)]
    qkm, q_dec, k_dec, rhs, a_list, dec_row = [], [], [], [], [], []
    for ch, hp in probs:
        rows = slice(ch * CHUNK, (ch + 1) * CHUNK)
        h0, h1 = 2 * hp, 2 * hp + 1
        cum_c, beta_c, ecum_c = cum[rows], beta_t[rows], e_cum[rows]
        last = cum_c[CHUNK - 1:CHUNK, :]

        def tile(arr):
            return jnp.where(lo_half, jnp.broadcast_to(arr[:, h0:h0 + 1], (CHUNK, LANES)),
                             jnp.broadcast_to(arr[:, h1:h1 + 1], (CHUNK, LANES)))

        def wide(arr):
            return jnp.concatenate([jnp.broadcast_to(arr[:, h0:h0 + 1], (arr.shape[0], hd)),
                                    jnp.broadcast_to(arr[:, h1:h1 + 1], (arr.shape[0], hd))],
                                   axis=1)

        cum_cp = tile(cum_c)
        cum_rp = jnp.sum(jnp.where(eye_b, cum_cp, 0.0), axis=0, keepdims=True)
        decay = jnp.where(causal, jnp.exp(jnp.where(causal, cum_cp - cum_rp, 0.0)), 0.0)
        q_p = jnp.concatenate([qn[h0][rows], qn[h1][rows]], axis=1)
        k_p = jnp.concatenate([kn[h0][rows], kn[h1][rows]], axis=1)
        v_p = jnp.concatenate([vv[h0][rows], vv[h1][rows]], axis=1)
        qkk = _mm_nt(jnp.concatenate([q_p, k_p], axis=0),
                     bd_rows(kn[h0][rows], kn[h1][rows]))
        qkm.append(qkk[:CHUNK] * decay)
        a_list.append(jnp.where(strict, qkk[CHUNK:] * decay, 0.0) * tile(beta_c))
        vb = v_p * wide(beta_c)
        kb = k_p * wide(beta_c * ecum_c)
        rhs.append(jnp.concatenate(
            [jnp.concatenate([vb[:, :hd], zero_c, kb[:, :hd], zero_c], axis=1),
             jnp.concatenate([zero_c, vb[:, hd:], zero_c, kb[:, hd:]], axis=1)], axis=0))
        q_dec.append(q_p * wide(ecum_c))
        k_dec.append(k_p * wide(jnp.exp(last - cum_c)))
        dec_row.append(wide(jnp.exp(last)))

    t_inv = _pair_inverse(a_list, eye, same16, half_masks)
    sol = [_mm(ti, r) for ti, r in zip(t_inv, rhs)]
    ktuw = [_mm_tn(kd, so) for kd, so in zip(k_dec, sol)]
    quw = [_mm(qm, jnp.concatenate(
               [jnp.concatenate([so[:, :hd], zero_c, so[:, 2 * hd:3 * hd], zero_c], axis=1),
                jnp.concatenate([zero_c, so[:, hd:2 * hd], zero_c, so[:, 3 * hd:]], axis=1)], axis=0))
           for qm, so in zip(qkm, sol)]

    state = [s_sc[hp] for hp in range(n_hp)]
    for ch in range(n_ch):
        rows = slice(ch * CHUNK, (ch + 1) * CHUNK)
        res = []
        for hp in range(n_hp):
            i = ch * n_hp + hp
            k_w = jnp.where(bd_mask, ktuw[i][:, 2 * hd:], 0.0)
            q_eff = q_dec[i] - quw[i][:, 2 * hd:]
            res.append(_mm(jnp.concatenate([k_w, q_eff], axis=0), state[hp]))
        for hp in range(n_hp):
            i = ch * n_hp + hp
            o_p = res[hp][2 * hd:] + quw[i][:, :2 * hd]
            state[hp] = (state[hp] * dec_row[i] - res[hp][:2 * hd]
                         + jnp.where(bd_mask, ktuw[i][:, :2 * hd], 0.0))
            for e in range(2):
                lo = (2 * hp + e) * hd
                o = o_p[:, e * hd:(e + 1) * hd]
                ms = jnp.mean(o * o, axis=-1, keepdims=True)
                zz = z[rows, lo:lo + hd]
                y_sc[rows, lo:lo + hd] = ((o * lax.rsqrt(ms + EPS)) * out_g) * (zz * _sigmoid(zz))
    for hp in range(n_hp):
        s_sc[hp] = state[hp]


def _route(logits):
    lane = lax.broadcasted_iota(jnp.int32, logits.shape, 1)
    lanef = lane.astype(F32)
    neg = -jnp.inf
    big = 1e9
    lg = jnp.where(lane < N_GROUPS, logits, neg)
    mg = jnp.max(lg, axis=-1, keepdims=True)
    p_sel = 1.0 / jnp.sum(jnp.exp(lg - mg), axis=-1, keepdims=True)
    g_sel = jnp.min(jnp.where(lg == mg, lanef, big), axis=-1, keepdims=True)
    e_grp = ((lane - N_GROUPS) >> 3).astype(F32)
    in_grp = (lane >= N_GROUPS) & (lane < N_GROUPS + N_EXPERTS) & (e_grp == g_sel)
    le = jnp.where(in_grp, logits, neg)
    m1 = jnp.max(le, axis=-1, keepdims=True)
    i1 = jnp.min(jnp.where(le == m1, lanef, big), axis=-1, keepdims=True)
    le2 = jnp.where(lanef == i1, neg, le)
    m2 = jnp.max(le2, axis=-1, keepdims=True)
    i2 = jnp.min(jnp.where(le2 == m2, lanef, big), axis=-1, keepdims=True)
    e21 = jnp.exp(m2 - m1)
    w1 = p_sel / (1.0 + e21)
    w2 = p_sel * e21 / (1.0 + e21)
    ids = jnp.where(lane == 0, i1, jnp.where(lane == 1, i2, float(N_GROUPS))) - float(N_GROUPS)
    gates = jnp.where(lane == 0, w1, jnp.where(lane == 1, w2, 0.0))
    return ids.astype(jnp.int32), gates


def _mixer_kernel(x_ref, tail_ref, h0_ref, s0_ref, mg_ref, wm_ref, wba_ref, cw_ref, lp_ref, wg_ref,
                  hp_ref, on_ref, wo_ref, fg_ref, wr_ref, br_ref,
                  h1_ref, xn_ref, id_ref, gt_ref, hl_ref, sl_ref, tl_ref,
                  h_sc, s_sc, cs_sc, y_sc, *, tb, pad):
    t = pl.program_id(1)
    hd = DN_HEAD_DIM

    @pl.when(t == 0)
    def _():
        zero_hd = jnp.zeros((hd, hd), F32)
        h_sc[...] = h0_ref[...]
        for hp in range(DN_HEADS // 2):
            s_sc[hp] = jnp.concatenate(
                [jnp.concatenate([s0_ref[2 * hp], zero_hd], axis=1),
                 jnp.concatenate([zero_hd, s0_ref[2 * hp + 1]], axis=1)], axis=0)
        cs_sc[0:SUBLANES, :] = tail_ref[...]

    x = x_ref[...]
    ms = jnp.mean(x * x, axis=-1, keepdims=True)
    u = ((x * lax.rsqrt(ms + EPS)) * mg_ref[...]).astype(BF16)
    valid = None
    if pad:
        valid = lax.broadcasted_iota(jnp.int32, (tb, 1), 0) >= pad

    for n in range(0, CONV_COLS, 512):
        cs_sc[SUBLANES:SUBLANES + tb, n:n + 512] = jnp.dot(u, wm_ref[:, n:n + 512],
                                                           preferred_element_type=F32)
    gate = jnp.dot(u, wm_ref[:, CONV_COLS:CONV_COLS + LRU_WIDTH], preferred_element_type=F32)
    z = jnp.dot(u, wm_ref[:, CONV_COLS + LRU_WIDTH:], preferred_element_type=F32)
    ba = jnp.dot(u, wba_ref[...], preferred_element_type=F32)

    def conv(lo, width):
        acc = cs_sc[SUBLANES:SUBLANES + tb, lo:lo + width] * cw_ref[3:4, lo:lo + width]
        for d in range(1, CONV_WIDTH):
            acc = acc + (cs_sc[SUBLANES - d:SUBLANES - d + tb, lo:lo + width]
                         * cw_ref[3 - d:4 - d, lo:lo + width])
        return acc

    xc = conv(0, LRU_WIDTH) + lp_ref[0:1, :]
    act = conv(LRU_WIDTH, 3 * DN_WIDTH)
    cs_sc[0:SUBLANES, :] = cs_sc[tb:tb + SUBLANES, :]

    y_lru = _lru_branch(xc, gate, lp_ref, wg_ref, h_sc, valid)
    _dn_branch(act, z, ba, hp_ref, on_ref, s_sc, y_sc, valid)

    mix = jnp.dot(y_lru.astype(BF16), wo_ref[:LRU_WIDTH, :], preferred_element_type=F32)
    mix = mix + jnp.dot(y_sc[...].astype(BF16), wo_ref[LRU_WIDTH:, :], preferred_element_type=F32)
    h1 = x + mix
    h1_ref[...] = h1
    ms1 = jnp.mean(h1 * h1, axis=-1, keepdims=True)
    xn = (h1 * lax.rsqrt(ms1 + EPS)) * fg_ref[...]
    xn_ref[...] = _pack_bf16_pairs(xn)
    ids, gates = _route(jnp.dot(xn.astype(BF16), wr_ref[...], preferred_element_type=F32)
                        + br_ref[...])
    id_ref[...] = ids
    gt_ref[...] = gates

    @pl.when(t == pl.num_programs(1) - 1)
    def _():
        hl_ref[...] = h_sc[...]
        for h in range(DN_HEADS):
            e = h % 2
            sl_ref[h] = s_sc[h // 2][e * hd:(e + 1) * hd, e * hd:(e + 1) * hd]
        tl_ref[...] = cs_sc[0:SUBLANES, :]


def _mixer(x3, tail8, h0, s0, weights, tb, pad):
    bsz, t, _ = x3.shape
    assert tb % CHUNK == 0 and t % tb == 0 and (pad == 0 or t == tb)
    fix2 = lambda b, i: (0, 0)
    fix3 = lambda b, i: (0, 0, 0)
    blk = lambda b, i: (b, i, 0)
    per_b = lambda b, i: (b, 0, 0)
    w_specs = [pl.BlockSpec(w.shape, fix2 if w.ndim == 2 else fix3) for w in weights]
    return pl.pallas_call(
        functools.partial(_mixer_kernel, tb=tb, pad=pad),
        grid=(bsz, t // tb),
        in_specs=[pl.BlockSpec((None, tb, D_MODEL), blk),
                  pl.BlockSpec((SUBLANES, CONV_COLS), fix2),
                  pl.BlockSpec((SUBLANES, LRU_WIDTH), fix2),
                  pl.BlockSpec((DN_HEADS, DN_HEAD_DIM, DN_HEAD_DIM), fix3)] + w_specs,
        out_specs=[pl.BlockSpec((None, tb, D_MODEL), blk),
                   pl.BlockSpec((None, tb, D_MODEL // 2), blk),
                   pl.BlockSpec((None, tb, LANES), blk),
                   pl.BlockSpec((None, tb, LANES), blk),
                   pl.BlockSpec((None, SUBLANES, LRU_WIDTH), per_b),
                   pl.BlockSpec((None, DN_HEADS, DN_HEAD_DIM, DN_HEAD_DIM),
                                lambda b, i: (b, 0, 0, 0)),
                   pl.BlockSpec((None, SUBLANES, CONV_COLS), per_b)],
        out_shape=[jax.ShapeDtypeStruct((bsz, t, D_MODEL), F32),
                   jax.ShapeDtypeStruct((bsz, t, D_MODEL // 2), jnp.uint32),
                   jax.ShapeDtypeStruct((bsz, t, LANES), jnp.int32),
                   jax.ShapeDtypeStruct((bsz, t, LANES), F32),
                   jax.ShapeDtypeStruct((bsz, SUBLANES, LRU_WIDTH), F32),
                   jax.ShapeDtypeStruct((bsz, DN_HEADS, DN_HEAD_DIM, DN_HEAD_DIM), F32),
                   jax.ShapeDtypeStruct((bsz, SUBLANES, CONV_COLS), F32)],
        scratch_shapes=[pltpu.VMEM((SUBLANES, LRU_WIDTH), F32),
                        pltpu.VMEM((DN_HEADS // 2, 2 * DN_HEAD_DIM, 2 * DN_HEAD_DIM), F32),
                        pltpu.VMEM((tb + SUBLANES, CONV_COLS), F32),
                        pltpu.VMEM((tb, DN_WIDTH), F32)],
        compiler_params=_cparams("parallel", "arbitrary"),
        name="mixer",
    )(x3, tail8, h0, s0, *weights)


def _expert_kernel(be_ref, x_ref, wgu_ref, wd_ref, y_ref):
    del be_ref
    x = _unpack_bf16_pairs(x_ref[...]).astype(BF16)
    gu = jnp.dot(x, wgu_ref[...], preferred_element_type=F32)
    g, u = gu[:, :D_EXPERT], gu[:, D_EXPERT:]
    hmid = (g * _sigmoid(g)) * u
    y_ref[...] = _pack_bf16_pairs(jnp.dot(hmid.astype(BF16), wd_ref[...],
                                          preferred_element_type=F32))


def _experts(block_expert, x_buf, w_gu, w_down):
    cap = x_buf.shape[0]
    grid_spec = pltpu.PrefetchScalarGridSpec(
        num_scalar_prefetch=1,
        grid=(cap // MOE_ROWS,),
        in_specs=[pl.BlockSpec((MOE_ROWS, D_MODEL // 2), lambda i, be: (i, 0)),
                  pl.BlockSpec((None, D_MODEL, 2 * D_EXPERT), lambda i, be: (be[i], 0, 0)),
                  pl.BlockSpec((None, D_EXPERT, D_MODEL), lambda i, be: (be[i], 0, 0))],
        out_specs=pl.BlockSpec((MOE_ROWS, D_MODEL // 2), lambda i, be: (i, 0)),
    )
    return pl.pallas_call(
        _expert_kernel,
        grid_spec=grid_spec,
        out_shape=jax.ShapeDtypeStruct((cap, D_MODEL // 2), jnp.uint32),
        compiler_params=_cparams("arbitrary"),
        name="experts",
    )(block_expert, x_buf, w_gu, w_down)


def _row_gather(table, idx):
    n_idx = idx.shape[0]
    d = table.shape[1]
    sc = plsc.get_sparse_core_info()
    n_workers = sc.num_cores * sc.num_subcores
    w = GATHER_WINDOW
    per_w = n_idx // n_workers
    n_steps = per_w // w
    assert per_w * n_workers == n_idx and n_steps * w == per_w and n_steps % 2 == 0
    mesh = plsc.VectorSubcoreMesh(core_axis_name="core", subcore_axis_name="subcore")

    @functools.partial(
        pl.kernel, out_type=jax.ShapeDtypeStruct((n_idx, d), table.dtype), mesh=mesh,
        scratch_types=[pltpu.VMEM((per_w,), jnp.int32),
                       pltpu.VMEM((2, w, d), table.dtype),
                       pltpu.SemaphoreType.DMA((2,)),
                       pltpu.SemaphoreType.DMA((2,))])
    def gather(x_hbm, i_hbm, o_hbm, idx_v, rows_v, g_sem, w_sem):
        wid = lax.axis_index("subcore") * sc.num_cores + lax.axis_index("core")
        base = wid * per_w
        pltpu.sync_copy(i_hbm.at[pl.ds(base, per_w)], idx_v)

        def fetch(s, b):
            return pltpu.make_async_copy(x_hbm.at[idx_v.at[pl.ds(s * w, w)]], rows_v.at[b],
                                         g_sem.at[b])

        def flush(s, b):
            return pltpu.make_async_copy(rows_v.at[b], o_hbm.at[pl.ds(base + s * w, w)],
                                         w_sem.at[b])

        fetch(0, 0).start()

        @pl.loop(0, n_steps, step=2)
        def _(s2):
            for b in range(2):
                s = s2 + b

                @pl.when(s + 1 < n_steps)
                def _():
                    @pl.when(s >= 1)
                    def _():
                        flush(s - 1, 1 - b).wait()
                    fetch(s + 1, 1 - b).start()

                fetch(s, b).wait()
                flush(s, b).start()

        flush(n_steps - 2, 0).wait()
        flush(n_steps - 1, 1).wait()

    return gather(table, idx)


def _combine_kernel(h1_ref, y0_ref, y1_ref, gt_ref, fg_ref, o_ref):
    gt = gt_ref[...]
    h = (h1_ref[...] + gt[:, 0:1] * _unpack_bf16_pairs(y0_ref[...])
         + gt[:, 1:2] * _unpack_bf16_pairs(y1_ref[...]))
    ms = jnp.mean(h * h, axis=-1, keepdims=True)
    o_ref[...] = (h * lax.rsqrt(ms + EPS)) * fg_ref[...]


def _combine(h1, y_tok, gates, final_g, tm):
    n = h1.shape[0]
    row = lambda i: (i, 0)
    return pl.pallas_call(
        _combine_kernel,
        grid=(n // tm,),
        in_specs=[pl.BlockSpec((tm, D_MODEL), row),
                  pl.BlockSpec((tm, D_MODEL // 2), row),
                  pl.BlockSpec((tm, D_MODEL // 2), lambda i: (i + n // tm, 0)),
                  pl.BlockSpec((tm, LANES), row),
                  pl.BlockSpec((1, D_MODEL), lambda i: (0, 0))],
        out_specs=pl.BlockSpec((tm, D_MODEL), row),
        out_shape=jax.ShapeDtypeStruct((n, D_MODEL), F32),
        compiler_params=_cparams("parallel"),
        name="combine",
    )(h1, y_tok, y_tok, gates, final_g)


def _block_diag(blocks):
    n, r, c = blocks.shape
    out = jnp.zeros((n * r, n * c), blocks.dtype)
    for i in range(n):
        out = out.at[i * r:(i + 1) * r, i * c:(i + 1) * c].set(blocks[i])
    return out


def _pad_rows(a, rows):
    return jnp.pad(a, ((0, rows - a.shape[0]), (0, 0)))


def _dispatch_tables(ids2):
    n = ids2.shape[0]
    n_assign = 2 * n
    eflat = ids2.T.reshape(-1)
    iota = jnp.arange(n_assign, dtype=jnp.int32)
    _, order = lax.sort((eflat, iota), num_keys=1, is_stable=True)
    _, rank = lax.sort((order, iota), num_keys=1, is_stable=False)
    counts = jnp.sum((eflat[:, None] == jnp.arange(N_EXPERTS, dtype=jnp.int32)[None, :])
                     .astype(jnp.int32), axis=0)
    starts = jnp.cumsum(counts) - counts
    padded = (counts + MOE_ROWS - 1) // MOE_ROWS * MOE_ROWS
    pend = jnp.cumsum(padded)
    pstart = pend - padded
    n_blocks = -(-(n_assign + N_EXPERTS * (MOE_ROWS - 1)) // MOE_ROWS)
    cap = n_blocks * MOE_ROWS
    block_start = jnp.arange(n_blocks, dtype=jnp.int32) * MOE_ROWS
    block_expert = jnp.minimum(jnp.sum(block_start[:, None] >= pend[None, :], axis=1),
                               N_EXPERTS - 1).astype(jnp.int32)
    slot = jnp.arange(cap, dtype=jnp.int32)
    slot_e = jnp.repeat(block_expert, MOE_ROWS)
    j = jnp.clip(slot - pstart[slot_e] + starts[slot_e], 0, n_assign - 1)
    buf_tok = order[j] % n
    dest = pstart[eflat] + rank - starts[eflat]
    return buf_tok, dest, block_expert


def kernel(x, meta_tokens, mix_norm, w_in, lru_conv_w, lru_conv_b, lru_w_r, lru_b_r, lru_w_i,
           lru_b_i, lru_lambda, lru_out_norm, dn_conv_w, dn_a_log, dn_dt_bias, dn_out_norm, w_out,
           ffn_norm, router_group_w, router_group_b, router_expert_w, router_expert_b, moe_w_gate,
           moe_w_up, moe_w_down, final_norm):
    bsz, seq, d = x.shape
    n = bsz * seq
    l = 0

    lw = LRU_WIDTH
    w = w_in[l]
    w_main = jnp.concatenate([w[:, :lw], w[:, 2 * lw:2 * lw + 3 * DN_WIDTH], w[:, lw:2 * lw],
                              w[:, 2 * lw + 3 * DN_WIDTH:2 * lw + 4 * DN_WIDTH]], axis=1).astype(BF16)
    w_ba = jnp.pad(w[:, 2 * lw + 4 * DN_WIDTH:], ((0, 0), (0, LANES - 2 * DN_HEADS))).astype(BF16)
    mix_g = mix_norm[l][None, :]
    conv_w = _pad_rows(jnp.concatenate([lru_conv_w[l], dn_conv_w[l]], axis=1), SUBLANES)
    lru_p = _pad_rows(jnp.stack([lru_conv_b[l], lru_b_r[l], lru_b_i[l], lru_lambda[l],
                                 lru_out_norm[l]]), SUBLANES)
    hb = LRU_BLOCKS // 2
    w_gate = jnp.stack([
        jnp.concatenate([_block_diag(lru_w_r[l][h * hb:(h + 1) * hb]),
                         _block_diag(lru_w_i[l][h * hb:(h + 1) * hb])], axis=1)
        for h in range(2)]).astype(BF16)
    head_p = jnp.zeros((SUBLANES, LANES), F32)
    head_p = head_p.at[0, DN_HEADS:2 * DN_HEADS].set(dn_a_log[l])
    head_p = head_p.at[1, DN_HEADS:2 * DN_HEADS].set(dn_dt_bias[l])
    dn_on = dn_out_norm[l][None, :]
    w_o = w_out[l].astype(BF16)
    ffn_g = ffn_norm[l][None, :]
    w_router = jnp.pad(jnp.concatenate([router_group_w[l], router_expert_w[l]], axis=1),
                       ((0, 0), (0, LANES - N_GROUPS - N_EXPERTS))).astype(BF16)
    b_router = jnp.pad(jnp.concatenate([router_group_b[l], router_expert_b[l]]),
                       (0, LANES - N_GROUPS - N_EXPERTS))[None, :]
    w_gu = jnp.concatenate([moe_w_gate[l], moe_w_up[l]], axis=-1).astype(BF16)
    w_dn = moe_w_down[l].astype(BF16)
    final_g = final_norm[None, :]
    weights = (mix_g, w_main, w_ba, conv_w, lru_p, w_gate, head_p, dn_on, w_o, ffn_g, w_router,
               b_router)

    meta_pad = CHUNK - N_META
    prefix = jnp.pad(meta_tokens, ((meta_pad, 0), (0, 0)))[None]
    zeros = lambda *s: jnp.zeros(s, F32)
    *_, h_meta, s_meta, tail_meta = _mixer(
        prefix, zeros(SUBLANES, CONV_COLS), zeros(SUBLANES, LRU_WIDTH),
        zeros(DN_HEADS, DN_HEAD_DIM, DN_HEAD_DIM), weights, CHUNK, meta_pad)

    h1, xn, ids, gates, _, _, _ = _mixer(x, tail_meta[0], h_meta[0], s_meta[0], weights, 256, 0)
    h1 = h1.reshape(n, d)
    gates = gates.reshape(n, LANES)

    buf_tok, dest, block_expert = _dispatch_tables(ids.reshape(n, LANES)[:, :2])
    x_buf = _row_gather(xn.reshape(n, d // 2), buf_tok)
    y_buf = _experts(block_expert, x_buf, w_gu, w_dn)
    y_tok = _row_gather(y_buf, dest)
    out = _combine(h1, y_tok, gates, final_g, 512)
    return out.reshape(bsz, seq, d)
```

```python
import functools

import jax
import jax.numpy as jnp
from jax import lax
from jax.experimental import pallas as pl
from jax.experimental.pallas import tpu as pltpu
from jax.experimental.pallas import tpu_sc as plsc

F32 = jnp.float32
BF16 = jnp.bfloat16

D_MODEL = 1024
N_META = 16
CHUNK = 64
CONV_WIDTH = 4
LRU_WIDTH = 512
LRU_BLOCKS = 8
LRU_C = 8.0
DN_HEADS = 4
DN_HEAD_DIM = 128
DN_WIDTH = DN_HEADS * DN_HEAD_DIM
N_GROUPS = 4
EXPERTS_PER_GROUP = 8
N_EXPERTS = N_GROUPS * EXPERTS_PER_GROUP
D_EXPERT = 256
EPS = 1e-6
CONV_COLS = LRU_WIDTH + 3 * DN_WIDTH
GATE_COLS = LRU_WIDTH + DN_WIDTH
LANES = 128
SUBLANES = 8
INV_BLOCK = 16
MOE_ROWS = 512
VMEM_LIMIT = 56 * 1024 * 1024
GATHER_WINDOW = 64


def _cparams(*sem):
    return pltpu.CompilerParams(dimension_semantics=sem, vmem_limit_bytes=VMEM_LIMIT)


def _sigmoid(x):
    return 0.5 * jnp.tanh(0.5 * x) + 0.5


def _pack_bf16_pairs(x):
    c = x.shape[1] // 2

    def rne(v):
        b = lax.bitcast_convert_type(v, jnp.uint32)
        return b + jnp.uint32(0x7FFF) + ((b >> 16) & jnp.uint32(1))

    return (rne(x[:, :c]) >> 16) | (rne(x[:, c:]) & jnp.uint32(0xFFFF0000))


def _unpack_bf16_pairs(w):
    lo = lax.bitcast_convert_type(w << 16, F32)
    hi = lax.bitcast_convert_type(w & jnp.uint32(0xFFFF0000), F32)
    return jnp.concatenate([lo, hi], axis=1)


def _softplus(x):
    return jnp.maximum(x, 0.0) + jnp.log(1.0 + jnp.exp(-jnp.abs(x)))


def _gelu_tanh(x):
    return 0.5 * x * (1.0 + jnp.tanh(0.7978845608028654 * (x + 0.044715 * (x * x * x))))


def _mm(a, b):
    return jnp.dot(a.astype(BF16), b.astype(BF16), preferred_element_type=F32)


def _mm_nt(a, b):
    return lax.dot_general(a.astype(BF16), b.astype(BF16), (((1,), (1,)), ((), ())),
                           preferred_element_type=F32)


def _mm_tn(a, b):
    return lax.dot_general(a.astype(BF16), b.astype(BF16), (((0,), (0,)), ((), ())),
                           preferred_element_type=F32)


def _lru_branch(xc, gate, p_ref, wg_ref, h_sc, valid):
    tb = xc.shape[0]
    b_r, b_i = p_ref[1:2, :], p_ref[2:3, :]
    lam, out_g = p_ref[3:4, :], p_ref[4:5, :]
    half = LRU_WIDTH // 2
    xcb = xc.astype(BF16)
    g0 = jnp.dot(xcb[:, :half], wg_ref[0], preferred_element_type=F32)
    g1 = jnp.dot(xcb[:, half:], wg_ref[1], preferred_element_type=F32)
    r = _sigmoid(jnp.concatenate([g0[:, :half], g1[:, :half]], axis=1) + b_r)
    i = _sigmoid(jnp.concatenate([g0[:, half:], g1[:, half:]], axis=1) + b_i)
    log_a = (-LRU_C) * r * _softplus(-lam)
    a = jnp.exp(log_a)
    var = 1.0 - jnp.exp(2.0 * log_a)
    b = jnp.where(var > 0.0, var * lax.rsqrt(var), 0.0) * (i * xc)
    if valid is not None:
        b = jnp.where(valid, b, 0.0)
        a = jnp.where(valid, a, 1.0)

    sub = lax.broadcasted_iota(jnp.int32, (tb, LRU_WIDTH), 0) & (SUBLANES - 1)
    for s in (1, 2, 4):
        keep = sub >= s
        b = jnp.where(keep, a * pltpu.roll(b, s, axis=0), 0.0) + b
        a = jnp.where(keep, a * pltpu.roll(a, s, axis=0), a)
    h = h_sc[...]
    hs = []
    for g in range(tb // SUBLANES):
        hg = a[g * SUBLANES:(g + 1) * SUBLANES] * h + b[g * SUBLANES:(g + 1) * SUBLANES]
        hs.append(hg)
        h = jnp.broadcast_to(hg[SUBLANES - 1:SUBLANES, :], (SUBLANES, LRU_WIDTH))
    h_sc[...] = h
    out = jnp.concatenate(hs, axis=0) * _gelu_tanh(gate)
    ms = jnp.mean(out * out, axis=-1, keepdims=True)
    return (out * lax.rsqrt(ms + EPS)) * out_g


def _pair_mm(x, y, half_masks):
    yb = y.astype(BF16)
    rhs = jnp.concatenate([yb * half_masks[0], yb * half_masks[1]], axis=0)
    return jnp.dot(x.astype(BF16), rhs, preferred_element_type=F32)


def _pair_inverse(a_list, eye, same16, half_masks):
    def mm(xs, ys):
        return [_pair_mm(x, y, half_masks) for x, y in zip(xs, ys)]

    def plus(xs):
        return [eye + x for x in xs]

    def minus(xs):
        return [eye - x for x in xs]

    d = [jnp.where(same16, a, 0.0) for a in a_list]
    d2 = mm(d, d)
    d4 = mm(d2, d2)
    d8 = mm(d4, d4)
    p = mm(mm(mm(minus(d), plus(d2)), plus(d4)), plus(d8))
    m = mm(p, [a - x for a, x in zip(a_list, d)])
    q = mm(minus(m), plus(mm(m, m)))
    return mm(q, p)


def _dn_branch(act, z, ba, hp_ref, on_ref, s_sc, y_sc, valid):
    tb = act.shape[0]
    hd = DN_HEAD_DIM
    act = act * _sigmoid(act)
    beta_t = _sigmoid(ba)
    g_t = pltpu.roll(-jnp.exp(hp_ref[0:1, :]) * _softplus(ba + hp_ref[1:2, :]),
                     LANES - DN_HEADS, axis=1)
    if valid is not None:
        act = jnp.where(valid, act, 0.0)
        beta_t = jnp.where(valid, beta_t, 0.0)
        g_t = jnp.where(valid, g_t, 0.0)

    heads = []
    for i in range(3 * DN_HEADS):
        seg = act[:, i * hd:(i + 1) * hd]
        if i < 2 * DN_HEADS:
            nrm = lax.rsqrt(jnp.sum(seg * seg, axis=-1, keepdims=True) + EPS)
            if i < DN_HEADS:
                nrm = nrm * (hd ** -0.5)
            seg = seg * nrm
        heads.append(seg)
    qn, kn, vv = heads[:DN_HEADS], heads[DN_HEADS:2 * DN_HEADS], heads[2 * DN_HEADS:]

    row_c = lax.broadcasted_iota(jnp.int32, (tb, LANES), 0) & (CHUNK - 1)
    cum = g_t
    s = 1
    while s < CHUNK:
        cum = cum + jnp.where(row_c >= s, pltpu.roll(cum, s, axis=0), 0.0)
        s *= 2
    e_cum = jnp.exp(cum)

    ri = lax.broadcasted_iota(jnp.int32, (CHUNK, LANES), 0)
    li = lax.broadcasted_iota(jnp.int32, (CHUNK, LANES), 1)
    lo_half = li < CHUNK
    half_masks = (lo_half.astype(BF16), (li >= CHUNK).astype(BF16))
    cj = li & (CHUNK - 1)
    eye_b = ri == cj
    eye = eye_b.astype(F32)
    causal = ri >= cj
    strict = ri > cj
    same16 = (ri // INV_BLOCK) == (cj // INV_BLOCK)
    bd_mask = ((lax.broadcasted_iota(jnp.int32, (2 * hd, 2 * hd), 0) >= hd)
               == (lax.broadcasted_iota(jnp.int32, (2 * hd, 2 * hd), 1) >= hd))
    zero_c = jnp.zeros((CHUNK, hd), F32)
    out_g = on_ref[...]

    def bd_rows(x0, x1):
        z0 = jnp.zeros_like(x0)
        return jnp.concatenate([jnp.concatenate([x0, z0], axis=1),
                                jnp.concatenate([z0, x1], axis=1)], axis=0)

    n_ch = tb // CHUNK
    n_hp = DN_HEADS // 2
    probs = [(ch, hp) for ch in range(n_ch) for hp in range(n_hp)]
    qkm, q_dec, k_dec, rhs, a_list, dec_row = [], [], [], [], [], []
    for ch, hp in probs:
        rows = slice(ch * CHUNK, (ch + 1) * CHUNK)
        h0, h1 = 2 * hp, 2 * hp + 1
        cum_c, beta_c, ecum_c = cum[rows], beta_t[rows], e_cum[rows]
        last = cum_c[CHUNK - 1:CHUNK, :]

        def tile(arr):
            return jnp.where(lo_half, jnp.broadcast_to(arr[:, h0:h0 + 1], (CHUNK, LANES)),
                             jnp.broadcast_to(arr[:, h1:h1 + 1], (CHUNK, LANES)))

        def wide(arr):
            return jnp.concatenate([jnp.broadcast_to(arr[:, h0:h0 + 1], (arr.shape[0], hd)),
                                    jnp.broadcast_to(arr[:, h1:h1 + 1], (arr.shape[0], hd))],
                                   axis=1)

        cum_cp = tile(cum_c)
        cum_rp = jnp.sum(jnp.where(eye_b, cum_cp, 0.0), axis=0, keepdims=True)
        decay = jnp.where(causal, jnp.exp(jnp.where(causal, cum_cp - cum_rp, 0.0)), 0.0)
        q_p = jnp.concatenate([qn[h0][rows], qn[h1][rows]], axis=1)
        k_p = jnp.concatenate([kn[h0][rows], kn[h1][rows]], axis=1)
        v_p = jnp.concatenate([vv[h0][rows], vv[h1][rows]], axis=1)
        qkk = _mm_nt(jnp.concatenate([q_p, k_p], axis=0),
                     bd_rows(kn[h0][rows], kn[h1][rows]))
        qkm.append(qkk[:CHUNK] * decay)
        a_list.append(jnp.where(strict, qkk[CHUNK:] * decay, 0.0) * tile(beta_c))
        vb = v_p * wide(beta_c)
        kb = k_p * wide(beta_c * ecum_c)
        rhs.append(jnp.concatenate(
            [jnp.concatenate([vb[:, :hd], zero_c, kb[:, :hd], zero_c], axis=1),
             jnp.concatenate([zero_c, vb[:, hd:], zero_c, kb[:, hd:]], axis=1)], axis=0))
        q_dec.append(q_p * wide(ecum_c))
        k_dec.append(k_p * wide(jnp.exp(last - cum_c)))
        dec_row.append(wide(jnp.exp(last)))

    t_inv = _pair_inverse(a_list, eye, same16, half_masks)
    sol = [_mm(ti, r) for ti, r in zip(t_inv, rhs)]
    ktuw = [_mm_tn(kd, so) for kd, so in zip(k_dec, sol)]
    quw = [_mm(qm, jnp.concatenate(
               [jnp.concatenate([so[:, :hd], zero_c, so[:, 2 * hd:3 * hd], zero_c], axis=1),
                jnp.concatenate([zero_c, so[:, hd:2 * hd], zero_c, so[:, 3 * hd:]], axis=1)], axis=0))
           for qm, so in zip(qkm, sol)]

    state = [s_sc[hp] for hp in range(n_hp)]
    for ch in range(n_ch):
        rows = slice(ch * CHUNK, (ch + 1) * CHUNK)
        res = []
        for hp in range(n_hp):
            i = ch * n_hp + hp
            k_w = jnp.where(bd_mask, ktuw[i][:, 2 * hd:], 0.0)
            q_eff = q_dec[i] - quw[i][:, 2 * hd:]
            res.append(_mm(jnp.concatenate([k_w, q_eff], axis=0), state[hp]))
        for hp in range(n_hp):
            i = ch * n_hp + hp
            o_p = res[hp][2 * hd:] + quw[i][:, :2 * hd]
            state[hp] = (state[hp] * dec_row[i] - res[hp][:2 * hd]
                         + jnp.where(bd_mask, ktuw[i][:, :2 * hd], 0.0))
            for e in range(2):
                lo = (2 * hp + e) * hd
                o = o_p[:, e * hd:(e + 1) * hd]
                ms = jnp.mean(o * o, axis=-1, keepdims=True)
                zz = z[rows, lo:lo + hd]
                y_sc[rows, lo:lo + hd] = ((o * lax.rsqrt(ms + EPS)) * out_g) * (zz * _sigmoid(zz))
    for hp in range(n_hp):
        s_sc[hp] = state[hp]


def _route(logits):
    lane = lax.broadcasted_iota(jnp.int32, logits.shape, 1)
    lanef = lane.astype(F32)
    neg = -jnp.inf
    big = 1e9
    lg = jnp.where(lane < N_GROUPS, logits, neg)
    mg = jnp.max(lg, axis=-1, keepdims=True)
    p_sel = 1.0 / jnp.sum(jnp.exp(lg - mg), axis=-1, keepdims=True)
    g_sel = jnp.min(jnp.where(lg == mg, lanef, big), axis=-1, keepdims=True)
    e_grp = ((lane - N_GROUPS) >> 3).astype(F32)
    in_grp = (lane >= N_GROUPS) & (lane < N_GROUPS + N_EXPERTS) & (e_grp == g_sel)
    le = jnp.where(in_grp, logits, neg)
    m1 = jnp.max(le, axis=-1, keepdims=True)
    i1 = jnp.min(jnp.where(le == m1, lanef, big), axis=-1, keepdims=True)
    le2 = jnp.where(lanef == i1, neg, le)
    m2 = jnp.max(le2, axis=-1, keepdims=True)
    i2 = jnp.min(jnp.where(le2 == m2, lanef, big), axis=-1, keepdims=True)
    e21 = jnp.exp(m2 - m1)
    w1 = p_sel / (1.0 + e21)
    w2 = p_sel * e21 / (1.0 + e21)
    ids = jnp.where(lane == 0, i1, jnp.where(lane == 1, i2, float(N_GROUPS))) - float(N_GROUPS)
    gates = jnp.where(lane == 0, w1, jnp.where(lane == 1, w2, 0.0))
    return ids.astype(jnp.int32), gates


def _mixer_kernel(x_ref, tail_ref, h0_ref, s0_ref, mg_ref, wm_ref, wba_ref, cw_ref, lp_ref, wg_ref,
                  hp_ref, on_ref, wo_ref, fg_ref, wr_ref, br_ref,
                  h1_ref, xn_ref, id_ref, gt_ref, cn_ref, hl_ref, sl_ref, tl_ref,
                  h_sc, s_sc, cs_sc, y_sc, cnt_sc, *, tb, pad):
    t = pl.program_id(1)
    hd = DN_HEAD_DIM

    @pl.when(t == 0)
    def _():
        zero_hd = jnp.zeros((hd, hd), F32)
        h_sc[...] = h0_ref[...]
        for hp in range(DN_HEADS // 2):
            s_sc[hp] = jnp.concatenate(
                [jnp.concatenate([s0_ref[2 * hp], zero_hd], axis=1),
                 jnp.concatenate([zero_hd, s0_ref[2 * hp + 1]], axis=1)], axis=0)
        cs_sc[0:SUBLANES, :] = tail_ref[...]

    x = x_ref[...]
    ms = jnp.mean(x * x, axis=-1, keepdims=True)
    u = ((x * lax.rsqrt(ms + EPS)) * mg_ref[...]).astype(BF16)
    valid = None
    if pad:
        valid = lax.broadcasted_iota(jnp.int32, (tb, 1), 0) >= pad

    for n in range(0, CONV_COLS, 512):
        cs_sc[SUBLANES:SUBLANES + tb, n:n + 512] = jnp.dot(u, wm_ref[:, n:n + 512],
                                                           preferred_element_type=F32)
    gate = jnp.dot(u, wm_ref[:, CONV_COLS:CONV_COLS + LRU_WIDTH], preferred_element_type=F32)
    z = jnp.dot(u, wm_ref[:, CONV_COLS + LRU_WIDTH:], preferred_element_type=F32)
    ba = jnp.dot(u, wba_ref[...], preferred_element_type=F32)

    def conv(lo, width):
        acc = cs_sc[SUBLANES:SUBLANES + tb, lo:lo + width] * cw_ref[3:4, lo:lo + width]
        for d in range(1, CONV_WIDTH):
            acc = acc + (cs_sc[SUBLANES - d:SUBLANES - d + tb, lo:lo + width]
                         * cw_ref[3 - d:4 - d, lo:lo + width])
        return acc

    xc = conv(0, LRU_WIDTH) + lp_ref[0:1, :]
    act = conv(LRU_WIDTH, 3 * DN_WIDTH)
    cs_sc[0:SUBLANES, :] = cs_sc[tb:tb + SUBLANES, :]

    y_lru = _lru_branch(xc, gate, lp_ref, wg_ref, h_sc, valid)
    _dn_branch(act, z, ba, hp_ref, on_ref, s_sc, y_sc, valid)

    mix = jnp.dot(y_lru.astype(BF16), wo_ref[:LRU_WIDTH, :], preferred_element_type=F32)
    mix = mix + jnp.dot(y_sc[...].astype(BF16), wo_ref[LRU_WIDTH:, :], preferred_element_type=F32)
    h1 = x + mix
    h1_ref[...] = h1
    ms1 = jnp.mean(h1 * h1, axis=-1, keepdims=True)
    xn = (h1 * lax.rsqrt(ms1 + EPS)) * fg_ref[...]
    xn_ref[...] = _pack_bf16_pairs(xn)
    ids, gates = _route(jnp.dot(xn.astype(BF16), wr_ref[...], preferred_element_type=F32)
                        + br_ref[...])
    id_ref[...] = ids
    gt_ref[...] = gates

    lane = lax.broadcasted_iota(jnp.int32, ids.shape, 1)
    hits = ((lane == ids[:, 0:1]).astype(F32) + (lane == ids[:, 1:2]).astype(F32))
    if valid is not None:
        hits = jnp.where(valid, hits, 0.0)
    hist = hits[0:SUBLANES]
    for g in range(1, tb // SUBLANES):
        hist = hist + hits[g * SUBLANES:(g + 1) * SUBLANES]

    @pl.when(t == 0)
    def _():
        cnt_sc[...] = hist

    @pl.when(t > 0)
    def _():
        cnt_sc[...] = cnt_sc[...] + hist

    @pl.when(t == pl.num_programs(1) - 1)
    def _():
        cn_ref[...] = cnt_sc[...]
        hl_ref[...] = h_sc[...]
        for h in range(DN_HEADS):
            e = h % 2
            sl_ref[h] = s_sc[h // 2][e * hd:(e + 1) * hd, e * hd:(e + 1) * hd]
        tl_ref[...] = cs_sc[0:SUBLANES, :]


def _mixer(x3, tail8, h0, s0, weights, tb, pad):
    bsz, t, _ = x3.shape
    assert tb % CHUNK == 0 and t % tb == 0 and (pad == 0 or t == tb)
    fix2 = lambda b, i: (0, 0)
    fix3 = lambda b, i: (0, 0, 0)
    blk = lambda b, i: (b, i, 0)
    per_b = lambda b, i: (b, 0, 0)
    w_specs = [pl.BlockSpec(w.shape, fix2 if w.ndim == 2 else fix3) for w in weights]
    return pl.pallas_call(
        functools.partial(_mixer_kernel, tb=tb, pad=pad),
        grid=(bsz, t // tb),
        in_specs=[pl.BlockSpec((None, tb, D_MODEL), blk),
                  pl.BlockSpec((SUBLANES, CONV_COLS), fix2),
                  pl.BlockSpec((SUBLANES, LRU_WIDTH), fix2),
                  pl.BlockSpec((DN_HEADS, DN_HEAD_DIM, DN_HEAD_DIM), fix3)] + w_specs,
        out_specs=[pl.BlockSpec((None, tb, D_MODEL), blk),
                   pl.BlockSpec((None, tb, D_MODEL // 2), blk),
                   pl.BlockSpec((None, tb, LANES), blk),
                   pl.BlockSpec((None, tb, LANES), blk),
                   pl.BlockSpec((None, SUBLANES, LANES), per_b),
                   pl.BlockSpec((None, SUBLANES, LRU_WIDTH), per_b),
                   pl.BlockSpec((None, DN_HEADS, DN_HEAD_DIM, DN_HEAD_DIM),
                                lambda b, i: (b, 0, 0, 0)),
                   pl.BlockSpec((None, SUBLANES, CONV_COLS), per_b)],
        out_shape=[jax.ShapeDtypeStruct((bsz, t, D_MODEL), F32),
                   jax.ShapeDtypeStruct((bsz, t, D_MODEL // 2), jnp.uint32),
                   jax.ShapeDtypeStruct((bsz, t, LANES), jnp.int32),
                   jax.ShapeDtypeStruct((bsz, t, LANES), F32),
                   jax.ShapeDtypeStruct((bsz, SUBLANES, LANES), F32),
                   jax.ShapeDtypeStruct((bsz, SUBLANES, LRU_WIDTH), F32),
                   jax.ShapeDtypeStruct((bsz, DN_HEADS, DN_HEAD_DIM, DN_HEAD_DIM), F32),
                   jax.ShapeDtypeStruct((bsz, SUBLANES, CONV_COLS), F32)],
        scratch_shapes=[pltpu.VMEM((SUBLANES, LRU_WIDTH), F32),
                        pltpu.VMEM((DN_HEADS // 2, 2 * DN_HEAD_DIM, 2 * DN_HEAD_DIM), F32),
                        pltpu.VMEM((tb + SUBLANES, CONV_COLS), F32),
                        pltpu.VMEM((tb, DN_WIDTH), F32),
                        pltpu.VMEM((SUBLANES, LANES), F32)],
        compiler_params=_cparams("parallel", "arbitrary"),
        name="mixer",
    )(x3, tail8, h0, s0, *weights)


def _expert_kernel(be_ref, x_ref, wg_ref, wu_ref, wd_ref, y_ref, wgu_sc, wd_sc):
    i = pl.program_id(0)
    prev = be_ref[jnp.maximum(i - 1, 0)]

    @pl.when((i == 0) | (be_ref[i] != prev))
    def _():
        wgu_sc[:, :D_EXPERT] = wg_ref[...].astype(BF16)
        wgu_sc[:, D_EXPERT:] = wu_ref[...].astype(BF16)
        wd_sc[...] = wd_ref[...].astype(BF16)

    x = _unpack_bf16_pairs(x_ref[...]).astype(BF16)
    gu = jnp.dot(x, wgu_sc[...], preferred_element_type=F32)
    g, u = gu[:, :D_EXPERT], gu[:, D_EXPERT:]
    hmid = (g * _sigmoid(g)) * u
    y_ref[...] = _pack_bf16_pairs(jnp.dot(hmid.astype(BF16), wd_sc[...],
                                          preferred_element_type=F32))


def _experts(block_expert, x_buf, w_gate, w_up, w_down):
    cap = x_buf.shape[0]
    by_expert = lambda i, be: (be[i], 0, 0)
    grid_spec = pltpu.PrefetchScalarGridSpec(
        num_scalar_prefetch=1,
        grid=(cap // MOE_ROWS,),
        in_specs=[pl.BlockSpec((MOE_ROWS, D_MODEL // 2), lambda i, be: (i, 0)),
                  pl.BlockSpec((None, D_MODEL, D_EXPERT), by_expert),
                  pl.BlockSpec((None, D_MODEL, D_EXPERT), by_expert),
                  pl.BlockSpec((None, D_EXPERT, D_MODEL), by_expert)],
        out_specs=pl.BlockSpec((MOE_ROWS, D_MODEL // 2), lambda i, be: (i, 0)),
        scratch_shapes=[pltpu.VMEM((D_MODEL, 2 * D_EXPERT), BF16),
                        pltpu.VMEM((D_EXPERT, D_MODEL), BF16)],
    )
    return pl.pallas_call(
        _expert_kernel,
        grid_spec=grid_spec,
        out_shape=jax.ShapeDtypeStruct((cap, D_MODEL // 2), jnp.uint32),
        compiler_params=_cparams("arbitrary"),
        name="experts",
    )(block_expert, x_buf, w_gate, w_up, w_down)


def _row_gather(table, idx):
    n_idx = idx.shape[0]
    d = table.shape[1]
    sc = plsc.get_sparse_core_info()
    n_workers = sc.num_cores * sc.num_subcores
    w = GATHER_WINDOW
    per_w = n_idx // n_workers
    n_steps = per_w // w
    assert per_w * n_workers == n_idx and n_steps * w == per_w and n_steps % 2 == 0
    mesh = plsc.VectorSubcoreMesh(core_axis_name="core", subcore_axis_name="subcore")

    @functools.partial(
        pl.kernel, out_type=jax.ShapeDtypeStruct((n_idx, d), table.dtype), mesh=mesh,
        scratch_types=[pltpu.VMEM((per_w,), jnp.int32),
                       pltpu.VMEM((2, w, d), table.dtype),
                       pltpu.SemaphoreType.DMA((2,)),
                       pltpu.SemaphoreType.DMA((2,))])
    def gather(x_hbm, i_hbm, o_hbm, idx_v, rows_v, g_sem, w_sem):
        wid = lax.axis_index("subcore") * sc.num_cores + lax.axis_index("core")
        base = wid * per_w
        pltpu.sync_copy(i_hbm.at[pl.ds(base, per_w)], idx_v)

        def fetch(s, b):
            return pltpu.make_async_copy(x_hbm.at[idx_v.at[pl.ds(s * w, w)]], rows_v.at[b],
                                         g_sem.at[b])

        def flush(s, b):
            return pltpu.make_async_copy(rows_v.at[b], o_hbm.at[pl.ds(base + s * w, w)],
                                         w_sem.at[b])

        fetch(0, 0).start()

        @pl.loop(0, n_steps, step=2)
        def _(s2):
            for b in range(2):
                s = s2 + b

                @pl.when(s + 1 < n_steps)
                def _():
                    @pl.when(s >= 1)
                    def _():
                        flush(s - 1, 1 - b).wait()
                    fetch(s + 1, 1 - b).start()

                fetch(s, b).wait()
                flush(s, b).start()

        flush(n_steps - 2, 0).wait()
        flush(n_steps - 1, 1).wait()

    return gather(table, idx)


def _combine_kernel(h1_ref, y0_ref, y1_ref, gt_ref, fg_ref, o_ref):
    gt = gt_ref[...]
    h = (h1_ref[...] + gt[:, 0:1] * _unpack_bf16_pairs(y0_ref[...])
         + gt[:, 1:2] * _unpack_bf16_pairs(y1_ref[...]))
    ms = jnp.mean(h * h, axis=-1, keepdims=True)
    o_ref[...] = (h * lax.rsqrt(ms + EPS)) * fg_ref[...]


def _combine(h1, y_tok, gates, final_g, tm):
    n = h1.shape[0]
    row = lambda i: (i, 0)
    return pl.pallas_call(
        _combine_kernel,
        grid=(n // tm,),
        in_specs=[pl.BlockSpec((tm, D_MODEL), row),
                  pl.BlockSpec((tm, D_MODEL // 2), row),
                  pl.BlockSpec((tm, D_MODEL // 2), lambda i: (i + n // tm, 0)),
                  pl.BlockSpec((tm, LANES), row),
                  pl.BlockSpec((1, D_MODEL), lambda i: (0, 0))],
        out_specs=pl.BlockSpec((tm, D_MODEL), row),
        out_shape=jax.ShapeDtypeStruct((n, D_MODEL), F32),
        compiler_params=_cparams("parallel"),
        name="combine",
    )(h1, y_tok, y_tok, gates, final_g)


def _block_diag(blocks):
    n, r, c = blocks.shape
    out = jnp.zeros((n * r, n * c), blocks.dtype)
    for i in range(n):
        out = out.at[i * r:(i + 1) * r, i * c:(i + 1) * c].set(blocks[i])
    return out


def _pad_rows(a, rows):
    return jnp.pad(a, ((0, rows - a.shape[0]), (0, 0)))


def _dispatch_tables(ids2, counts):
    n = ids2.shape[0]
    n_assign = 2 * n
    eflat = ids2.T.reshape(-1)
    iota = jnp.arange(n_assign, dtype=jnp.int32)
    _, order = lax.sort((eflat, iota), num_keys=1, is_stable=True)
    _, rank = lax.sort((order, iota), num_keys=1, is_stable=False)
    starts = jnp.cumsum(counts) - counts
    padded = (counts + MOE_ROWS - 1) // MOE_ROWS * MOE_ROWS
    pend = jnp.cumsum(padded)
    pstart = pend - padded
    n_blocks = -(-(n_assign + N_EXPERTS * (MOE_ROWS - 1)) // MOE_ROWS)
    cap = n_blocks * MOE_ROWS
    block_start = jnp.arange(n_blocks, dtype=jnp.int32) * MOE_ROWS
    block_expert = jnp.minimum(jnp.sum(block_start[:, None] >= pend[None, :], axis=1),
                               N_EXPERTS - 1).astype(jnp.int32)
    slot = jnp.arange(cap, dtype=jnp.int32)
    slot_e = jnp.repeat(block_expert, MOE_ROWS)
    in_e = slot - pstart[slot_e]
    j = jnp.clip(in_e + starts[slot_e], 0, n_assign - 1)
    buf_tok = jnp.where(in_e < counts[slot_e], order[j], slot) % n
    dest = pstart[eflat] + rank - starts[eflat]
    return buf_tok, dest, block_expert


def kernel(x, meta_tokens, mix_norm, w_in, lru_conv_w, lru_conv_b, lru_w_r, lru_b_r, lru_w_i,
           lru_b_i, lru_lambda, lru_out_norm, dn_conv_w, dn_a_log, dn_dt_bias, dn_out_norm, w_out,
           ffn_norm, router_group_w, router_group_b, router_expert_w, router_expert_b, moe_w_gate,
           moe_w_up, moe_w_down, final_norm):
    bsz, seq, d = x.shape
    n = bsz * seq
    l = 0

    lw = LRU_WIDTH
    w = w_in[l]
    w_main = jnp.concatenate([w[:, :lw], w[:, 2 * lw:2 * lw + 3 * DN_WIDTH], w[:, lw:2 * lw],
                              w[:, 2 * lw + 3 * DN_WIDTH:2 * lw + 4 * DN_WIDTH]], axis=1).astype(BF16)
    w_ba = jnp.pad(w[:, 2 * lw + 4 * DN_WIDTH:], ((0, 0), (0, LANES - 2 * DN_HEADS))).astype(BF16)
    mix_g = mix_norm[l][None, :]
    conv_w = _pad_rows(jnp.concatenate([lru_conv_w[l], dn_conv_w[l]], axis=1), SUBLANES)
    lru_p = _pad_rows(jnp.stack([lru_conv_b[l], lru_b_r[l], lru_b_i[l], lru_lambda[l],
                                 lru_out_norm[l]]), SUBLANES)
    hb = LRU_BLOCKS // 2
    w_gate = jnp.stack([
        jnp.concatenate([_block_diag(lru_w_r[l][h * hb:(h + 1) * hb]),
                         _block_diag(lru_w_i[l][h * hb:(h + 1) * hb])], axis=1)
        for h in range(2)]).astype(BF16)
    head_p = jnp.zeros((SUBLANES, LANES), F32)
    head_p = head_p.at[0, DN_HEADS:2 * DN_HEADS].set(dn_a_log[l])
    head_p = head_p.at[1, DN_HEADS:2 * DN_HEADS].set(dn_dt_bias[l])
    dn_on = dn_out_norm[l][None, :]
    w_o = w_out[l].astype(BF16)
    ffn_g = ffn_norm[l][None, :]
    w_router = jnp.pad(jnp.concatenate([router_group_w[l], router_expert_w[l]], axis=1),
                       ((0, 0), (0, LANES - N_GROUPS - N_EXPERTS))).astype(BF16)
    b_router = jnp.pad(jnp.concatenate([router_group_b[l], router_expert_b[l]]),
                       (0, LANES - N_GROUPS - N_EXPERTS))[None, :]
    final_g = final_norm[None, :]
    weights = (mix_g, w_main, w_ba, conv_w, lru_p, w_gate, head_p, dn_on, w_o, ffn_g, w_router,
               b_router)

    meta_pad = CHUNK - N_META
    prefix = jnp.pad(meta_tokens, ((meta_pad, 0), (0, 0)))[None]
    zeros = lambda *s: jnp.zeros(s, F32)
    *_, h_meta, s_meta, tail_meta = _mixer(
        prefix, zeros(SUBLANES, CONV_COLS), zeros(SUBLANES, LRU_WIDTH),
        zeros(DN_HEADS, DN_HEAD_DIM, DN_HEAD_DIM), weights, CHUNK, meta_pad)

    h1, xn, ids, gates, cnt, _, _, _ = _mixer(x, tail_meta[0], h_meta[0], s_meta[0], weights,
                                              256, 0)
    h1 = h1.reshape(n, d)
    gates = gates.reshape(n, LANES)
    counts = jnp.sum(cnt, axis=(0, 1))[:N_EXPERTS].astype(jnp.int32)

    buf_tok, dest, block_expert = _dispatch_tables(ids.reshape(n, LANES)[:, :2], counts)
    x_buf = _row_gather(xn.reshape(n, d // 2), buf_tok)
    y_buf = _experts(block_expert, x_buf, moe_w_gate[l], moe_w_up[l], moe_w_down[l])
    y_tok = _row_gather(y_buf, dest)
    out = _combine(h1, y_tok, gates, final_g, 512)
    return out.reshape(bsz, seq, d)
```

```python
import functools

import jax
import jax.numpy as jnp
from jax import lax
from jax.experimental import pallas as pl
from jax.experimental.pallas import tpu as pltpu
from jax.experimental.pallas import tpu_sc as plsc

F32 = jnp.float32
BF16 = jnp.bfloat16

D_MODEL = 1024
N_META = 16
CHUNK = 64
CONV_WIDTH = 4
LRU_WIDTH = 512
LRU_BLOCKS = 8
LRU_C = 8.0
DN_HEADS = 4
DN_HEAD_DIM = 128
DN_WIDTH = DN_HEADS * DN_HEAD_DIM
N_GROUPS = 4
EXPERTS_PER_GROUP = 8
N_EXPERTS = N_GROUPS * EXPERTS_PER_GROUP
D_EXPERT = 256
EPS = 1e-6
CONV_COLS = LRU_WIDTH + 3 * DN_WIDTH
GATE_COLS = LRU_WIDTH + DN_WIDTH
LANES = 128
SUBLANES = 8
INV_BLOCK = 16
MOE_ROWS = 512
ROUTE_ROWS = 48
VMEM_LIMIT = 56 * 1024 * 1024
GATHER_WINDOW = 64


def _cparams(*sem):
    return pltpu.CompilerParams(dimension_semantics=sem, vmem_limit_bytes=VMEM_LIMIT)


def _sigmoid(x):
    return 0.5 * jnp.tanh(0.5 * x) + 0.5


def _pack_bf16_pairs(x):
    c = x.shape[1] // 2

    def rne(v):
        b = lax.bitcast_convert_type(v, jnp.uint32)
        return b + jnp.uint32(0x7FFF) + ((b >> 16) & jnp.uint32(1))

    return (rne(x[:, :c]) >> 16) | (rne(x[:, c:]) & jnp.uint32(0xFFFF0000))


def _unpack_bf16_pairs(w):
    lo = lax.bitcast_convert_type(w << 16, F32)
    hi = lax.bitcast_convert_type(w & jnp.uint32(0xFFFF0000), F32)
    return jnp.concatenate([lo, hi], axis=1)


def _softplus(x):
    return jnp.maximum(x, 0.0) + jnp.log(1.0 + jnp.exp(-jnp.abs(x)))


def _gelu_tanh(x):
    return 0.5 * x * (1.0 + jnp.tanh(0.7978845608028654 * (x + 0.044715 * (x * x * x))))


def _mm(a, b):
    return jnp.dot(a.astype(BF16), b.astype(BF16), preferred_element_type=F32)


def _mm_nt(a, b):
    return lax.dot_general(a.astype(BF16), b.astype(BF16), (((1,), (1,)), ((), ())),
                           preferred_element_type=F32)


def _mm_tn(a, b):
    return lax.dot_general(a.astype(BF16), b.astype(BF16), (((0,), (0,)), ((), ())),
                           preferred_element_type=F32)


def _lru_branch(xc, gate, p_ref, wg_ref, h_sc, valid):
    tb = xc.shape[0]
    b_r, b_i = p_ref[1:2, :], p_ref[2:3, :]
    lam, out_g = p_ref[3:4, :], p_ref[4:5, :]
    half = LRU_WIDTH // 2
    xcb = xc.astype(BF16)
    g0 = jnp.dot(xcb[:, :half], wg_ref[0], preferred_element_type=F32)
    g1 = jnp.dot(xcb[:, half:], wg_ref[1], preferred_element_type=F32)
    r = _sigmoid(jnp.concatenate([g0[:, :half], g1[:, :half]], axis=1) + b_r)
    i = _sigmoid(jnp.concatenate([g0[:, half:], g1[:, half:]], axis=1) + b_i)
    log_a = (-LRU_C) * r * _softplus(-lam)
    a = jnp.exp(log_a)
    var = 1.0 - jnp.exp(2.0 * log_a)
    b = jnp.where(var > 0.0, var * lax.rsqrt(var), 0.0) * (i * xc)
    if valid is not None:
        b = jnp.where(valid, b, 0.0)
        a = jnp.where(valid, a, 1.0)

    sub = lax.broadcasted_iota(jnp.int32, (tb, LRU_WIDTH), 0) & (SUBLANES - 1)
    for s in (1, 2, 4):
        keep = sub >= s
        b = jnp.where(keep, a * pltpu.roll(b, s, axis=0), 0.0) + b
        a = jnp.where(keep, a * pltpu.roll(a, s, axis=0), a)
    h = h_sc[...]
    hs = []
    for g in range(tb // SUBLANES):
        hg = a[g * SUBLANES:(g + 1) * SUBLANES] * h + b[g * SUBLANES:(g + 1) * SUBLANES]
        hs.append(hg)
        h = jnp.broadcast_to(hg[SUBLANES - 1:SUBLANES, :], (SUBLANES, LRU_WIDTH))
    h_sc[...] = h
    out = jnp.concatenate(hs, axis=0) * _gelu_tanh(gate)
    ms = jnp.mean(out * out, axis=-1, keepdims=True)
    return (out * lax.rsqrt(ms + EPS)) * out_g


def _pair_mm(x, y, half_masks):
    yb = y.astype(BF16)
    rhs = jnp.concatenate([yb * half_masks[0], yb * half_masks[1]], axis=0)
    return jnp.dot(x.astype(BF16), rhs, preferred_element_type=F32)


def _pair_inverse(a_list, eye, same16, half_masks):
    def mm(xs, ys):
        return [_pair_mm(x, y, half_masks) for x, y in zip(xs, ys)]

    def plus(xs):
        return [eye + x for x in xs]

    def minus(xs):
        return [eye - x for x in xs]

    d = [jnp.where(same16, a, 0.0) for a in a_list]
    d2 = mm(d, d)
    d4 = mm(d2, d2)
    d8 = mm(d4, d4)
    p = mm(mm(mm(minus(d), plus(d2)), plus(d4)), plus(d8))
    m = mm(p, [a - x for a, x in zip(a_list, d)])
    q = mm(minus(m), plus(mm(m, m)))
    return mm(q, p)


def _dn_branch(act, z, ba, hp_ref, on_ref, s_sc, y_sc, valid):
    tb = act.shape[0]
    hd = DN_HEAD_DIM
    act = act * _sigmoid(act)
    beta_t = _sigmoid(ba)
    g_t = pltpu.roll(-jnp.exp(hp_ref[0:1, :]) * _softplus(ba + hp_ref[1:2, :]),
                     LANES - DN_HEADS, axis=1)
    if valid is not None:
        act = jnp.where(valid, act, 0.0)
        beta_t = jnp.where(valid, beta_t, 0.0)
        g_t = jnp.where(valid, g_t, 0.0)

    heads = []
    for i in range(3 * DN_HEADS):
        seg = act[:, i * hd:(i + 1) * hd]
        if i < 2 * DN_HEADS:
            nrm = lax.rsqrt(jnp.sum(seg * seg, axis=-1, keepdims=True) + EPS)
            if i < DN_HEADS:
                nrm = nrm * (hd ** -0.5)
            seg = seg * nrm
        heads.append(seg)
    qn, kn, vv = heads[:DN_HEADS], heads[DN_HEADS:2 * DN_HEADS], heads[2 * DN_HEADS:]

    row_c = lax.broadcasted_iota(jnp.int32, (tb, LANES), 0) & (CHUNK - 1)
    cum = g_t
    s = 1
    while s < CHUNK:
        cum = cum + jnp.where(row_c >= s, pltpu.roll(cum, s, axis=0), 0.0)
        s *= 2
    e_cum = jnp.exp(cum)

    ri = lax.broadcasted_iota(jnp.int32, (CHUNK, LANES), 0)
    li = lax.broadcasted_iota(jnp.int32, (CHUNK, LANES), 1)
    lo_half = li < CHUNK
    half_masks = (lo_half.astype(BF16), (li >= CHUNK).astype(BF16))
    cj = li & (CHUNK - 1)
    eye_b = ri == cj
    eye = eye_b.astype(F32)
    causal = ri >= cj
    strict = ri > cj
    same16 = (ri // INV_BLOCK) == (cj // INV_BLOCK)
    bd_mask = ((lax.broadcasted_iota(jnp.int32, (2 * hd, 2 * hd), 0) >= hd)
               == (lax.broadcasted_iota(jnp.int32, (2 * hd, 2 * hd), 1) >= hd))
    zero_c = jnp.zeros((CHUNK, hd), F32)
    out_g = on_ref[...]

    def bd_rows(x0, x1):
        z0 = jnp.zeros_like(x0)
        return jnp.concatenate([jnp.concatenate([x0, z0], axis=1),
                                jnp.concatenate([z0, x1], axis=1)], axis=0)

    n_ch = tb // CHUNK
    n_hp = DN_HEADS // 2
    probs = [(ch, hp) for ch in range(n_ch) for hp in range(n_hp)]
    qkm, q_dec, k_dec, rhs, a_list, dec_row = [], [], [], [], [], []
    for ch, hp in probs:
        rows = slice(ch * CHUNK, (ch + 1) * CHUNK)
        h0, h1 = 2 * hp, 2 * hp + 1
        cum_c, beta_c, ecum_c = cum[rows], beta_t[rows], e_cum[rows]
        last = cum_c[CHUNK - 1:CHUNK, :]

        def tile(arr):
            return jnp.where(lo_half, jnp.broadcast_to(arr[:, h0:h0 + 1], (CHUNK, LANES)),
                             jnp.broadcast_to(arr[:, h1:h1 + 1], (CHUNK, LANES)))

        def wide(arr):
            return jnp.concatenate([jnp.broadcast_to(arr[:, h0:h0 + 1], (arr.shape[0], hd)),
                                    jnp.broadcast_to(arr[:, h1:h1 + 1], (arr.shape[0], hd))],
                                   axis=1)

        cum_cp = tile(cum_c)
        cum_rp = jnp.sum(jnp.where(eye_b, cum_cp, 0.0), axis=0, keepdims=True)
        decay = jnp.where(causal, jnp.exp(jnp.where(causal, cum_cp - cum_rp, 0.0)), 0.0)
        q_p = jnp.concatenate([qn[h0][rows], qn[h1][rows]], axis=1)
        k_p = jnp.concatenate([kn[h0][rows], kn[h1][rows]], axis=1)
        v_p = jnp.concatenate([vv[h0][rows], vv[h1][rows]], axis=1)
        qkk = _mm_nt(jnp.concatenate([q_p, k_p], axis=0),
                     bd_rows(kn[h0][rows], kn[h1][rows]))
        qkm.append(qkk[:CHUNK] * decay)
        a_list.append(jnp.where(strict, qkk[CHUNK:] * decay, 0.0) * tile(beta_c))
        vb = v_p * wide(beta_c)
        kb = k_p * wide(beta_c * ecum_c)
        rhs.append(jnp.concatenate(
            [jnp.concatenate([vb[:, :hd], zero_c, kb[:, :hd], zero_c], axis=1),
             jnp.concatenate([zero_c, vb[:, hd:], zero_c, kb[:, hd:]], axis=1)], axis=0))
        q_dec.append(q_p * wide(ecum_c))
        k_dec.append(k_p * wide(jnp.exp(last - cum_c)))
        dec_row.append(wide(jnp.exp(last)))

    t_inv = _pair_inverse(a_list, eye, same16, half_masks)
    sol = [_mm(ti, r) for ti, r in zip(t_inv, rhs)]
    ktuw = [_mm_tn(kd, so) for kd, so in zip(k_dec, sol)]
    quw = [_mm(qm, jnp.concatenate(
               [jnp.concatenate([so[:, :hd], zero_c, so[:, 2 * hd:3 * hd], zero_c], axis=1),
                jnp.concatenate([zero_c, so[:, hd:2 * hd], zero_c, so[:, 3 * hd:]], axis=1)], axis=0))
           for qm, so in zip(qkm, sol)]

    state = [s_sc[hp] for hp in range(n_hp)]
    for ch in range(n_ch):
        rows = slice(ch * CHUNK, (ch + 1) * CHUNK)
        res = []
        for hp in range(n_hp):
            i = ch * n_hp + hp
            k_w = jnp.where(bd_mask, ktuw[i][:, 2 * hd:], 0.0)
            q_eff = q_dec[i] - quw[i][:, 2 * hd:]
            res.append(_mm(jnp.concatenate([k_w, q_eff], axis=0), state[hp]))
        for hp in range(n_hp):
            i = ch * n_hp + hp
            o_p = res[hp][2 * hd:] + quw[i][:, :2 * hd]
            state[hp] = (state[hp] * dec_row[i] - res[hp][:2 * hd]
                         + jnp.where(bd_mask, ktuw[i][:, :2 * hd], 0.0))
            for e in range(2):
                lo = (2 * hp + e) * hd
                o = o_p[:, e * hd:(e + 1) * hd]
                ms = jnp.mean(o * o, axis=-1, keepdims=True)
                zz = z[rows, lo:lo + hd]
                y_sc[rows, lo:lo + hd] = ((o * lax.rsqrt(ms + EPS)) * out_g) * (zz * _sigmoid(zz))
    for hp in range(n_hp):
        s_sc[hp] = state[hp]


def _route(logits_t):
    tm = logits_t.shape[1]
    row = lax.broadcasted_iota(jnp.int32, (SUBLANES, tm), 0).astype(F32)
    neg = -jnp.inf
    big = 1e9
    lg = jnp.where(row < N_GROUPS, logits_t[0:SUBLANES], neg)
    mg = jnp.max(lg, axis=0, keepdims=True)
    p_sel = 1.0 / jnp.sum(jnp.exp(lg - mg), axis=0, keepdims=True)
    g_sel = jnp.min(jnp.where(lg == mg, row, big), axis=0, keepdims=True)
    le = logits_t[SUBLANES:2 * SUBLANES]
    for g in range(1, N_GROUPS):
        le = jnp.where(g_sel == float(g), logits_t[(g + 1) * SUBLANES:(g + 2) * SUBLANES], le)
    m1 = jnp.max(le, axis=0, keepdims=True)
    i1 = jnp.min(jnp.where(le == m1, row, big), axis=0, keepdims=True)
    le2 = jnp.where(row == i1, neg, le)
    m2 = jnp.max(le2, axis=0, keepdims=True)
    i2 = jnp.min(jnp.where(le2 == m2, row, big), axis=0, keepdims=True)
    e21 = jnp.exp(m2 - m1)
    w1 = p_sel / (1.0 + e21)
    w2 = p_sel * e21 / (1.0 + e21)
    base = g_sel * float(EXPERTS_PER_GROUP)
    return jnp.where(row == 0.0, base + i1,
                     jnp.where(row == 1.0, base + i2,
                               jnp.where(row == 2.0, w1, jnp.where(row == 3.0, w2, 0.0))))


def _mixer_kernel(x_ref, tail_ref, h0_ref, s0_ref, mg_ref, wm_ref, wba_ref, cw_ref, lp_ref, wg_ref,
                  hp_ref, on_ref, wo_ref, fg_ref, wr_ref, br_ref,
                  h1_ref, xn_ref, rt_ref, cn_ref, hl_ref, sl_ref, tl_ref,
                  h_sc, s_sc, cs_sc, y_sc, cnt_sc, *, tb, pad):
    t = pl.program_id(1)
    hd = DN_HEAD_DIM

    @pl.when(t == 0)
    def _():
        zero_hd = jnp.zeros((hd, hd), F32)
        h_sc[...] = h0_ref[...]
        for hp in range(DN_HEADS // 2):
            s_sc[hp] = jnp.concatenate(
                [jnp.concatenate([s0_ref[2 * hp], zero_hd], axis=1),
                 jnp.concatenate([zero_hd, s0_ref[2 * hp + 1]], axis=1)], axis=0)
        cs_sc[0:SUBLANES, :] = tail_ref[...]

    x = x_ref[...]
    ms = jnp.mean(x * x, axis=-1, keepdims=True)
    u = ((x * lax.rsqrt(ms + EPS)) * mg_ref[...]).astype(BF16)
    valid = None
    if pad:
        valid = lax.broadcasted_iota(jnp.int32, (tb, 1), 0) >= pad

    for n in range(0, CONV_COLS, 512):
        cs_sc[SUBLANES:SUBLANES + tb, n:n + 512] = jnp.dot(u, wm_ref[:, n:n + 512],
                                                           preferred_element_type=F32)
    gate = jnp.dot(u, wm_ref[:, CONV_COLS:CONV_COLS + LRU_WIDTH], preferred_element_type=F32)
    z = jnp.dot(u, wm_ref[:, CONV_COLS + LRU_WIDTH:], preferred_element_type=F32)
    ba = jnp.dot(u, wba_ref[...], preferred_element_type=F32)

    def conv(lo, width):
        acc = cs_sc[SUBLANES:SUBLANES + tb, lo:lo + width] * cw_ref[3:4, lo:lo + width]
        for d in range(1, CONV_WIDTH):
            acc = acc + (cs_sc[SUBLANES - d:SUBLANES - d + tb, lo:lo + width]
                         * cw_ref[3 - d:4 - d, lo:lo + width])
        return acc

    xc = conv(0, LRU_WIDTH) + lp_ref[0:1, :]
    act = conv(LRU_WIDTH, 3 * DN_WIDTH)
    cs_sc[0:SUBLANES, :] = cs_sc[tb:tb + SUBLANES, :]

    y_lru = _lru_branch(xc, gate, lp_ref, wg_ref, h_sc, valid)
    _dn_branch(act, z, ba, hp_ref, on_ref, s_sc, y_sc, valid)

    mix = jnp.dot(y_lru.astype(BF16), wo_ref[:LRU_WIDTH, :], preferred_element_type=F32)
    mix = mix + jnp.dot(y_sc[...].astype(BF16), wo_ref[LRU_WIDTH:, :], preferred_element_type=F32)
    h1 = x + mix
    h1_ref[...] = h1
    ms1 = jnp.mean(h1 * h1, axis=-1, keepdims=True)
    xn = (h1 * lax.rsqrt(ms1 + EPS)) * fg_ref[...]
    xn_ref[...] = _pack_bf16_pairs(xn)
    logits_t = lax.dot_general(wr_ref[...], xn.astype(BF16), (((1,), (1,)), ((), ())),
                               preferred_element_type=F32) + br_ref[:, 0:1]
    route = _route(logits_t)
    rt_ref[...] = route

    erow = lax.broadcasted_iota(jnp.int32, (N_EXPERTS, tb), 0).astype(F32)
    hist = (erow == route[0:1, :]).astype(F32) + (erow == route[1:2, :]).astype(F32)
    if pad:
        hist = jnp.where(lax.broadcasted_iota(jnp.int32, (1, tb), 1) >= pad, hist, 0.0)

    @pl.when(t == 0)
    def _():
        cnt_sc[...] = hist

    @pl.when(t > 0)
    def _():
        cnt_sc[...] = cnt_sc[...] + hist

    @pl.when(t == pl.num_programs(1) - 1)
    def _():
        cn_ref[...] = cnt_sc[...]
        hl_ref[...] = h_sc[...]
        for h in range(DN_HEADS):
            e = h % 2
            sl_ref[h] = s_sc[h // 2][e * hd:(e + 1) * hd, e * hd:(e + 1) * hd]
        tl_ref[...] = cs_sc[0:SUBLANES, :]


def _mixer(x3, tail8, h0, s0, weights, tb, pad):
    bsz, t, _ = x3.shape
    assert tb % CHUNK == 0 and t % tb == 0 and (pad == 0 or t == tb)
    fix2 = lambda b, i: (0, 0)
    fix3 = lambda b, i: (0, 0, 0)
    blk = lambda b, i: (b, i, 0)
    per_b = lambda b, i: (b, 0, 0)
    w_specs = [pl.BlockSpec(w.shape, fix2 if w.ndim == 2 else fix3) for w in weights]
    return pl.pallas_call(
        functools.partial(_mixer_kernel, tb=tb, pad=pad),
        grid=(bsz, t // tb),
        in_specs=[pl.BlockSpec((None, tb, D_MODEL), blk),
                  pl.BlockSpec((SUBLANES, CONV_COLS), fix2),
                  pl.BlockSpec((SUBLANES, LRU_WIDTH), fix2),
                  pl.BlockSpec((DN_HEADS, DN_HEAD_DIM, DN_HEAD_DIM), fix3)] + w_specs,
        out_specs=[pl.BlockSpec((None, tb, D_MODEL), blk),
                   pl.BlockSpec((None, tb, D_MODEL // 2), blk),
                   pl.BlockSpec((None, SUBLANES, tb), lambda b, i: (b, 0, i)),
                   pl.BlockSpec((None, N_EXPERTS, tb), per_b),
                   pl.BlockSpec((None, SUBLANES, LRU_WIDTH), per_b),
                   pl.BlockSpec((None, DN_HEADS, DN_HEAD_DIM, DN_HEAD_DIM),
                                lambda b, i: (b, 0, 0, 0)),
                   pl.BlockSpec((None, SUBLANES, CONV_COLS), per_b)],
        out_shape=[jax.ShapeDtypeStruct((bsz, t, D_MODEL), F32),
                   jax.ShapeDtypeStruct((bsz, t, D_MODEL // 2), jnp.uint32),
                   jax.ShapeDtypeStruct((bsz, SUBLANES, t), F32),
                   jax.ShapeDtypeStruct((bsz, N_EXPERTS, tb), F32),
                   jax.ShapeDtypeStruct((bsz, SUBLANES, LRU_WIDTH), F32),
                   jax.ShapeDtypeStruct((bsz, DN_HEADS, DN_HEAD_DIM, DN_HEAD_DIM), F32),
                   jax.ShapeDtypeStruct((bsz, SUBLANES, CONV_COLS), F32)],
        scratch_shapes=[pltpu.VMEM((SUBLANES, LRU_WIDTH), F32),
                        pltpu.VMEM((DN_HEADS // 2, 2 * DN_HEAD_DIM, 2 * DN_HEAD_DIM), F32),
                        pltpu.VMEM((tb + SUBLANES, CONV_COLS), F32),
                        pltpu.VMEM((tb, DN_WIDTH), F32),
                        pltpu.VMEM((N_EXPERTS, tb), F32)],
        compiler_params=_cparams("parallel", "arbitrary"),
        name="mixer",
    )(x3, tail8, h0, s0, *weights)


def _expert_kernel(be_ref, x_ref, wg_ref, wu_ref, wd_ref, y_ref, wgu_sc, wd_sc):
    i = pl.program_id(0)
    prev = be_ref[jnp.maximum(i - 1, 0)]

    @pl.when((i == 0) | (be_ref[i] != prev))
    def _():
        wgu_sc[:, :D_EXPERT] = wg_ref[...].astype(BF16)
        wgu_sc[:, D_EXPERT:] = wu_ref[...].astype(BF16)
        wd_sc[...] = wd_ref[...].astype(BF16)

    x = _unpack_bf16_pairs(x_ref[...]).astype(BF16)
    gu = jnp.dot(x, wgu_sc[...], preferred_element_type=F32)
    g, u = gu[:, :D_EXPERT], gu[:, D_EXPERT:]
    hmid = (g * _sigmoid(g)) * u
    y_ref[...] = _pack_bf16_pairs(jnp.dot(hmid.astype(BF16), wd_sc[...],
                                          preferred_element_type=F32))


def _experts(block_expert, x_buf, w_gate, w_up, w_down):
    cap = x_buf.shape[0]
    by_expert = lambda i, be: (be[i], 0, 0)
    grid_spec = pltpu.PrefetchScalarGridSpec(
        num_scalar_prefetch=1,
        grid=(cap // MOE_ROWS,),
        in_specs=[pl.BlockSpec((MOE_ROWS, D_MODEL // 2), lambda i, be: (i, 0)),
                  pl.BlockSpec((None, D_MODEL, D_EXPERT), by_expert),
                  pl.BlockSpec((None, D_MODEL, D_EXPERT), by_expert),
                  pl.BlockSpec((None, D_EXPERT, D_MODEL), by_expert)],
        out_specs=pl.BlockSpec((MOE_ROWS, D_MODEL // 2), lambda i, be: (i, 0)),
        scratch_shapes=[pltpu.VMEM((D_MODEL, 2 * D_EXPERT), BF16),
                        pltpu.VMEM((D_EXPERT, D_MODEL), BF16)],
    )
    return pl.pallas_call(
        _expert_kernel,
        grid_spec=grid_spec,
        out_shape=jax.ShapeDtypeStruct((cap, D_MODEL // 2), jnp.uint32),
        compiler_params=_cparams("arbitrary"),
        name="experts",
    )(block_expert, x_buf, w_gate, w_up, w_down)


def _row_gather(table, idx):
    n_idx = idx.shape[0]
    d = table.shape[1]
    sc = plsc.get_sparse_core_info()
    n_workers = sc.num_cores * sc.num_subcores
    w = GATHER_WINDOW
    per_w = n_idx // n_workers
    n_steps = per_w // w
    assert per_w * n_workers == n_idx and n_steps * w == per_w and n_steps % 2 == 0
    mesh = plsc.VectorSubcoreMesh(core_axis_name="core", subcore_axis_name="subcore")

    @functools.partial(
        pl.kernel, out_type=jax.ShapeDtypeStruct((n_idx, d), table.dtype), mesh=mesh,
        scratch_types=[pltpu.VMEM((per_w,), jnp.int32),
                       pltpu.VMEM((2, w, d), table.dtype),
                       pltpu.SemaphoreType.DMA((2,)),
                       pltpu.SemaphoreType.DMA((2,))])
    def gather(x_hbm, i_hbm, o_hbm, idx_v, rows_v, g_sem, w_sem):
        wid = lax.axis_index("subcore") * sc.num_cores + lax.axis_index("core")
        base = wid * per_w
        pltpu.sync_copy(i_hbm.at[pl.ds(base, per_w)], idx_v)

        def fetch(s, b):
            return pltpu.make_async_copy(x_hbm.at[idx_v.at[pl.ds(s * w, w)]], rows_v.at[b],
                                         g_sem.at[b])

        def flush(s, b):
            return pltpu.make_async_copy(rows_v.at[b], o_hbm.at[pl.ds(base + s * w, w)],
                                         w_sem.at[b])

        fetch(0, 0).start()

        @pl.loop(0, n_steps, step=2)
        def _(s2):
            for b in range(2):
                s = s2 + b

                @pl.when(s + 1 < n_steps)
                def _():
                    @pl.when(s >= 1)
                    def _():
                        flush(s - 1, 1 - b).wait()
                    fetch(s + 1, 1 - b).start()

                fetch(s, b).wait()
                flush(s, b).start()

        flush(n_steps - 2, 0).wait()
        flush(n_steps - 1, 1).wait()

    return gather(table, idx)


def _combine_kernel(h1_ref, y0_ref, y1_ref, gt_ref, fg_ref, o_ref):
    gt = gt_ref[...]
    h = (h1_ref[...] + gt[:, 0:1] * _unpack_bf16_pairs(y0_ref[...])
         + gt[:, 1:2] * _unpack_bf16_pairs(y1_ref[...]))
    ms = jnp.mean(h * h, axis=-1, keepdims=True)
    o_ref[...] = (h * lax.rsqrt(ms + EPS)) * fg_ref[...]


def _combine(h1, y_tok, gates, final_g, tm):
    n = h1.shape[0]
    row = lambda i: (i, 0)
    return pl.pallas_call(
        _combine_kernel,
        grid=(n // tm,),
        in_specs=[pl.BlockSpec((tm, D_MODEL), row),
                  pl.BlockSpec((tm, D_MODEL // 2), row),
                  pl.BlockSpec((tm, D_MODEL // 2), lambda i: (i + n // tm, 0)),
                  pl.BlockSpec((tm, 2), row),
                  pl.BlockSpec((1, D_MODEL), lambda i: (0, 0))],
        out_specs=pl.BlockSpec((tm, D_MODEL), row),
        out_shape=jax.ShapeDtypeStruct((n, D_MODEL), F32),
        compiler_params=_cparams("parallel"),
        name="combine",
    )(h1, y_tok, y_tok, gates, final_g)


def _block_diag(blocks):
    n, r, c = blocks.shape
    out = jnp.zeros((n * r, n * c), blocks.dtype)
    for i in range(n):
        out = out.at[i * r:(i + 1) * r, i * c:(i + 1) * c].set(blocks[i])
    return out


def _pad_rows(a, rows):
    return jnp.pad(a, ((0, rows - a.shape[0]), (0, 0)))


def _slot_kernel(off_ref, e_ref, o_ref):
    e = e_ref[...]
    rows, lanes = e.shape
    j = (lax.broadcasted_iota(jnp.int32, e.shape, 0) * lanes
         + lax.broadcasted_iota(jnp.int32, e.shape, 1))
    off = jnp.zeros_like(e)
    for k in range(N_EXPERTS):
        off = jnp.where(e == k, off_ref[k], off)
    o_ref[...] = j + off


def _sorted_slots(sorted_e, offset):
    rows = sorted_e.shape[0] // LANES
    grid_spec = pltpu.PrefetchScalarGridSpec(
        num_scalar_prefetch=1, grid=(1,),
        in_specs=[pl.BlockSpec((rows, LANES), lambda i, off: (0, 0))],
        out_specs=pl.BlockSpec((rows, LANES), lambda i, off: (0, 0)))
    return pl.pallas_call(
        _slot_kernel, grid_spec=grid_spec,
        out_shape=jax.ShapeDtypeStruct((rows, LANES), jnp.int32),
        compiler_params=_cparams("arbitrary"), name="sorted_slots",
    )(offset, sorted_e.reshape(rows, LANES)).reshape(-1)


def _dispatch_tables(eflat, counts, n):
    n_assign = 2 * n
    iota = jnp.arange(n_assign, dtype=jnp.int32)
    sorted_e, order = lax.sort((eflat, iota), num_keys=1, is_stable=True)
    starts = jnp.cumsum(counts) - counts
    padded = (counts + MOE_ROWS - 1) // MOE_ROWS * MOE_ROWS
    pend = jnp.cumsum(padded)
    pstart = pend - padded
    dest_sorted = _sorted_slots(sorted_e, pstart - starts)
    _, dest = lax.sort((order, dest_sorted), num_keys=1, is_stable=False)
    n_blocks = -(-(n_assign + N_EXPERTS * (MOE_ROWS - 1)) // MOE_ROWS)
    block_start = jnp.arange(n_blocks, dtype=jnp.int32) * MOE_ROWS
    block_expert = jnp.minimum(jnp.sum(block_start[:, None] >= pend[None, :], axis=1),
                               N_EXPERTS - 1).astype(jnp.int32)
    in_e = (block_start - pstart[block_expert])[:, None] + jnp.arange(MOE_ROWS, dtype=jnp.int32)
    first = jnp.clip(block_start - pstart[block_expert] + starts[block_expert], 0, n_assign)
    order_pad = jnp.concatenate([order, jnp.zeros((MOE_ROWS,), jnp.int32)])
    runs = jax.vmap(lambda s: lax.dynamic_slice(order_pad, (s,), (MOE_ROWS,)))(first)
    slot = block_start[:, None] + jnp.arange(MOE_ROWS, dtype=jnp.int32)
    buf_tok = jnp.where(in_e < counts[block_expert][:, None], runs, slot) % n
    return buf_tok.reshape(-1), dest, block_expert


def kernel(x, meta_tokens, mix_norm, w_in, lru_conv_w, lru_conv_b, lru_w_r, lru_b_r, lru_w_i,
           lru_b_i, lru_lambda, lru_out_norm, dn_conv_w, dn_a_log, dn_dt_bias, dn_out_norm, w_out,
           ffn_norm, router_group_w, router_group_b, router_expert_w, router_expert_b, moe_w_gate,
           moe_w_up, moe_w_down, final_norm):
    bsz, seq, d = x.shape
    n = bsz * seq
    l = 0

    lw = LRU_WIDTH
    w = w_in[l]
    w_main = jnp.concatenate([w[:, :lw], w[:, 2 * lw:2 * lw + 3 * DN_WIDTH], w[:, lw:2 * lw],
                              w[:, 2 * lw + 3 * DN_WIDTH:2 * lw + 4 * DN_WIDTH]], axis=1).astype(BF16)
    w_ba = jnp.pad(w[:, 2 * lw + 4 * DN_WIDTH:], ((0, 0), (0, LANES - 2 * DN_HEADS))).astype(BF16)
    mix_g = mix_norm[l][None, :]
    conv_w = _pad_rows(jnp.concatenate([lru_conv_w[l], dn_conv_w[l]], axis=1), SUBLANES)
    lru_p = _pad_rows(jnp.stack([lru_conv_b[l], lru_b_r[l], lru_b_i[l], lru_lambda[l],
                                 lru_out_norm[l]]), SUBLANES)
    hb = LRU_BLOCKS // 2
    w_gate = jnp.stack([
        jnp.concatenate([_block_diag(lru_w_r[l][h * hb:(h + 1) * hb]),
                         _block_diag(lru_w_i[l][h * hb:(h + 1) * hb])], axis=1)
        for h in range(2)]).astype(BF16)
    head_p = jnp.zeros((SUBLANES, LANES), F32)
    head_p = head_p.at[0, DN_HEADS:2 * DN_HEADS].set(dn_a_log[l])
    head_p = head_p.at[1, DN_HEADS:2 * DN_HEADS].set(dn_dt_bias[l])
    dn_on = dn_out_norm[l][None, :]
    w_o = w_out[l].astype(BF16)
    ffn_g = ffn_norm[l][None, :]
    w_router = jnp.zeros((ROUTE_ROWS, D_MODEL), F32)
    w_router = w_router.at[:N_GROUPS].set(router_group_w[l].T)
    w_router = w_router.at[SUBLANES:SUBLANES + N_EXPERTS].set(router_expert_w[l].T).astype(BF16)
    b_router = jnp.zeros((ROUTE_ROWS,), F32).at[:N_GROUPS].set(router_group_b[l])
    b_router = b_router.at[SUBLANES:SUBLANES + N_EXPERTS].set(router_expert_b[l])
    b_router = jnp.broadcast_to(b_router[:, None], (ROUTE_ROWS, LANES))
    final_g = final_norm[None, :]
    weights = (mix_g, w_main, w_ba, conv_w, lru_p, w_gate, head_p, dn_on, w_o, ffn_g, w_router,
               b_router)

    meta_pad = CHUNK - N_META
    prefix = jnp.pad(meta_tokens, ((meta_pad, 0), (0, 0)))[None]
    zeros = lambda *s: jnp.zeros(s, F32)
    *_, h_meta, s_meta, tail_meta = _mixer(
        prefix, zeros(SUBLANES, CONV_COLS), zeros(SUBLANES, LRU_WIDTH),
        zeros(DN_HEADS, DN_HEAD_DIM, DN_HEAD_DIM), weights, CHUNK, meta_pad)

    h1, xn, route, cnt, _, _, _ = _mixer(x, tail_meta[0], h_meta[0], s_meta[0], weights, 256, 0)
    h1 = h1.reshape(n, d)
    eflat = jnp.transpose(route[:, 0:2, :], (1, 0, 2)).reshape(2 * n).astype(jnp.int32)
    gates = jnp.transpose(route[:, 2:4, :], (0, 2, 1)).reshape(n, 2)
    counts = jnp.sum(cnt, axis=(0, 2)).astype(jnp.int32)

    buf_tok, dest, block_expert = _dispatch_tables(eflat, counts, n)
    x_buf = _row_gather(xn.reshape(n, d // 2), buf_tok)
    y_buf = _experts(block_expert, x_buf, moe_w_gate[l], moe_w_up[l], moe_w_down[l])
    y_tok = _row_gather(y_buf, dest)
    out = _combine(h1, y_tok, gates, final_g, 512)
    return out.reshape(bsz, seq, d)
```

```python
import functools

import jax
import jax.numpy as jnp
from jax import lax
from jax.experimental import pallas as pl
from jax.experimental.pallas import tpu as pltpu
from jax.experimental.pallas import tpu_sc as plsc

F32 = jnp.float32
BF16 = jnp.bfloat16

D_MODEL = 1024
N_META = 16
CHUNK = 64
CONV_WIDTH = 4
LRU_WIDTH = 512
LRU_BLOCKS = 8
LRU_C = 8.0
DN_HEADS = 4
DN_HEAD_DIM = 128
DN_WIDTH = DN_HEADS * DN_HEAD_DIM
N_GROUPS = 4
EXPERTS_PER_GROUP = 8
N_EXPERTS = N_GROUPS * EXPERTS_PER_GROUP
D_EXPERT = 256
EPS = 1e-6
CONV_COLS = LRU_WIDTH + 3 * DN_WIDTH
GATE_COLS = LRU_WIDTH + DN_WIDTH
LANES = 128
SUBLANES = 8
INV_BLOCK = 16
MOE_ROWS = 512
ROUTE_ROWS = 48
VMEM_LIMIT = 56 * 1024 * 1024
GATHER_WINDOW = 64


def _cparams(*sem):
    return pltpu.CompilerParams(dimension_semantics=sem, vmem_limit_bytes=VMEM_LIMIT)


def _sigmoid(x):
    return 0.5 * jnp.tanh(0.5 * x) + 0.5


def _pack_bf16_pairs(x):
    c = x.shape[1] // 2

    def rne(v):
        b = lax.bitcast_convert_type(v, jnp.uint32)
        return b + jnp.uint32(0x7FFF) + ((b >> 16) & jnp.uint32(1))

    return (rne(x[:, :c]) >> 16) | (rne(x[:, c:]) & jnp.uint32(0xFFFF0000))


def _unpack_bf16_pairs(w):
    lo = lax.bitcast_convert_type(w << 16, F32)
    hi = lax.bitcast_convert_type(w & jnp.uint32(0xFFFF0000), F32)
    return jnp.concatenate([lo, hi], axis=1)


def _softplus(x):
    return jnp.maximum(x, 0.0) + jnp.log(1.0 + jnp.exp(-jnp.abs(x)))


def _gelu_tanh(x):
    return 0.5 * x * (1.0 + jnp.tanh(0.7978845608028654 * (x + 0.044715 * (x * x * x))))


def _mm(a, b):
    return jnp.dot(a.astype(BF16), b.astype(BF16), preferred_element_type=F32)


def _mm_nt(a, b):
    return lax.dot_general(a.astype(BF16), b.astype(BF16), (((1,), (1,)), ((), ())),
                           preferred_element_type=F32)


def _mm_tn(a, b):
    return lax.dot_general(a.astype(BF16), b.astype(BF16), (((0,), (0,)), ((), ())),
                           preferred_element_type=F32)


def _lru_branch(xc, gate, p_ref, wg_ref, h_sc, valid):
    tb = xc.shape[0]
    b_r, b_i = p_ref[1:2, :], p_ref[2:3, :]
    lam, out_g = p_ref[3:4, :], p_ref[4:5, :]
    half = LRU_WIDTH // 2
    xcb = xc.astype(BF16)
    g0 = jnp.dot(xcb[:, :half], wg_ref[0], preferred_element_type=F32)
    g1 = jnp.dot(xcb[:, half:], wg_ref[1], preferred_element_type=F32)
    r = _sigmoid(jnp.concatenate([g0[:, :half], g1[:, :half]], axis=1) + b_r)
    i = _sigmoid(jnp.concatenate([g0[:, half:], g1[:, half:]], axis=1) + b_i)
    log_a = (-LRU_C) * r * _softplus(-lam)
    a = jnp.exp(log_a)
    var = 1.0 - jnp.exp(2.0 * log_a)
    b = jnp.where(var > 0.0, var * lax.rsqrt(var), 0.0) * (i * xc)
    if valid is not None:
        b = jnp.where(valid, b, 0.0)
        a = jnp.where(valid, a, 1.0)

    sub = lax.broadcasted_iota(jnp.int32, (tb, LRU_WIDTH), 0) & (SUBLANES - 1)
    for s in (1, 2, 4):
        keep = sub >= s
        b = jnp.where(keep, a * pltpu.roll(b, s, axis=0), 0.0) + b
        a = jnp.where(keep, a * pltpu.roll(a, s, axis=0), a)
    h = h_sc[...]
    hs = []
    for g in range(tb // SUBLANES):
        hg = a[g * SUBLANES:(g + 1) * SUBLANES] * h + b[g * SUBLANES:(g + 1) * SUBLANES]
        hs.append(hg)
        h = jnp.broadcast_to(hg[SUBLANES - 1:SUBLANES, :], (SUBLANES, LRU_WIDTH))
    h_sc[...] = h
    out = jnp.concatenate(hs, axis=0) * _gelu_tanh(gate)
    ms = jnp.mean(out * out, axis=-1, keepdims=True)
    return (out * lax.rsqrt(ms + EPS)) * out_g


def _pair_mm(x, y, half_masks):
    yb = y.astype(BF16)
    rhs = jnp.concatenate([yb * half_masks[0], yb * half_masks[1]], axis=0)
    return jnp.dot(x.astype(BF16), rhs, preferred_element_type=F32)


def _pair_inverse(a_list, eye, same16, half_masks):
    def mm(xs, ys):
        return [_pair_mm(x, y, half_masks) for x, y in zip(xs, ys)]

    def plus(xs):
        return [eye + x for x in xs]

    def minus(xs):
        return [eye - x for x in xs]

    d = [jnp.where(same16, a, 0.0) for a in a_list]
    d2 = mm(d, d)
    d4 = mm(d2, d2)
    d8 = mm(d4, d4)
    p = mm(mm(mm(minus(d), plus(d2)), plus(d4)), plus(d8))
    m = mm(p, [a - x for a, x in zip(a_list, d)])
    q = mm(minus(m), plus(mm(m, m)))
    return mm(q, p)


def _dn_branch(act, z, ba, hp_ref, on_ref, s_sc, y_sc, valid):
    tb = act.shape[0]
    hd = DN_HEAD_DIM
    act = act * _sigmoid(act)
    beta_t = _sigmoid(ba)
    g_t = pltpu.roll(-jnp.exp(hp_ref[0:1, :]) * _softplus(ba + hp_ref[1:2, :]),
                     LANES - DN_HEADS, axis=1)
    if valid is not None:
        act = jnp.where(valid, act, 0.0)
        beta_t = jnp.where(valid, beta_t, 0.0)
        g_t = jnp.where(valid, g_t, 0.0)

    heads = []
    for i in range(3 * DN_HEADS):
        seg = act[:, i * hd:(i + 1) * hd]
        if i < 2 * DN_HEADS:
            nrm = lax.rsqrt(jnp.sum(seg * seg, axis=-1, keepdims=True) + EPS)
            if i < DN_HEADS:
                nrm = nrm * (hd ** -0.5)
            seg = seg * nrm
        heads.append(seg)
    qn, kn, vv = heads[:DN_HEADS], heads[DN_HEADS:2 * DN_HEADS], heads[2 * DN_HEADS:]

    row_c = lax.broadcasted_iota(jnp.int32, (tb, LANES), 0) & (CHUNK - 1)
    cum = g_t
    s = 1
    while s < CHUNK:
        cum = cum + jnp.where(row_c >= s, pltpu.roll(cum, s, axis=0), 0.0)
        s *= 2
    e_cum = jnp.exp(cum)

    ri = lax.broadcasted_iota(jnp.int32, (CHUNK, LANES), 0)
    li = lax.broadcasted_iota(jnp.int32, (CHUNK, LANES), 1)
    lo_half = li < CHUNK
    half_masks = (lo_half.astype(BF16), (li >= CHUNK).astype(BF16))
    cj = li & (CHUNK - 1)
    eye_b = ri == cj
    eye = eye_b.astype(F32)
    causal = ri >= cj
    strict = ri > cj
    same16 = (ri // INV_BLOCK) == (cj // INV_BLOCK)
    bd_mask = ((lax.broadcasted_iota(jnp.int32, (2 * hd, 2 * hd), 0) >= hd)
               == (lax.broadcasted_iota(jnp.int32, (2 * hd, 2 * hd), 1) >= hd))
    zero_c = jnp.zeros((CHUNK, hd), F32)
    out_g = on_ref[...]

    def bd_rows(x0, x1):
        z0 = jnp.zeros_like(x0)
        return jnp.concatenate([jnp.concatenate([x0, z0], axis=1),
                                jnp.concatenate([z0, x1], axis=1)], axis=0)

    n_ch = tb // CHUNK
    n_hp = DN_HEADS // 2
    probs = [(ch, hp) for ch in range(n_ch) for hp in range(n_hp)]
    qkm, q_dec, k_dec, rhs, a_list, dec_row = [], [], [], [], [], []
    for ch, hp in probs:
        rows = slice(ch * CHUNK, (ch + 1) * CHUNK)
        h0, h1 = 2 * hp, 2 * hp + 1
        cum_c, beta_c, ecum_c = cum[rows], beta_t[rows], e_cum[rows]
        last = cum_c[CHUNK - 1:CHUNK, :]

        def tile(arr):
            return jnp.where(lo_half, jnp.broadcast_to(arr[:, h0:h0 + 1], (CHUNK, LANES)),
                             jnp.broadcast_to(arr[:, h1:h1 + 1], (CHUNK, LANES)))

        def wide(arr):
            return jnp.concatenate([jnp.broadcast_to(arr[:, h0:h0 + 1], (arr.shape[0], hd)),
                                    jnp.broadcast_to(arr[:, h1:h1 + 1], (arr.shape[0], hd))],
                                   axis=1)

        cum_cp = tile(cum_c)
        cum_rp = jnp.sum(jnp.where(eye_b, cum_cp, 0.0), axis=0, keepdims=True)
        decay = jnp.where(causal, jnp.exp(jnp.where(causal, cum_cp - cum_rp, 0.0)), 0.0)
        q_p = jnp.concatenate([qn[h0][rows], qn[h1][rows]], axis=1)
        k_p = jnp.concatenate([kn[h0][rows], kn[h1][rows]], axis=1)
        v_p = jnp.concatenate([vv[h0][rows], vv[h1][rows]], axis=1)
        qkk = _mm_nt(jnp.concatenate([q_p, k_p], axis=0),
                     bd_rows(kn[h0][rows], kn[h1][rows]))
        qkm.append(qkk[:CHUNK] * decay)
        a_list.append(jnp.where(strict, qkk[CHUNK:] * decay, 0.0) * tile(beta_c))
        vb = v_p * wide(beta_c)
        kb = k_p * wide(beta_c * ecum_c)
        rhs.append(jnp.concatenate(
            [jnp.concatenate([vb[:, :hd], zero_c, kb[:, :hd], zero_c], axis=1),
             jnp.concatenate([zero_c, vb[:, hd:], zero_c, kb[:, hd:]], axis=1)], axis=0))
        q_dec.append(q_p * wide(ecum_c))
        k_dec.append(k_p * wide(jnp.exp(last - cum_c)))
        dec_row.append(wide(jnp.exp(last)))

    t_inv = _pair_inverse(a_list, eye, same16, half_masks)
    sol = [_mm(ti, r) for ti, r in zip(t_inv, rhs)]
    ktuw = [_mm_tn(kd, so) for kd, so in zip(k_dec, sol)]
    quw = [_mm(qm, jnp.concatenate(
               [jnp.concatenate([so[:, :hd], zero_c, so[:, 2 * hd:3 * hd], zero_c], axis=1),
                jnp.concatenate([zero_c, so[:, hd:2 * hd], zero_c, so[:, 3 * hd:]], axis=1)], axis=0))
           for qm, so in zip(qkm, sol)]

    state = [s_sc[hp] for hp in range(n_hp)]
    for ch in range(n_ch):
        rows = slice(ch * CHUNK, (ch + 1) * CHUNK)
        res = []
        for hp in range(n_hp):
            i = ch * n_hp + hp
            k_w = jnp.where(bd_mask, ktuw[i][:, 2 * hd:], 0.0)
            q_eff = q_dec[i] - quw[i][:, 2 * hd:]
            res.append(_mm(jnp.concatenate([k_w, q_eff], axis=0), state[hp]))
        for hp in range(n_hp):
            i = ch * n_hp + hp
            o_p = res[hp][2 * hd:] + quw[i][:, :2 * hd]
            state[hp] = (state[hp] * dec_row[i] - res[hp][:2 * hd]
                         + jnp.where(bd_mask, ktuw[i][:, :2 * hd], 0.0))
            for e in range(2):
                lo = (2 * hp + e) * hd
                o = o_p[:, e * hd:(e + 1) * hd]
                ms = jnp.mean(o * o, axis=-1, keepdims=True)
                zz = z[rows, lo:lo + hd]
                y_sc[rows, lo:lo + hd] = ((o * lax.rsqrt(ms + EPS)) * out_g) * (zz * _sigmoid(zz))
    for hp in range(n_hp):
        s_sc[hp] = state[hp]


def _route(logits_t):
    tm = logits_t.shape[1]
    row = lax.broadcasted_iota(jnp.int32, (SUBLANES, tm), 0).astype(F32)
    neg = -jnp.inf
    big = 1e9
    lg = jnp.where(row < N_GROUPS, logits_t[0:SUBLANES], neg)
    mg = jnp.max(lg, axis=0, keepdims=True)
    p_sel = 1.0 / jnp.sum(jnp.exp(lg - mg), axis=0, keepdims=True)
    g_sel = jnp.min(jnp.where(lg == mg, row, big), axis=0, keepdims=True)
    le = logits_t[SUBLANES:2 * SUBLANES]
    for g in range(1, N_GROUPS):
        le = jnp.where(g_sel == float(g), logits_t[(g + 1) * SUBLANES:(g + 2) * SUBLANES], le)
    m1 = jnp.max(le, axis=0, keepdims=True)
    i1 = jnp.min(jnp.where(le == m1, row, big), axis=0, keepdims=True)
    le2 = jnp.where(row == i1, neg, le)
    m2 = jnp.max(le2, axis=0, keepdims=True)
    i2 = jnp.min(jnp.where(le2 == m2, row, big), axis=0, keepdims=True)
    e21 = jnp.exp(m2 - m1)
    w1 = p_sel / (1.0 + e21)
    w2 = p_sel * e21 / (1.0 + e21)
    base = g_sel * float(EXPERTS_PER_GROUP)
    return jnp.where(row == 0.0, base + i1,
                     jnp.where(row == 1.0, base + i2,
                               jnp.where(row == 2.0, w1, jnp.where(row == 3.0, w2, 0.0))))


def _mixer_kernel(x_ref, tail_ref, h0_ref, s0_ref, mg_ref, wm_ref, wba_ref, cw_ref, lp_ref, wg_ref,
                  hp_ref, on_ref, wo_ref, fg_ref, wr_ref, br_ref,
                  h1_ref, xn_ref, rt_ref, cn_ref, hl_ref, sl_ref, tl_ref,
                  h_sc, s_sc, cs_sc, y_sc, cnt_sc, *, tb, pad):
    t = pl.program_id(1)
    hd = DN_HEAD_DIM

    @pl.when(t == 0)
    def _():
        zero_hd = jnp.zeros((hd, hd), F32)
        h_sc[...] = h0_ref[...]
        for hp in range(DN_HEADS // 2):
            s_sc[hp] = jnp.concatenate(
                [jnp.concatenate([s0_ref[2 * hp], zero_hd], axis=1),
                 jnp.concatenate([zero_hd, s0_ref[2 * hp + 1]], axis=1)], axis=0)
        cs_sc[0:SUBLANES, :] = tail_ref[...]

    x = x_ref[...]
    ms = jnp.mean(x * x, axis=-1, keepdims=True)
    u = ((x * lax.rsqrt(ms + EPS)) * mg_ref[...]).astype(BF16)
    valid = None
    if pad:
        valid = lax.broadcasted_iota(jnp.int32, (tb, 1), 0) >= pad

    for n in range(0, CONV_COLS, 512):
        cs_sc[SUBLANES:SUBLANES + tb, n:n + 512] = jnp.dot(u, wm_ref[:, n:n + 512],
                                                           preferred_element_type=F32)
    gate = jnp.dot(u, wm_ref[:, CONV_COLS:CONV_COLS + LRU_WIDTH], preferred_element_type=F32)
    z = jnp.dot(u, wm_ref[:, CONV_COLS + LRU_WIDTH:], preferred_element_type=F32)
    ba = jnp.dot(u, wba_ref[...], preferred_element_type=F32)

    def conv(lo, width):
        acc = cs_sc[SUBLANES:SUBLANES + tb, lo:lo + width] * cw_ref[3:4, lo:lo + width]
        for d in range(1, CONV_WIDTH):
            acc = acc + (cs_sc[SUBLANES - d:SUBLANES - d + tb, lo:lo + width]
                         * cw_ref[3 - d:4 - d, lo:lo + width])
        return acc

    xc = conv(0, LRU_WIDTH) + lp_ref[0:1, :]
    act = conv(LRU_WIDTH, 3 * DN_WIDTH)
    cs_sc[0:SUBLANES, :] = cs_sc[tb:tb + SUBLANES, :]

    y_lru = _lru_branch(xc, gate, lp_ref, wg_ref, h_sc, valid)
    _dn_branch(act, z, ba, hp_ref, on_ref, s_sc, y_sc, valid)

    mix = jnp.dot(y_lru.astype(BF16), wo_ref[:LRU_WIDTH, :], preferred_element_type=F32)
    mix = mix + jnp.dot(y_sc[...].astype(BF16), wo_ref[LRU_WIDTH:, :], preferred_element_type=F32)
    h1 = x + mix
    h1_ref[...] = h1
    ms1 = jnp.mean(h1 * h1, axis=-1, keepdims=True)
    xn = (h1 * lax.rsqrt(ms1 + EPS)) * fg_ref[...]
    xn_ref[...] = _pack_bf16_pairs(xn)
    logits_t = lax.dot_general(wr_ref[...], xn.astype(BF16), (((1,), (1,)), ((), ())),
                               preferred_element_type=F32) + br_ref[:, 0:1]
    route = _route(logits_t)
    rt_ref[...] = route

    erow = lax.broadcasted_iota(jnp.int32, (N_EXPERTS, tb), 0).astype(F32)
    hist = (erow == route[0:1, :]).astype(F32) + (erow == route[1:2, :]).astype(F32)
    if pad:
        hist = jnp.where(lax.broadcasted_iota(jnp.int32, (1, tb), 1) >= pad, hist, 0.0)

    @pl.when(t == 0)
    def _():
        cnt_sc[...] = hist

    @pl.when(t > 0)
    def _():
        cnt_sc[...] = cnt_sc[...] + hist

    @pl.when(t == pl.num_programs(1) - 1)
    def _():
        cn_ref[...] = cnt_sc[...]
        hl_ref[...] = h_sc[...]
        for h in range(DN_HEADS):
            e = h % 2
            sl_ref[h] = s_sc[h // 2][e * hd:(e + 1) * hd, e * hd:(e + 1) * hd]
        tl_ref[...] = cs_sc[0:SUBLANES, :]


def _mixer(x3, tail8, h0, s0, weights, tb, pad):
    bsz, t, _ = x3.shape
    assert tb % CHUNK == 0 and t % tb == 0 and (pad == 0 or t == tb)
    fix2 = lambda b, i: (0, 0)
    fix3 = lambda b, i: (0, 0, 0)
    blk = lambda b, i: (b, i, 0)
    per_b = lambda b, i: (b, 0, 0)
    w_specs = [pl.BlockSpec(w.shape, fix2 if w.ndim == 2 else fix3) for w in weights]
    return pl.pallas_call(
        functools.partial(_mixer_kernel, tb=tb, pad=pad),
        grid=(bsz, t // tb),
        in_specs=[pl.BlockSpec((None, tb, D_MODEL), blk),
                  pl.BlockSpec((SUBLANES, CONV_COLS), fix2),
                  pl.BlockSpec((SUBLANES, LRU_WIDTH), fix2),
                  pl.BlockSpec((DN_HEADS, DN_HEAD_DIM, DN_HEAD_DIM), fix3)] + w_specs,
        out_specs=[pl.BlockSpec((None, tb, D_MODEL), blk),
                   pl.BlockSpec((None, tb, D_MODEL // 2), blk),
                   pl.BlockSpec((None, SUBLANES, tb), lambda b, i: (b, 0, i)),
                   pl.BlockSpec((None, N_EXPERTS, tb), per_b),
                   pl.BlockSpec((None, SUBLANES, LRU_WIDTH), per_b),
                   pl.BlockSpec((None, DN_HEADS, DN_HEAD_DIM, DN_HEAD_DIM),
                                lambda b, i: (b, 0, 0, 0)),
                   pl.BlockSpec((None, SUBLANES, CONV_COLS), per_b)],
        out_shape=[jax.ShapeDtypeStruct((bsz, t, D_MODEL), F32),
                   jax.ShapeDtypeStruct((bsz, t, D_MODEL // 2), jnp.uint32),
                   jax.ShapeDtypeStruct((bsz, SUBLANES, t), F32),
                   jax.ShapeDtypeStruct((bsz, N_EXPERTS, tb), F32),
                   jax.ShapeDtypeStruct((bsz, SUBLANES, LRU_WIDTH), F32),
                   jax.ShapeDtypeStruct((bsz, DN_HEADS, DN_HEAD_DIM, DN_HEAD_DIM), F32),
                   jax.ShapeDtypeStruct((bsz, SUBLANES, CONV_COLS), F32)],
        scratch_shapes=[pltpu.VMEM((SUBLANES, LRU_WIDTH), F32),
                        pltpu.VMEM((DN_HEADS // 2, 2 * DN_HEAD_DIM, 2 * DN_HEAD_DIM), F32),
                        pltpu.VMEM((tb + SUBLANES, CONV_COLS), F32),
                        pltpu.VMEM((tb, DN_WIDTH), F32),
                        pltpu.VMEM((N_EXPERTS, tb), F32)],
        compiler_params=_cparams("parallel", "arbitrary"),
        name="mixer",
    )(x3, tail8, h0, s0, *weights)


def _expert_kernel(be_ref, x_ref, wg_ref, wu_ref, wd_ref, y_ref, wgu_sc, wd_sc):
    i = pl.program_id(0)
    prev = be_ref[jnp.maximum(i - 1, 0)]

    @pl.when((i == 0) | (be_ref[i] != prev))
    def _():
        wgu_sc[:, :D_EXPERT] = wg_ref[...].astype(BF16)
        wgu_sc[:, D_EXPERT:] = wu_ref[...].astype(BF16)
        wd_sc[...] = wd_ref[...].astype(BF16)

    x = _unpack_bf16_pairs(x_ref[...]).astype(BF16)
    gu = jnp.dot(x, wgu_sc[...], preferred_element_type=F32)
    g, u = gu[:, :D_EXPERT], gu[:, D_EXPERT:]
    hmid = (g * _sigmoid(g)) * u
    y_ref[...] = _pack_bf16_pairs(jnp.dot(hmid.astype(BF16), wd_sc[...],
                                          preferred_element_type=F32))


def _experts(block_expert, x_buf, w_gate, w_up, w_down):
    cap = x_buf.shape[0]
    by_expert = lambda i, be: (be[i], 0, 0)
    grid_spec = pltpu.PrefetchScalarGridSpec(
        num_scalar_prefetch=1,
        grid=(cap // MOE_ROWS,),
        in_specs=[pl.BlockSpec((MOE_ROWS, D_MODEL // 2), lambda i, be: (i, 0)),
                  pl.BlockSpec((None, D_MODEL, D_EXPERT), by_expert),
                  pl.BlockSpec((None, D_MODEL, D_EXPERT), by_expert),
                  pl.BlockSpec((None, D_EXPERT, D_MODEL), by_expert)],
        out_specs=pl.BlockSpec((MOE_ROWS, D_MODEL // 2), lambda i, be: (i, 0)),
        scratch_shapes=[pltpu.VMEM((D_MODEL, 2 * D_EXPERT), BF16),
                        pltpu.VMEM((D_EXPERT, D_MODEL), BF16)],
    )
    return pl.pallas_call(
        _expert_kernel,
        grid_spec=grid_spec,
        out_shape=jax.ShapeDtypeStruct((cap, D_MODEL // 2), jnp.uint32),
        compiler_params=_cparams("arbitrary"),
        name="experts",
    )(block_expert, x_buf, w_gate, w_up, w_down)


def _row_gather(table, idx):
    n_idx = idx.shape[0]
    d = table.shape[1]
    sc = plsc.get_sparse_core_info()
    n_workers = sc.num_cores * sc.num_subcores
    w = GATHER_WINDOW
    per_w = n_idx // n_workers
    n_steps = per_w // w
    assert per_w * n_workers == n_idx and n_steps * w == per_w and n_steps % 2 == 0
    mesh = plsc.VectorSubcoreMesh(core_axis_name="core", subcore_axis_name="subcore")

    @functools.partial(
        pl.kernel, out_type=jax.ShapeDtypeStruct((n_idx, d), table.dtype), mesh=mesh,
        scratch_types=[pltpu.VMEM((per_w,), jnp.int32),
                       pltpu.VMEM((2, w, d), table.dtype),
                       pltpu.SemaphoreType.DMA((2,)),
                       pltpu.SemaphoreType.DMA((2,))])
    def gather(x_hbm, i_hbm, o_hbm, idx_v, rows_v, g_sem, w_sem):
        wid = lax.axis_index("subcore") * sc.num_cores + lax.axis_index("core")
        base = wid * per_w
        pltpu.sync_copy(i_hbm.at[pl.ds(base, per_w)], idx_v)

        def fetch(s, b):
            return pltpu.make_async_copy(x_hbm.at[idx_v.at[pl.ds(s * w, w)]], rows_v.at[b],
                                         g_sem.at[b])

        def flush(s, b):
            return pltpu.make_async_copy(rows_v.at[b], o_hbm.at[pl.ds(base + s * w, w)],
                                         w_sem.at[b])

        fetch(0, 0).start()

        @pl.loop(0, n_steps, step=2)
        def _(s2):
            for b in range(2):
                s = s2 + b

                @pl.when(s + 1 < n_steps)
                def _():
                    @pl.when(s >= 1)
                    def _():
                        flush(s - 1, 1 - b).wait()
                    fetch(s + 1, 1 - b).start()

                fetch(s, b).wait()
                flush(s, b).start()

        flush(n_steps - 2, 0).wait()
        flush(n_steps - 1, 1).wait()

    return gather(table, idx)


def _combine_kernel(h1_ref, y0_ref, y1_ref, gt_ref, fg_ref, o_ref):
    gt = gt_ref[...]
    h = (h1_ref[...] + gt[:, 0:1] * _unpack_bf16_pairs(y0_ref[...])
         + gt[:, 1:2] * _unpack_bf16_pairs(y1_ref[...]))
    ms = jnp.mean(h * h, axis=-1, keepdims=True)
    o_ref[...] = (h * lax.rsqrt(ms + EPS)) * fg_ref[...]


def _combine(h1, y_tok, gates, final_g, tm):
    n = h1.shape[0]
    row = lambda i: (i, 0)
    return pl.pallas_call(
        _combine_kernel,
        grid=(n // tm,),
        in_specs=[pl.BlockSpec((tm, D_MODEL), row),
                  pl.BlockSpec((tm, D_MODEL // 2), row),
                  pl.BlockSpec((tm, D_MODEL // 2), lambda i: (i + n // tm, 0)),
                  pl.BlockSpec((tm, 2), row),
                  pl.BlockSpec((1, D_MODEL), lambda i: (0, 0))],
        out_specs=pl.BlockSpec((tm, D_MODEL), row),
        out_shape=jax.ShapeDtypeStruct((n, D_MODEL), F32),
        compiler_params=_cparams("parallel"),
        name="combine",
    )(h1, y_tok, y_tok, gates, final_g)


def _block_diag(blocks):
    n, r, c = blocks.shape
    out = jnp.zeros((n * r, n * c), blocks.dtype)
    for i in range(n):
        out = out.at[i * r:(i + 1) * r, i * c:(i + 1) * c].set(blocks[i])
    return out


def _pad_rows(a, rows):
    return jnp.pad(a, ((0, rows - a.shape[0]), (0, 0)))


def _slot_kernel(off_ref, e_ref, o_ref):
    e = e_ref[...]
    rows, lanes = e.shape
    j = (lax.broadcasted_iota(jnp.int32, e.shape, 0) * lanes
         + lax.broadcasted_iota(jnp.int32, e.shape, 1))
    off = jnp.zeros_like(e)
    for k in range(N_EXPERTS):
        off = jnp.where(e == k, off_ref[k], off)
    o_ref[...] = j + off


def _sorted_slots(sorted_e, offset):
    rows = sorted_e.shape[0] // LANES
    grid_spec = pltpu.PrefetchScalarGridSpec(
        num_scalar_prefetch=1, grid=(1,),
        in_specs=[pl.BlockSpec((rows, LANES), lambda i, off: (0, 0))],
        out_specs=pl.BlockSpec((rows, LANES), lambda i, off: (0, 0)))
    return pl.pallas_call(
        _slot_kernel, grid_spec=grid_spec,
        out_shape=jax.ShapeDtypeStruct((rows, LANES), jnp.int32),
        compiler_params=_cparams("arbitrary"), name="sorted_slots",
    )(offset, sorted_e.reshape(rows, LANES)).reshape(-1)


def _dispatch_tables(eflat, counts, n):
    n_assign = 2 * n
    iota = jnp.arange(n_assign, dtype=jnp.int32)
    sorted_e, order = lax.sort((eflat, iota), num_keys=1, is_stable=True)
    starts = jnp.cumsum(counts) - counts
    padded = (counts + MOE_ROWS - 1) // MOE_ROWS * MOE_ROWS
    pend = jnp.cumsum(padded)
    pstart = pend - padded
    dest_sorted = _sorted_slots(sorted_e, pstart - starts)
    _, dest = lax.sort((order, dest_sorted), num_keys=1, is_stable=False)
    n_blocks = -(-(n_assign + N_EXPERTS * (MOE_ROWS - 1)) // MOE_ROWS)
    block_start = jnp.arange(n_blocks, dtype=jnp.int32) * MOE_ROWS
    block_expert = jnp.minimum(jnp.sum(block_start[:, None] >= pend[None, :], axis=1),
                               N_EXPERTS - 1).astype(jnp.int32)
    in_e = (block_start - pstart[block_expert])[:, None] + jnp.arange(MOE_ROWS, dtype=jnp.int32)
    runs = order[jnp.clip(in_e + starts[block_expert][:, None], 0, n_assign - 1)]
    slot = block_start[:, None] + jnp.arange(MOE_ROWS, dtype=jnp.int32)
    buf_tok = jnp.where(in_e < counts[block_expert][:, None], runs, slot) % n
    return buf_tok.reshape(-1), dest, block_expert


def kernel(x, meta_tokens, mix_norm, w_in, lru_conv_w, lru_conv_b, lru_w_r, lru_b_r, lru_w_i,
           lru_b_i, lru_lambda, lru_out_norm, dn_conv_w, dn_a_log, dn_dt_bias, dn_out_norm, w_out,
           ffn_norm, router_group_w, router_group_b, router_expert_w, router_expert_b, moe_w_gate,
           moe_w_up, moe_w_down, final_norm):
    bsz, seq, d = x.shape
    n = bsz * seq
    l = 0

    lw = LRU_WIDTH
    w = w_in[l]
    w_main = jnp.concatenate([w[:, :lw], w[:, 2 * lw:2 * lw + 3 * DN_WIDTH], w[:, lw:2 * lw],
                              w[:, 2 * lw + 3 * DN_WIDTH:2 * lw + 4 * DN_WIDTH]], axis=1).astype(BF16)
    w_ba = jnp.pad(w[:, 2 * lw + 4 * DN_WIDTH:], ((0, 0), (0, LANES - 2 * DN_HEADS))).astype(BF16)
    mix_g = mix_norm[l][None, :]
    conv_w = _pad_rows(jnp.concatenate([lru_conv_w[l], dn_conv_w[l]], axis=1), SUBLANES)
    lru_p = _pad_rows(jnp.stack([lru_conv_b[l], lru_b_r[l], lru_b_i[l], lru_lambda[l],
                                 lru_out_norm[l]]), SUBLANES)
    hb = LRU_BLOCKS // 2
    w_gate = jnp.stack([
        jnp.concatenate([_block_diag(lru_w_r[l][h * hb:(h + 1) * hb]),
                         _block_diag(lru_w_i[l][h * hb:(h + 1) * hb])], axis=1)
        for h in range(2)]).astype(BF16)
    head_p = jnp.zeros((SUBLANES, LANES), F32)
    head_p = head_p.at[0, DN_HEADS:2 * DN_HEADS].set(dn_a_log[l])
    head_p = head_p.at[1, DN_HEADS:2 * DN_HEADS].set(dn_dt_bias[l])
    dn_on = dn_out_norm[l][None, :]
    w_o = w_out[l].astype(BF16)
    ffn_g = ffn_norm[l][None, :]
    w_router = jnp.zeros((ROUTE_ROWS, D_MODEL), F32)
    w_router = w_router.at[:N_GROUPS].set(router_group_w[l].T)
    w_router = w_router.at[SUBLANES:SUBLANES + N_EXPERTS].set(router_expert_w[l].T).astype(BF16)
    b_router = jnp.zeros((ROUTE_ROWS,), F32).at[:N_GROUPS].set(router_group_b[l])
    b_router = b_router.at[SUBLANES:SUBLANES + N_EXPERTS].set(router_expert_b[l])
    b_router = jnp.broadcast_to(b_router[:, None], (ROUTE_ROWS, LANES))
    final_g = final_norm[None, :]
    weights = (mix_g, w_main, w_ba, conv_w, lru_p, w_gate, head_p, dn_on, w_o, ffn_g, w_router,
               b_router)

    meta_pad = CHUNK - N_META
    prefix = jnp.pad(meta_tokens, ((meta_pad, 0), (0, 0)))[None]
    zeros = lambda *s: jnp.zeros(s, F32)
    *_, h_meta, s_meta, tail_meta = _mixer(
        prefix, zeros(SUBLANES, CONV_COLS), zeros(SUBLANES, LRU_WIDTH),
        zeros(DN_HEADS, DN_HEAD_DIM, DN_HEAD_DIM), weights, CHUNK, meta_pad)

    h1, xn, route, cnt, _, _, _ = _mixer(x, tail_meta[0], h_meta[0], s_meta[0], weights, 256, 0)
    h1 = h1.reshape(n, d)
    eflat = jnp.transpose(route[:, 0:2, :], (1, 0, 2)).reshape(2 * n).astype(jnp.int32)
    gates = jnp.transpose(route[:, 2:4, :], (0, 2, 1)).reshape(n, 2)
    counts = jnp.sum(cnt, axis=(0, 2)).astype(jnp.int32)

    buf_tok, dest, block_expert = _dispatch_tables(eflat, counts, n)
    x_buf = _row_gather(xn.reshape(n, d // 2), buf_tok)
    y_buf = _experts(block_expert, x_buf, moe_w_gate[l], moe_w_up[l], moe_w_down[l])
    y_tok = _row_gather(y_buf, dest)
    out = _combine(h1, y_tok, gates, final_g, 512)
    return out.reshape(bsz, seq, d)
```

```python
import functools

import jax
import jax.numpy as jnp
from jax import lax
from jax.experimental import pallas as pl
from jax.experimental.pallas import tpu as pltpu
from jax.experimental.pallas import tpu_sc as plsc

F32 = jnp.float32
BF16 = jnp.bfloat16

D_MODEL = 1024
N_META = 16
CHUNK = 64
CONV_WIDTH = 4
LRU_WIDTH = 512
LRU_BLOCKS = 8
LRU_C = 8.0
DN_HEADS = 4
DN_HEAD_DIM = 128
DN_WIDTH = DN_HEADS * DN_HEAD_DIM
N_GROUPS = 4
EXPERTS_PER_GROUP = 8
N_EXPERTS = N_GROUPS * EXPERTS_PER_GROUP
D_EXPERT = 256
EPS = 1e-6
CONV_COLS = LRU_WIDTH + 3 * DN_WIDTH
GATE_COLS = LRU_WIDTH + DN_WIDTH
LANES = 128
SUBLANES = 8
INV_BLOCK = 16
MOE_ROWS = 512
BATCH_PARTS = 2
ROUTE_ROWS = 48
VMEM_LIMIT = 56 * 1024 * 1024
GATHER_WINDOW = 64


def _cparams(*sem):
    return pltpu.CompilerParams(dimension_semantics=sem, vmem_limit_bytes=VMEM_LIMIT)


def _sigmoid(x):
    return 0.5 * jnp.tanh(0.5 * x) + 0.5


def _pack_bf16_pairs(x):
    c = x.shape[1] // 2

    def rne(v):
        b = lax.bitcast_convert_type(v, jnp.uint32)
        return b + jnp.uint32(0x7FFF) + ((b >> 16) & jnp.uint32(1))

    return (rne(x[:, :c]) >> 16) | (rne(x[:, c:]) & jnp.uint32(0xFFFF0000))


def _unpack_bf16_pairs(w):
    lo = lax.bitcast_convert_type(w << 16, F32)
    hi = lax.bitcast_convert_type(w & jnp.uint32(0xFFFF0000), F32)
    return jnp.concatenate([lo, hi], axis=1)


def _softplus(x):
    return jnp.maximum(x, 0.0) + jnp.log(1.0 + jnp.exp(-jnp.abs(x)))


def _gelu_tanh(x):
    return 0.5 * x * (1.0 + jnp.tanh(0.7978845608028654 * (x + 0.044715 * (x * x * x))))


def _mm(a, b):
    return jnp.dot(a.astype(BF16), b.astype(BF16), preferred_element_type=F32)


def _mm_nt(a, b):
    return lax.dot_general(a.astype(BF16), b.astype(BF16), (((1,), (1,)), ((), ())),
                           preferred_element_type=F32)


def _mm_tn(a, b):
    return lax.dot_general(a.astype(BF16), b.astype(BF16), (((0,), (0,)), ((), ())),
                           preferred_element_type=F32)


def _lru_branch(xc, gate, p_ref, wg_ref, h_sc, valid):
    tb = xc.shape[0]
    b_r, b_i = p_ref[1:2, :], p_ref[2:3, :]
    lam, out_g = p_ref[3:4, :], p_ref[4:5, :]
    half = LRU_WIDTH // 2
    xcb = xc.astype(BF16)
    g0 = jnp.dot(xcb[:, :half], wg_ref[0], preferred_element_type=F32)
    g1 = jnp.dot(xcb[:, half:], wg_ref[1], preferred_element_type=F32)
    r = _sigmoid(jnp.concatenate([g0[:, :half], g1[:, :half]], axis=1) + b_r)
    i = _sigmoid(jnp.concatenate([g0[:, half:], g1[:, half:]], axis=1) + b_i)
    log_a = (-LRU_C) * r * _softplus(-lam)
    a = jnp.exp(log_a)
    var = 1.0 - jnp.exp(2.0 * log_a)
    b = jnp.where(var > 0.0, var * lax.rsqrt(var), 0.0) * (i * xc)
    if valid is not None:
        b = jnp.where(valid, b, 0.0)
        a = jnp.where(valid, a, 1.0)

    sub = lax.broadcasted_iota(jnp.int32, (tb, LRU_WIDTH), 0) & (SUBLANES - 1)
    for s in (1, 2, 4):
        keep = sub >= s
        b = jnp.where(keep, a * pltpu.roll(b, s, axis=0), 0.0) + b
        a = jnp.where(keep, a * pltpu.roll(a, s, axis=0), a)
    h = h_sc[...]
    hs = []
    for g in range(tb // SUBLANES):
        hg = a[g * SUBLANES:(g + 1) * SUBLANES] * h + b[g * SUBLANES:(g + 1) * SUBLANES]
        hs.append(hg)
        h = jnp.broadcast_to(hg[SUBLANES - 1:SUBLANES, :], (SUBLANES, LRU_WIDTH))
    h_sc[...] = h
    out = jnp.concatenate(hs, axis=0) * _gelu_tanh(gate)
    ms = jnp.mean(out * out, axis=-1, keepdims=True)
    return (out * lax.rsqrt(ms + EPS)) * out_g


def _pair_mm(x, y, half_masks):
    yb = y.astype(BF16)
    rhs = jnp.concatenate([yb * half_masks[0], yb * half_masks[1]], axis=0)
    return jnp.dot(x.astype(BF16), rhs, preferred_element_type=F32)


def _pair_inverse(a_list, eye, same16, half_masks):
    def mm(xs, ys):
        return [_pair_mm(x, y, half_masks) for x, y in zip(xs, ys)]

    def plus(xs):
        return [eye + x for x in xs]

    def minus(xs):
        return [eye - x for x in xs]

    d = [jnp.where(same16, a, 0.0) for a in a_list]
    d2 = mm(d, d)
    d4 = mm(d2, d2)
    d8 = mm(d4, d4)
    p = mm(mm(mm(minus(d), plus(d2)), plus(d4)), plus(d8))
    m = mm(p, [a - x for a, x in zip(a_list, d)])
    q = mm(minus(m), plus(mm(m, m)))
    return mm(q, p)


def _dn_branch(act, z, ba, hp_ref, on_ref, s_sc, y_sc, valid):
    tb = act.shape[0]
    hd = DN_HEAD_DIM
    act = act * _sigmoid(act)
    beta_t = _sigmoid(ba)
    g_t = pltpu.roll(-jnp.exp(hp_ref[0:1, :]) * _softplus(ba + hp_ref[1:2, :]),
                     LANES - DN_HEADS, axis=1)
    if valid is not None:
        act = jnp.where(valid, act, 0.0)
        beta_t = jnp.where(valid, beta_t, 0.0)
        g_t = jnp.where(valid, g_t, 0.0)

    heads = []
    for i in range(3 * DN_HEADS):
        seg = act[:, i * hd:(i + 1) * hd]
        if i < 2 * DN_HEADS:
            nrm = lax.rsqrt(jnp.sum(seg * seg, axis=-1, keepdims=True) + EPS)
            if i < DN_HEADS:
                nrm = nrm * (hd ** -0.5)
            seg = seg * nrm
        heads.append(seg)
    qn, kn, vv = heads[:DN_HEADS], heads[DN_HEADS:2 * DN_HEADS], heads[2 * DN_HEADS:]

    row_c = lax.broadcasted_iota(jnp.int32, (tb, LANES), 0) & (CHUNK - 1)
    cum = g_t
    s = 1
    while s < CHUNK:
        cum = cum + jnp.where(row_c >= s, pltpu.roll(cum, s, axis=0), 0.0)
        s *= 2
    e_cum = jnp.exp(cum)

    ri = lax.broadcasted_iota(jnp.int32, (CHUNK, LANES), 0)
    li = lax.broadcasted_iota(jnp.int32, (CHUNK, LANES), 1)
    lo_half = li < CHUNK
    half_masks = (lo_half.astype(BF16), (li >= CHUNK).astype(BF16))
    cj = li & (CHUNK - 1)
    eye_b = ri == cj
    eye = eye_b.astype(F32)
    causal = ri >= cj
    strict = ri > cj
    same16 = (ri // INV_BLOCK) == (cj // INV_BLOCK)
    bd_mask = ((lax.broadcasted_iota(jnp.int32, (2 * hd, 2 * hd), 0) >= hd)
               == (lax.broadcasted_iota(jnp.int32, (2 * hd, 2 * hd), 1) >= hd))
    zero_c = jnp.zeros((CHUNK, hd), F32)
    out_g = on_ref[...]

    def bd_rows(x0, x1):
        z0 = jnp.zeros_like(x0)
        return jnp.concatenate([jnp.concatenate([x0, z0], axis=1),
                                jnp.concatenate([z0, x1], axis=1)], axis=0)

    n_ch = tb // CHUNK
    n_hp = DN_HEADS // 2
    probs = [(ch, hp) for ch in range(n_ch) for hp in range(n_hp)]
    qkm, q_dec, k_dec, rhs, a_list, dec_row = [], [], [], [], [], []
    for ch, hp in probs:
        rows = slice(ch * CHUNK, (ch + 1) * CHUNK)
        h0, h1 = 2 * hp, 2 * hp + 1
        cum_c, beta_c, ecum_c = cum[rows], beta_t[rows], e_cum[rows]
        last = cum_c[CHUNK - 1:CHUNK, :]

        def tile(arr):
            return jnp.where(lo_half, jnp.broadcast_to(arr[:, h0:h0 + 1], (CHUNK, LANES)),
                             jnp.broadcast_to(arr[:, h1:h1 + 1], (CHUNK, LANES)))

        def wide(arr):
            return jnp.concatenate([jnp.broadcast_to(arr[:, h0:h0 + 1], (arr.shape[0], hd)),
                                    jnp.broadcast_to(arr[:, h1:h1 + 1], (arr.shape[0], hd))],
                                   axis=1)

        cum_cp = tile(cum_c)
        cum_rp = jnp.sum(jnp.where(eye_b, cum_cp, 0.0), axis=0, keepdims=True)
        decay = jnp.where(causal, jnp.exp(jnp.where(causal, cum_cp - cum_rp, 0.0)), 0.0)
        q_p = jnp.concatenate([qn[h0][rows], qn[h1][rows]], axis=1)
        k_p = jnp.concatenate([kn[h0][rows], kn[h1][rows]], axis=1)
        v_p = jnp.concatenate([vv[h0][rows], vv[h1][rows]], axis=1)
        qkk = _mm_nt(jnp.concatenate([q_p, k_p], axis=0),
                     bd_rows(kn[h0][rows], kn[h1][rows]))
        qkm.append(qkk[:CHUNK] * decay)
        a_list.append(jnp.where(strict, qkk[CHUNK:] * decay, 0.0) * tile(beta_c))
        vb = v_p * wide(beta_c)
        kb = k_p * wide(beta_c * ecum_c)
        rhs.append(jnp.concatenate(
            [jnp.concatenate([vb[:, :hd], zero_c, kb[:, :hd], zero_c], axis=1),
             jnp.concatenate([zero_c, vb[:, hd:], zero_c, kb[:, hd:]], axis=1)], axis=0))
        q_dec.append(q_p * wide(ecum_c))
        k_dec.append(k_p * wide(jnp.exp(last - cum_c)))
        dec_row.append(wide(jnp.exp(last)))

    t_inv = _pair_inverse(a_list, eye, same16, half_masks)
    sol = [_mm(ti, r) for ti, r in zip(t_inv, rhs)]
    ktuw = [_mm_tn(kd, so) for kd, so in zip(k_dec, sol)]
    quw = [_mm(qm, jnp.concatenate(
               [jnp.concatenate([so[:, :hd], zero_c, so[:, 2 * hd:3 * hd], zero_c], axis=1),
                jnp.concatenate([zero_c, so[:, hd:2 * hd], zero_c, so[:, 3 * hd:]], axis=1)], axis=0))
           for qm, so in zip(qkm, sol)]

    state = [s_sc[hp] for hp in range(n_hp)]
    for ch in range(n_ch):
        rows = slice(ch * CHUNK, (ch + 1) * CHUNK)
        res = []
        for hp in range(n_hp):
            i = ch * n_hp + hp
            k_w = jnp.where(bd_mask, ktuw[i][:, 2 * hd:], 0.0)
            q_eff = q_dec[i] - quw[i][:, 2 * hd:]
            res.append(_mm(jnp.concatenate([k_w, q_eff], axis=0), state[hp]))
        for hp in range(n_hp):
            i = ch * n_hp + hp
            o_p = res[hp][2 * hd:] + quw[i][:, :2 * hd]
            state[hp] = (state[hp] * dec_row[i] - res[hp][:2 * hd]
                         + jnp.where(bd_mask, ktuw[i][:, :2 * hd], 0.0))
            for e in range(2):
                lo = (2 * hp + e) * hd
                o = o_p[:, e * hd:(e + 1) * hd]
                ms = jnp.mean(o * o, axis=-1, keepdims=True)
                zz = z[rows, lo:lo + hd]
                y_sc[rows, lo:lo + hd] = ((o * lax.rsqrt(ms + EPS)) * out_g) * (zz * _sigmoid(zz))
    for hp in range(n_hp):
        s_sc[hp] = state[hp]


def _route(logits_t):
    tm = logits_t.shape[1]
    row = lax.broadcasted_iota(jnp.int32, (SUBLANES, tm), 0).astype(F32)
    neg = -jnp.inf
    big = 1e9
    lg = jnp.where(row < N_GROUPS, logits_t[0:SUBLANES], neg)
    mg = jnp.max(lg, axis=0, keepdims=True)
    p_sel = 1.0 / jnp.sum(jnp.exp(lg - mg), axis=0, keepdims=True)
    g_sel = jnp.min(jnp.where(lg == mg, row, big), axis=0, keepdims=True)
    le = logits_t[SUBLANES:2 * SUBLANES]
    for g in range(1, N_GROUPS):
        le = jnp.where(g_sel == float(g), logits_t[(g + 1) * SUBLANES:(g + 2) * SUBLANES], le)
    m1 = jnp.max(le, axis=0, keepdims=True)
    i1 = jnp.min(jnp.where(le == m1, row, big), axis=0, keepdims=True)
    le2 = jnp.where(row == i1, neg, le)
    m2 = jnp.max(le2, axis=0, keepdims=True)
    i2 = jnp.min(jnp.where(le2 == m2, row, big), axis=0, keepdims=True)
    e21 = jnp.exp(m2 - m1)
    w1 = p_sel / (1.0 + e21)
    w2 = p_sel * e21 / (1.0 + e21)
    base = g_sel * float(EXPERTS_PER_GROUP)
    return jnp.where(row == 0.0, base + i1,
                     jnp.where(row == 1.0, base + i2,
                               jnp.where(row == 2.0, w1, jnp.where(row == 3.0, w2, 0.0))))


def _mixer_kernel(x_ref, tail_ref, h0_ref, s0_ref, mg_ref, wm_ref, wba_ref, cw_ref, lp_ref, wg_ref,
                  hp_ref, on_ref, wo_ref, fg_ref, wr_ref, br_ref,
                  h1_ref, xn_ref, rt_ref, cn_ref, hl_ref, sl_ref, tl_ref,
                  h_sc, s_sc, cs_sc, y_sc, cnt_sc, *, tb, pad):
    t = pl.program_id(1)
    hd = DN_HEAD_DIM

    @pl.when(t == 0)
    def _():
        zero_hd = jnp.zeros((hd, hd), F32)
        h_sc[...] = h0_ref[...]
        for hp in range(DN_HEADS // 2):
            s_sc[hp] = jnp.concatenate(
                [jnp.concatenate([s0_ref[2 * hp], zero_hd], axis=1),
                 jnp.concatenate([zero_hd, s0_ref[2 * hp + 1]], axis=1)], axis=0)
        cs_sc[0:SUBLANES, :] = tail_ref[...]

    x = x_ref[...]
    ms = jnp.mean(x * x, axis=-1, keepdims=True)
    u = ((x * lax.rsqrt(ms + EPS)) * mg_ref[...]).astype(BF16)
    valid = None
    if pad:
        valid = lax.broadcasted_iota(jnp.int32, (tb, 1), 0) >= pad

    for n in range(0, CONV_COLS, 512):
        cs_sc[SUBLANES:SUBLANES + tb, n:n + 512] = jnp.dot(u, wm_ref[:, n:n + 512],
                                                           preferred_element_type=F32)
    gate = jnp.dot(u, wm_ref[:, CONV_COLS:CONV_COLS + LRU_WIDTH], preferred_element_type=F32)
    z = jnp.dot(u, wm_ref[:, CONV_COLS + LRU_WIDTH:], preferred_element_type=F32)
    ba = jnp.dot(u, wba_ref[...], preferred_element_type=F32)

    def conv(lo, width):
        acc = cs_sc[SUBLANES:SUBLANES + tb, lo:lo + width] * cw_ref[3:4, lo:lo + width]
        for d in range(1, CONV_WIDTH):
            acc = acc + (cs_sc[SUBLANES - d:SUBLANES - d + tb, lo:lo + width]
                         * cw_ref[3 - d:4 - d, lo:lo + width])
        return acc

    xc = conv(0, LRU_WIDTH) + lp_ref[0:1, :]
    act = conv(LRU_WIDTH, 3 * DN_WIDTH)
    cs_sc[0:SUBLANES, :] = cs_sc[tb:tb + SUBLANES, :]

    y_lru = _lru_branch(xc, gate, lp_ref, wg_ref, h_sc, valid)
    _dn_branch(act, z, ba, hp_ref, on_ref, s_sc, y_sc, valid)

    mix = jnp.dot(y_lru.astype(BF16), wo_ref[:LRU_WIDTH, :], preferred_element_type=F32)
    mix = mix + jnp.dot(y_sc[...].astype(BF16), wo_ref[LRU_WIDTH:, :], preferred_element_type=F32)
    h1 = x + mix
    h1_ref[...] = h1
    ms1 = jnp.mean(h1 * h1, axis=-1, keepdims=True)
    xn = (h1 * lax.rsqrt(ms1 + EPS)) * fg_ref[...]
    xn_ref[...] = _pack_bf16_pairs(xn)
    logits_t = lax.dot_general(wr_ref[...], xn.astype(BF16), (((1,), (1,)), ((), ())),
                               preferred_element_type=F32) + br_ref[:, 0:1]
    route = _route(logits_t)
    rt_ref[...] = route

    erow = lax.broadcasted_iota(jnp.int32, (N_EXPERTS, tb), 0).astype(F32)
    hist = (erow == route[0:1, :]).astype(F32) + (erow == route[1:2, :]).astype(F32)
    if pad:
        hist = jnp.where(lax.broadcasted_iota(jnp.int32, (1, tb), 1) >= pad, hist, 0.0)

    @pl.when(t == 0)
    def _():
        cnt_sc[...] = hist

    @pl.when(t > 0)
    def _():
        cnt_sc[...] = cnt_sc[...] + hist

    @pl.when(t == pl.num_programs(1) - 1)
    def _():
        cn_ref[...] = cnt_sc[...]
        hl_ref[...] = h_sc[...]
        for h in range(DN_HEADS):
            e = h % 2
            sl_ref[h] = s_sc[h // 2][e * hd:(e + 1) * hd, e * hd:(e + 1) * hd]
        tl_ref[...] = cs_sc[0:SUBLANES, :]


def _mixer(x3, tail8, h0, s0, weights, tb, pad, b0=0, bsz=None):
    t = x3.shape[1]
    bsz = x3.shape[0] if bsz is None else bsz
    assert tb % CHUNK == 0 and t % tb == 0 and (pad == 0 or t == tb)
    fix2 = lambda b, i: (0, 0)
    fix3 = lambda b, i: (0, 0, 0)
    blk = lambda b, i: (b, i, 0)
    per_b = lambda b, i: (b, 0, 0)
    w_specs = [pl.BlockSpec(w.shape, fix2 if w.ndim == 2 else fix3) for w in weights]
    return pl.pallas_call(
        functools.partial(_mixer_kernel, tb=tb, pad=pad),
        grid=(bsz, t // tb),
        in_specs=[pl.BlockSpec((None, tb, D_MODEL), lambda b, i: (b + b0, i, 0)),
                  pl.BlockSpec((SUBLANES, CONV_COLS), fix2),
                  pl.BlockSpec((SUBLANES, LRU_WIDTH), fix2),
                  pl.BlockSpec((DN_HEADS, DN_HEAD_DIM, DN_HEAD_DIM), fix3)] + w_specs,
        out_specs=[pl.BlockSpec((None, tb, D_MODEL), blk),
                   pl.BlockSpec((None, tb, D_MODEL // 2), blk),
                   pl.BlockSpec((None, SUBLANES, tb), lambda b, i: (b, 0, i)),
                   pl.BlockSpec((None, N_EXPERTS, tb), per_b),
                   pl.BlockSpec((None, SUBLANES, LRU_WIDTH), per_b),
                   pl.BlockSpec((None, DN_HEADS, DN_HEAD_DIM, DN_HEAD_DIM),
                                lambda b, i: (b, 0, 0, 0)),
                   pl.BlockSpec((None, SUBLANES, CONV_COLS), per_b)],
        out_shape=[jax.ShapeDtypeStruct((bsz, t, D_MODEL), F32),
                   jax.ShapeDtypeStruct((bsz, t, D_MODEL // 2), jnp.uint32),
                   jax.ShapeDtypeStruct((bsz, SUBLANES, t), F32),
                   jax.ShapeDtypeStruct((bsz, N_EXPERTS, tb), F32),
                   jax.ShapeDtypeStruct((bsz, SUBLANES, LRU_WIDTH), F32),
                   jax.ShapeDtypeStruct((bsz, DN_HEADS, DN_HEAD_DIM, DN_HEAD_DIM), F32),
                   jax.ShapeDtypeStruct((bsz, SUBLANES, CONV_COLS), F32)],
        scratch_shapes=[pltpu.VMEM((SUBLANES, LRU_WIDTH), F32),
                        pltpu.VMEM((DN_HEADS // 2, 2 * DN_HEAD_DIM, 2 * DN_HEAD_DIM), F32),
                        pltpu.VMEM((tb + SUBLANES, CONV_COLS), F32),
                        pltpu.VMEM((tb, DN_WIDTH), F32),
                        pltpu.VMEM((N_EXPERTS, tb), F32)],
        compiler_params=_cparams("parallel", "arbitrary"),
        name="mixer",
    )(x3, tail8, h0, s0, *weights)


def _expert_kernel(be_ref, nu_ref, x_ref, wg_ref, wu_ref, wd_ref, y_ref, wgu_sc, wd_sc):
    i = pl.program_id(0)
    prev = be_ref[jnp.maximum(i - 1, 0)]

    @pl.when((i == 0) | (be_ref[i] != prev))
    def _():
        wgu_sc[:, :D_EXPERT] = wg_ref[...].astype(BF16)
        wgu_sc[:, D_EXPERT:] = wu_ref[...].astype(BF16)
        wd_sc[...] = wd_ref[...].astype(BF16)

    @pl.when(i < nu_ref[0])
    def _():
        x = _unpack_bf16_pairs(x_ref[...]).astype(BF16)
        gu = jnp.dot(x, wgu_sc[...], preferred_element_type=F32)
        g, u = gu[:, :D_EXPERT], gu[:, D_EXPERT:]
        hmid = (g * _sigmoid(g)) * u
        y_ref[...] = _pack_bf16_pairs(jnp.dot(hmid.astype(BF16), wd_sc[...],
                                              preferred_element_type=F32))

    @pl.when(i >= nu_ref[0])
    def _():
        y_ref[...] = jnp.zeros_like(y_ref)


def _experts(block_expert, n_used, x_buf, w_gate, w_up, w_down):
    cap = x_buf.shape[0]
    by_expert = lambda i, be, nu: (be[i], 0, 0)
    grid_spec = pltpu.PrefetchScalarGridSpec(
        num_scalar_prefetch=2,
        grid=(cap // MOE_ROWS,),
        in_specs=[pl.BlockSpec((MOE_ROWS, D_MODEL // 2), lambda i, be, nu: (i, 0)),
                  pl.BlockSpec((None, D_MODEL, D_EXPERT), by_expert),
                  pl.BlockSpec((None, D_MODEL, D_EXPERT), by_expert),
                  pl.BlockSpec((None, D_EXPERT, D_MODEL), by_expert)],
        out_specs=pl.BlockSpec((MOE_ROWS, D_MODEL // 2), lambda i, be, nu: (i, 0)),
        scratch_shapes=[pltpu.VMEM((D_MODEL, 2 * D_EXPERT), BF16),
                        pltpu.VMEM((D_EXPERT, D_MODEL), BF16)],
    )
    return pl.pallas_call(
        _expert_kernel,
        grid_spec=grid_spec,
        out_shape=jax.ShapeDtypeStruct((cap, D_MODEL // 2), jnp.uint32),
        compiler_params=_cparams("arbitrary"),
        name="experts",
    )(block_expert, n_used, x_buf, w_gate, w_up, w_down)


def _row_gather(table, idx):
    n_idx = idx.shape[0]
    d = table.shape[1]
    sc = plsc.get_sparse_core_info()
    n_workers = sc.num_cores * sc.num_subcores
    w = GATHER_WINDOW
    per_w = n_idx // n_workers
    n_steps = per_w // w
    assert per_w * n_workers == n_idx and n_steps * w == per_w and n_steps % 2 == 0
    mesh = plsc.VectorSubcoreMesh(core_axis_name="core", subcore_axis_name="subcore")

    @functools.partial(
        pl.kernel, out_type=jax.ShapeDtypeStruct((n_idx, d), table.dtype), mesh=mesh,
        scratch_types=[pltpu.VMEM((per_w,), jnp.int32),
                       pltpu.VMEM((2, w, d), table.dtype),
                       pltpu.SemaphoreType.DMA((2,)),
                       pltpu.SemaphoreType.DMA((2,))])
    def gather(x_hbm, i_hbm, o_hbm, idx_v, rows_v, g_sem, w_sem):
        wid = lax.axis_index("subcore") * sc.num_cores + lax.axis_index("core")
        base = wid * per_w
        pltpu.sync_copy(i_hbm.at[pl.ds(base, per_w)], idx_v)

        def fetch(s, b):
            return pltpu.make_async_copy(x_hbm.at[idx_v.at[pl.ds(s * w, w)]], rows_v.at[b],
                                         g_sem.at[b])

        def flush(s, b):
            return pltpu.make_async_copy(rows_v.at[b], o_hbm.at[pl.ds(base + s * w, w)],
                                         w_sem.at[b])

        fetch(0, 0).start()

        @pl.loop(0, n_steps, step=2)
        def _(s2):
            for b in range(2):
                s = s2 + b

                @pl.when(s + 1 < n_steps)
                def _():
                    @pl.when(s >= 1)
                    def _():
                        flush(s - 1, 1 - b).wait()
                    fetch(s + 1, 1 - b).start()

                fetch(s, b).wait()
                flush(s, b).start()

        flush(n_steps - 2, 0).wait()
        flush(n_steps - 1, 1).wait()

    return gather(table, idx)


def _combine_kernel(*refs, n_parts, steps):
    fg_ref, o_ref = refs[4 * n_parts], refs[4 * n_parts + 1]
    i = pl.program_id(0)
    for p in range(n_parts):
        h1_ref, y0_ref, y1_ref, gt_ref = refs[4 * p:4 * p + 4]

        @pl.when((i >= p * steps) & (i < (p + 1) * steps))
        def _():
            gt = gt_ref[...]
            h = (h1_ref[...] + gt[:, 0:1] * _unpack_bf16_pairs(y0_ref[...])
                 + gt[:, 1:2] * _unpack_bf16_pairs(y1_ref[...]))
            ms = jnp.mean(h * h, axis=-1, keepdims=True)
            o_ref[...] = (h * lax.rsqrt(ms + EPS)) * fg_ref[...]


def _combine(parts, final_g, tm):
    m = parts[0][0].shape[0]
    steps = m // tm
    n_parts = len(parts)
    in_specs, args = [], []
    for p, (h1, y_tok, gates) in enumerate(parts):
        local = lambda i, p=p: jnp.clip(i - p * steps, 0, steps - 1)
        in_specs += [pl.BlockSpec((tm, D_MODEL), lambda i, f=local: (f(i), 0)),
                     pl.BlockSpec((tm, D_MODEL // 2), lambda i, f=local: (f(i), 0)),
                     pl.BlockSpec((tm, D_MODEL // 2), lambda i, f=local: (f(i) + steps, 0)),
                     pl.BlockSpec((tm, 2), lambda i, f=local: (f(i), 0))]
        args += [h1, y_tok, y_tok, gates]
    in_specs.append(pl.BlockSpec((1, D_MODEL), lambda i: (0, 0)))
    return pl.pallas_call(
        functools.partial(_combine_kernel, n_parts=n_parts, steps=steps),
        grid=(n_parts * steps,),
        in_specs=in_specs,
        out_specs=pl.BlockSpec((tm, D_MODEL), lambda i: (i, 0)),
        out_shape=jax.ShapeDtypeStruct((n_parts * m, D_MODEL), F32),
        compiler_params=_cparams("arbitrary"),
        name="combine",
    )(*args, final_g)


def _block_diag(blocks):
    n, r, c = blocks.shape
    out = jnp.zeros((n * r, n * c), blocks.dtype)
    for i in range(n):
        out = out.at[i * r:(i + 1) * r, i * c:(i + 1) * c].set(blocks[i])
    return out


def _pad_rows(a, rows):
    return jnp.pad(a, ((0, rows - a.shape[0]), (0, 0)))


def _slot_kernel(off_ref, e_ref, o_ref):
    e = e_ref[...]
    rows, lanes = e.shape
    j = (lax.broadcasted_iota(jnp.int32, e.shape, 0) * lanes
         + lax.broadcasted_iota(jnp.int32, e.shape, 1))
    off = jnp.zeros_like(e)
    for k in range(N_EXPERTS):
        off = jnp.where(e == k, off_ref[k], off)
    o_ref[...] = j + off


def _sorted_slots(sorted_e, offset):
    rows = sorted_e.shape[0] // LANES
    grid_spec = pltpu.PrefetchScalarGridSpec(
        num_scalar_prefetch=1, grid=(1,),
        in_specs=[pl.BlockSpec((rows, LANES), lambda i, off: (0, 0))],
        out_specs=pl.BlockSpec((rows, LANES), lambda i, off: (0, 0)))
    return pl.pallas_call(
        _slot_kernel, grid_spec=grid_spec,
        out_shape=jax.ShapeDtypeStruct((rows, LANES), jnp.int32),
        compiler_params=_cparams("arbitrary"), name="sorted_slots",
    )(offset, sorted_e.reshape(rows, LANES)).reshape(-1)


def _dispatch_tables(eflat, counts, n):
    n_assign = 2 * n
    iota = jnp.arange(n_assign, dtype=jnp.int32)
    sorted_e, order = lax.sort((eflat, iota), num_keys=1, is_stable=True)
    starts = jnp.cumsum(counts) - counts
    padded = (counts + MOE_ROWS - 1) // MOE_ROWS * MOE_ROWS
    pend = jnp.cumsum(padded)
    pstart = pend - padded
    dest_sorted = _sorted_slots(sorted_e, pstart - starts)
    _, dest = lax.sort((order, dest_sorted), num_keys=1, is_stable=False)
    n_blocks = -(-(n_assign + N_EXPERTS * (MOE_ROWS - 1)) // MOE_ROWS)
    block_start = jnp.arange(n_blocks, dtype=jnp.int32) * MOE_ROWS
    block_expert = jnp.minimum(jnp.sum(block_start[:, None] >= pend[None, :], axis=1),
                               N_EXPERTS - 1).astype(jnp.int32)
    in_e = (block_start - pstart[block_expert])[:, None] + jnp.arange(MOE_ROWS, dtype=jnp.int32)
    runs = order[jnp.clip(in_e + starts[block_expert][:, None], 0, n_assign - 1)]
    slot = block_start[:, None] + jnp.arange(MOE_ROWS, dtype=jnp.int32)
    buf_tok = jnp.where(in_e < counts[block_expert][:, None], runs, slot) % n
    n_used = (pend[N_EXPERTS - 1:] // MOE_ROWS).astype(jnp.int32)
    return buf_tok.reshape(-1), dest, block_expert, n_used


def kernel(x, meta_tokens, mix_norm, w_in, lru_conv_w, lru_conv_b, lru_w_r, lru_b_r, lru_w_i,
           lru_b_i, lru_lambda, lru_out_norm, dn_conv_w, dn_a_log, dn_dt_bias, dn_out_norm, w_out,
           ffn_norm, router_group_w, router_group_b, router_expert_w, router_expert_b, moe_w_gate,
           moe_w_up, moe_w_down, final_norm):
    bsz, seq, d = x.shape
    n = bsz * seq
    l = 0

    lw = LRU_WIDTH
    w = w_in[l]
    w_main = jnp.concatenate([w[:, :lw], w[:, 2 * lw:2 * lw + 3 * DN_WIDTH], w[:, lw:2 * lw],
                              w[:, 2 * lw + 3 * DN_WIDTH:2 * lw + 4 * DN_WIDTH]], axis=1).astype(BF16)
    w_ba = jnp.pad(w[:, 2 * lw + 4 * DN_WIDTH:], ((0, 0), (0, LANES - 2 * DN_HEADS))).astype(BF16)
    mix_g = mix_norm[l][None, :]
    conv_w = _pad_rows(jnp.concatenate([lru_conv_w[l], dn_conv_w[l]], axis=1), SUBLANES)
    lru_p = _pad_rows(jnp.stack([lru_conv_b[l], lru_b_r[l], lru_b_i[l], lru_lambda[l],
                                 lru_out_norm[l]]), SUBLANES)
    hb = LRU_BLOCKS // 2
    w_gate = jnp.stack([
        jnp.concatenate([_block_diag(lru_w_r[l][h * hb:(h + 1) * hb]),
                         _block_diag(lru_w_i[l][h * hb:(h + 1) * hb])], axis=1)
        for h in range(2)]).astype(BF16)
    head_p = jnp.zeros((SUBLANES, LANES), F32)
    head_p = head_p.at[0, DN_HEADS:2 * DN_HEADS].set(dn_a_log[l])
    head_p = head_p.at[1, DN_HEADS:2 * DN_HEADS].set(dn_dt_bias[l])
    dn_on = dn_out_norm[l][None, :]
    w_o = w_out[l].astype(BF16)
    ffn_g = ffn_norm[l][None, :]
    w_router = jnp.zeros((ROUTE_ROWS, D_MODEL), F32)
    w_router = w_router.at[:N_GROUPS].set(router_group_w[l].T)
    w_router = w_router.at[SUBLANES:SUBLANES + N_EXPERTS].set(router_expert_w[l].T).astype(BF16)
    b_router = jnp.zeros((ROUTE_ROWS,), F32).at[:N_GROUPS].set(router_group_b[l])
    b_router = b_router.at[SUBLANES:SUBLANES + N_EXPERTS].set(router_expert_b[l])
    b_router = jnp.broadcast_to(b_router[:, None], (ROUTE_ROWS, LANES))
    final_g = final_norm[None, :]
    weights = (mix_g, w_main, w_ba, conv_w, lru_p, w_gate, head_p, dn_on, w_o, ffn_g, w_router,
               b_router)

    meta_pad = CHUNK - N_META
    prefix = jnp.pad(meta_tokens, ((meta_pad, 0), (0, 0)))[None]
    zeros = lambda *s: jnp.zeros(s, F32)
    *_, h_meta, s_meta, tail_meta = _mixer(
        prefix, zeros(SUBLANES, CONV_COLS), zeros(SUBLANES, LRU_WIDTH),
        zeros(DN_HEADS, DN_HEAD_DIM, DN_HEAD_DIM), weights, CHUNK, meta_pad)

    n_parts = BATCH_PARTS if bsz % BATCH_PARTS == 0 else 1
    pb = bsz // n_parts
    m = pb * seq
    parts = []
    for p in range(n_parts):
        h1, xn, route, cnt, _, _, _ = _mixer(x, tail_meta[0], h_meta[0], s_meta[0], weights,
                                             256, 0, b0=p * pb, bsz=pb)
        eflat = jnp.transpose(route[:, 0:2, :], (1, 0, 2)).reshape(2 * m).astype(jnp.int32)
        gates = jnp.transpose(route[:, 2:4, :], (0, 2, 1)).reshape(m, 2)
        counts = jnp.sum(cnt, axis=(0, 2)).astype(jnp.int32)
        buf_tok, dest, block_expert, n_used = _dispatch_tables(eflat, counts, m)
        x_buf = _row_gather(xn.reshape(m, d // 2), buf_tok)
        y_buf = _experts(block_expert, n_used, x_buf, moe_w_gate[l], moe_w_up[l], moe_w_down[l])
        y_tok = _row_gather(y_buf, dest)
        parts.append((h1.reshape(m, d), y_tok, gates))
    out = _combine(parts, final_g, 512)
    return out.reshape(bsz, seq, d)
```

```python
import functools

import jax
import jax.numpy as jnp
from jax import lax
from jax.experimental import pallas as pl
from jax.experimental.pallas import tpu as pltpu
from jax.experimental.pallas import tpu_sc as plsc

F32 = jnp.float32
BF16 = jnp.bfloat16

D_MODEL = 1024
N_META = 16
CHUNK = 64
CONV_WIDTH = 4
LRU_WIDTH = 512
LRU_BLOCKS = 8
LRU_C = 8.0
DN_HEADS = 4
DN_HEAD_DIM = 128
DN_WIDTH = DN_HEADS * DN_HEAD_DIM
N_GROUPS = 4
EXPERTS_PER_GROUP = 8
N_EXPERTS = N_GROUPS * EXPERTS_PER_GROUP
D_EXPERT = 256
EPS = 1e-6
CONV_COLS = LRU_WIDTH + 3 * DN_WIDTH
GATE_COLS = LRU_WIDTH + DN_WIDTH
LANES = 128
SUBLANES = 8
INV_BLOCK = 16
MOE_ROWS = 512
BATCH_PARTS = 1
ROUTE_ROWS = 48
VMEM_LIMIT = 56 * 1024 * 1024
GATHER_WINDOW = 64


def _cparams(*sem):
    return pltpu.CompilerParams(dimension_semantics=sem, vmem_limit_bytes=VMEM_LIMIT)


def _sigmoid(x):
    return 0.5 * jnp.tanh(0.5 * x) + 0.5


def _pack_bf16_pairs(x):
    c = x.shape[1] // 2

    def rne(v):
        b = lax.bitcast_convert_type(v, jnp.uint32)
        return b + jnp.uint32(0x7FFF) + ((b >> 16) & jnp.uint32(1))

    return (rne(x[:, :c]) >> 16) | (rne(x[:, c:]) & jnp.uint32(0xFFFF0000))


def _unpack_bf16_pairs(w):
    lo = lax.bitcast_convert_type(w << 16, F32)
    hi = lax.bitcast_convert_type(w & jnp.uint32(0xFFFF0000), F32)
    return jnp.concatenate([lo, hi], axis=1)


def _softplus(x):
    return jnp.maximum(x, 0.0) + jnp.log(1.0 + jnp.exp(-jnp.abs(x)))


def _gelu_tanh(x):
    return 0.5 * x * (1.0 + jnp.tanh(0.7978845608028654 * (x + 0.044715 * (x * x * x))))


def _mm(a, b):
    return jnp.dot(a.astype(BF16), b.astype(BF16), preferred_element_type=F32)


def _mm_nt(a, b):
    return lax.dot_general(a.astype(BF16), b.astype(BF16), (((1,), (1,)), ((), ())),
                           preferred_element_type=F32)


def _mm_tn(a, b):
    return lax.dot_general(a.astype(BF16), b.astype(BF16), (((0,), (0,)), ((), ())),
                           preferred_element_type=F32)


def _lru_branch(xc, gate, p_ref, wg_ref, h_sc, valid, tick=lambda: None):
    tb = xc.shape[0]
    b_r, b_i = p_ref[1:2, :], p_ref[2:3, :]
    lam, out_g = p_ref[3:4, :], p_ref[4:5, :]
    half = LRU_WIDTH // 2
    xcb = xc.astype(BF16)
    g0 = jnp.dot(xcb[:, :half], wg_ref[0], preferred_element_type=F32)
    g1 = jnp.dot(xcb[:, half:], wg_ref[1], preferred_element_type=F32)
    r = _sigmoid(jnp.concatenate([g0[:, :half], g1[:, :half]], axis=1) + b_r)
    i = _sigmoid(jnp.concatenate([g0[:, half:], g1[:, half:]], axis=1) + b_i)
    tick()
    log_a = (-LRU_C) * r * _softplus(-lam)
    a = jnp.exp(log_a)
    var = 1.0 - jnp.exp(2.0 * log_a)
    b = jnp.where(var > 0.0, var * lax.rsqrt(var), 0.0) * (i * xc)
    if valid is not None:
        b = jnp.where(valid, b, 0.0)
        a = jnp.where(valid, a, 1.0)
    tick()

    sub = lax.broadcasted_iota(jnp.int32, (tb, LRU_WIDTH), 0) & (SUBLANES - 1)
    for s in (1, 2, 4):
        keep = sub >= s
        b = jnp.where(keep, a * pltpu.roll(b, s, axis=0), 0.0) + b
        a = jnp.where(keep, a * pltpu.roll(a, s, axis=0), a)
        tick()
    h = h_sc[...]
    hs = []
    for g in range(tb // SUBLANES):
        hg = a[g * SUBLANES:(g + 1) * SUBLANES] * h + b[g * SUBLANES:(g + 1) * SUBLANES]
        hs.append(hg)
        h = jnp.broadcast_to(hg[SUBLANES - 1:SUBLANES, :], (SUBLANES, LRU_WIDTH))
    h_sc[...] = h
    out = jnp.concatenate(hs, axis=0) * _gelu_tanh(gate)
    tick()
    ms = jnp.mean(out * out, axis=-1, keepdims=True)
    return (out * lax.rsqrt(ms + EPS)) * out_g


def _pair_mm(x, y, half_masks):
    yb = y.astype(BF16)
    rhs = jnp.concatenate([yb * half_masks[0], yb * half_masks[1]], axis=0)
    return jnp.dot(x.astype(BF16), rhs, preferred_element_type=F32)


def _pair_inverse(a_list, eye, same16, half_masks):
    def mm(xs, ys):
        return [_pair_mm(x, y, half_masks) for x, y in zip(xs, ys)]

    def plus(xs):
        return [eye + x for x in xs]

    def minus(xs):
        return [eye - x for x in xs]

    d = [jnp.where(same16, a, 0.0) for a in a_list]
    d2 = mm(d, d)
    d4 = mm(d2, d2)
    d8 = mm(d4, d4)
    p = mm(mm(mm(minus(d), plus(d2)), plus(d4)), plus(d8))
    m = mm(p, [a - x for a, x in zip(a_list, d)])
    q = mm(minus(m), plus(mm(m, m)))
    return mm(q, p)


def _dn_branch(act, z, ba, hp_ref, on_ref, s_sc, y_sc, valid, tick=lambda: None):
    tb = act.shape[0]
    hd = DN_HEAD_DIM
    act = act * _sigmoid(act)
    beta_t = _sigmoid(ba)
    g_t = pltpu.roll(-jnp.exp(hp_ref[0:1, :]) * _softplus(ba + hp_ref[1:2, :]),
                     LANES - DN_HEADS, axis=1)
    if valid is not None:
        act = jnp.where(valid, act, 0.0)
        beta_t = jnp.where(valid, beta_t, 0.0)
        g_t = jnp.where(valid, g_t, 0.0)

    heads = []
    for i in range(3 * DN_HEADS):
        seg = act[:, i * hd:(i + 1) * hd]
        if i < 2 * DN_HEADS:
            nrm = lax.rsqrt(jnp.sum(seg * seg, axis=-1, keepdims=True) + EPS)
            if i < DN_HEADS:
                nrm = nrm * (hd ** -0.5)
            seg = seg * nrm
        heads.append(seg)
    qn, kn, vv = heads[:DN_HEADS], heads[DN_HEADS:2 * DN_HEADS], heads[2 * DN_HEADS:]

    row_c = lax.broadcasted_iota(jnp.int32, (tb, LANES), 0) & (CHUNK - 1)
    cum = g_t
    s = 1
    while s < CHUNK:
        cum = cum + jnp.where(row_c >= s, pltpu.roll(cum, s, axis=0), 0.0)
        s *= 2
    e_cum = jnp.exp(cum)

    ri = lax.broadcasted_iota(jnp.int32, (CHUNK, LANES), 0)
    li = lax.broadcasted_iota(jnp.int32, (CHUNK, LANES), 1)
    lo_half = li < CHUNK
    half_masks = (lo_half.astype(BF16), (li >= CHUNK).astype(BF16))
    cj = li & (CHUNK - 1)
    eye_b = ri == cj
    eye = eye_b.astype(F32)
    causal = ri >= cj
    strict = ri > cj
    same16 = (ri // INV_BLOCK) == (cj // INV_BLOCK)
    bd_mask = ((lax.broadcasted_iota(jnp.int32, (2 * hd, 2 * hd), 0) >= hd)
               == (lax.broadcasted_iota(jnp.int32, (2 * hd, 2 * hd), 1) >= hd))
    zero_c = jnp.zeros((CHUNK, hd), F32)
    out_g = on_ref[...]

    def bd_rows(x0, x1):
        z0 = jnp.zeros_like(x0)
        return jnp.concatenate([jnp.concatenate([x0, z0], axis=1),
                                jnp.concatenate([z0, x1], axis=1)], axis=0)

    n_ch = tb // CHUNK
    n_hp = DN_HEADS // 2
    probs = [(ch, hp) for ch in range(n_ch) for hp in range(n_hp)]
    qkm, q_dec, k_dec, rhs, a_list, dec_row = [], [], [], [], [], []
    for ch, hp in probs:
        rows = slice(ch * CHUNK, (ch + 1) * CHUNK)
        h0, h1 = 2 * hp, 2 * hp + 1
        cum_c, beta_c, ecum_c = cum[rows], beta_t[rows], e_cum[rows]
        last = cum_c[CHUNK - 1:CHUNK, :]

        def tile(arr):
            return jnp.where(lo_half, jnp.broadcast_to(arr[:, h0:h0 + 1], (CHUNK, LANES)),
                             jnp.broadcast_to(arr[:, h1:h1 + 1], (CHUNK, LANES)))

        def wide(arr):
            return jnp.concatenate([jnp.broadcast_to(arr[:, h0:h0 + 1], (arr.shape[0], hd)),
                                    jnp.broadcast_to(arr[:, h1:h1 + 1], (arr.shape[0], hd))],
                                   axis=1)

        cum_cp = tile(cum_c)
        cum_rp = jnp.sum(jnp.where(eye_b, cum_cp, 0.0), axis=0, keepdims=True)
        decay = jnp.where(causal, jnp.exp(jnp.where(causal, cum_cp - cum_rp, 0.0)), 0.0)
        q_p = jnp.concatenate([qn[h0][rows], qn[h1][rows]], axis=1)
        k_p = jnp.concatenate([kn[h0][rows], kn[h1][rows]], axis=1)
        v_p = jnp.concatenate([vv[h0][rows], vv[h1][rows]], axis=1)
        qkk = _mm_nt(jnp.concatenate([q_p, k_p], axis=0),
                     bd_rows(kn[h0][rows], kn[h1][rows]))
        qkm.append(qkk[:CHUNK] * decay)
        a_list.append(jnp.where(strict, qkk[CHUNK:] * decay, 0.0) * tile(beta_c))
        vb = v_p * wide(beta_c)
        kb = k_p * wide(beta_c * ecum_c)
        rhs.append(jnp.concatenate(
            [jnp.concatenate([vb[:, :hd], zero_c, kb[:, :hd], zero_c], axis=1),
             jnp.concatenate([zero_c, vb[:, hd:], zero_c, kb[:, hd:]], axis=1)], axis=0))
        q_dec.append(q_p * wide(ecum_c))
        k_dec.append(k_p * wide(jnp.exp(last - cum_c)))
        dec_row.append(wide(jnp.exp(last)))

    t_inv = _pair_inverse(a_list, eye, same16, half_masks)
    sol = [_mm(ti, r) for ti, r in zip(t_inv, rhs)]
    ktuw = [_mm_tn(kd, so) for kd, so in zip(k_dec, sol)]
    quw = [_mm(qm, jnp.concatenate(
               [jnp.concatenate([so[:, :hd], zero_c, so[:, 2 * hd:3 * hd], zero_c], axis=1),
                jnp.concatenate([zero_c, so[:, hd:2 * hd], zero_c, so[:, 3 * hd:]], axis=1)], axis=0))
           for qm, so in zip(qkm, sol)]

    state = [s_sc[hp] for hp in range(n_hp)]
    for ch in range(n_ch):
        rows = slice(ch * CHUNK, (ch + 1) * CHUNK)
        res = []
        for hp in range(n_hp):
            i = ch * n_hp + hp
            k_w = jnp.where(bd_mask, ktuw[i][:, 2 * hd:], 0.0)
            q_eff = q_dec[i] - quw[i][:, 2 * hd:]
            res.append(_mm(jnp.concatenate([k_w, q_eff], axis=0), state[hp]))
        tick()
        for hp in range(n_hp):
            i = ch * n_hp + hp
            o_p = res[hp][2 * hd:] + quw[i][:, :2 * hd]
            state[hp] = (state[hp] * dec_row[i] - res[hp][:2 * hd]
                         + jnp.where(bd_mask, ktuw[i][:, :2 * hd], 0.0))
            for e in range(2):
                lo = (2 * hp + e) * hd
                o = o_p[:, e * hd:(e + 1) * hd]
                ms = jnp.mean(o * o, axis=-1, keepdims=True)
                zz = z[rows, lo:lo + hd]
                y_sc[rows, lo:lo + hd] = ((o * lax.rsqrt(ms + EPS)) * out_g) * (zz * _sigmoid(zz))
    for hp in range(n_hp):
        s_sc[hp] = state[hp]


def _route(logits_t):
    tm = logits_t.shape[1]
    row = lax.broadcasted_iota(jnp.int32, (SUBLANES, tm), 0).astype(F32)
    neg = -jnp.inf
    big = 1e9
    lg = jnp.where(row < N_GROUPS, logits_t[0:SUBLANES], neg)
    mg = jnp.max(lg, axis=0, keepdims=True)
    p_sel = 1.0 / jnp.sum(jnp.exp(lg - mg), axis=0, keepdims=True)
    g_sel = jnp.min(jnp.where(lg == mg, row, big), axis=0, keepdims=True)
    le = logits_t[SUBLANES:2 * SUBLANES]
    for g in range(1, N_GROUPS):
        le = jnp.where(g_sel == float(g), logits_t[(g + 1) * SUBLANES:(g + 2) * SUBLANES], le)
    m1 = jnp.max(le, axis=0, keepdims=True)
    i1 = jnp.min(jnp.where(le == m1, row, big), axis=0, keepdims=True)
    le2 = jnp.where(row == i1, neg, le)
    m2 = jnp.max(le2, axis=0, keepdims=True)
    i2 = jnp.min(jnp.where(le2 == m2, row, big), axis=0, keepdims=True)
    e21 = jnp.exp(m2 - m1)
    w1 = p_sel / (1.0 + e21)
    w2 = p_sel * e21 / (1.0 + e21)
    base = g_sel * float(EXPERTS_PER_GROUP)
    return jnp.where(row == 0.0, base + i1,
                     jnp.where(row == 1.0, base + i2,
                               jnp.where(row == 2.0, w1, jnp.where(row == 3.0, w2, 0.0))))


def _mixer_kernel(x_ref, tail_ref, h0_ref, s0_ref, mg_ref, wm_ref, wba_ref, cw_ref, lp_ref, wg_ref,
                  hp_ref, on_ref, wo_ref, fg_ref, wr_ref, br_ref,
                  h1_ref, xn_ref, rt_ref, cn_ref, hl_ref, sl_ref, tl_ref,
                  h_sc, s_sc, cs_sc, y_sc, cnt_sc, *, tb, pad):
    t = pl.program_id(1)
    hd = DN_HEAD_DIM

    @pl.when(t == 0)
    def _():
        zero_hd = jnp.zeros((hd, hd), F32)
        h_sc[...] = h0_ref[...]
        for hp in range(DN_HEADS // 2):
            s_sc[hp] = jnp.concatenate(
                [jnp.concatenate([s0_ref[2 * hp], zero_hd], axis=1),
                 jnp.concatenate([zero_hd, s0_ref[2 * hp + 1]], axis=1)], axis=0)
        cs_sc[0:SUBLANES, :] = tail_ref[...]

    x = x_ref[...]
    ms = jnp.mean(x * x, axis=-1, keepdims=True)
    u = ((x * lax.rsqrt(ms + EPS)) * mg_ref[...]).astype(BF16)
    valid = None
    if pad:
        valid = lax.broadcasted_iota(jnp.int32, (tb, 1), 0) >= pad

    def project_conv(n):
        cs_sc[SUBLANES:SUBLANES + tb, n:n + 512] = jnp.dot(u, wm_ref[:, n:n + 512],
                                                           preferred_element_type=F32)

    def conv(lo, width):
        acc = cs_sc[SUBLANES:SUBLANES + tb, lo:lo + width] * cw_ref[3:4, lo:lo + width]
        for d in range(1, CONV_WIDTH):
            acc = acc + (cs_sc[SUBLANES - d:SUBLANES - d + tb, lo:lo + width]
                         * cw_ref[3 - d:4 - d, lo:lo + width])
        return acc

    project_conv(0)
    gate = jnp.dot(u, wm_ref[:, CONV_COLS:CONV_COLS + LRU_WIDTH], preferred_element_type=F32)
    xc = conv(0, LRU_WIDTH) + lp_ref[0:1, :]

    late = {}
    pending = [functools.partial(project_conv, n) for n in range(LRU_WIDTH, CONV_COLS, 512)]
    pending.append(lambda: late.update(
        z=jnp.dot(u, wm_ref[:, CONV_COLS + LRU_WIDTH:], preferred_element_type=F32)))
    pending.append(lambda: late.update(
        ba=jnp.dot(u, wba_ref[...], preferred_element_type=F32)))

    def tick():
        if pending:
            pending.pop(0)()

    y_lru = _lru_branch(xc, gate, lp_ref, wg_ref, h_sc, valid, tick)
    while pending:
        tick()
    act = conv(LRU_WIDTH, 3 * DN_WIDTH)
    cs_sc[0:SUBLANES, :] = cs_sc[tb:tb + SUBLANES, :]
    pending.append(lambda: late.update(mix=jnp.dot(
        y_lru.astype(BF16), wo_ref[:LRU_WIDTH, :], preferred_element_type=F32)))
    _dn_branch(act, late["z"], late["ba"], hp_ref, on_ref, s_sc, y_sc, valid, tick)
    mix = late["mix"] + jnp.dot(y_sc[...].astype(BF16), wo_ref[LRU_WIDTH:, :],
                                preferred_element_type=F32)
    h1 = x + mix
    h1_ref[...] = h1
    ms1 = jnp.mean(h1 * h1, axis=-1, keepdims=True)
    xn = (h1 * lax.rsqrt(ms1 + EPS)) * fg_ref[...]
    xn_ref[...] = _pack_bf16_pairs(xn)
    logits_t = lax.dot_general(wr_ref[...], xn.astype(BF16), (((1,), (1,)), ((), ())),
                               preferred_element_type=F32) + br_ref[:, 0:1]
    route = _route(logits_t)
    rt_ref[...] = route

    erow = lax.broadcasted_iota(jnp.int32, (N_EXPERTS, tb), 0).astype(F32)
    hist = (erow == route[0:1, :]).astype(F32) + (erow == route[1:2, :]).astype(F32)
    if pad:
        hist = jnp.where(lax.broadcasted_iota(jnp.int32, (1, tb), 1) >= pad, hist, 0.0)

    @pl.when(t == 0)
    def _():
        cnt_sc[...] = hist

    @pl.when(t > 0)
    def _():
        cnt_sc[...] = cnt_sc[...] + hist

    @pl.when(t == pl.num_programs(1) - 1)
    def _():
        cn_ref[...] = cnt_sc[...]
        hl_ref[...] = h_sc[...]
        for h in range(DN_HEADS):
            e = h % 2
            sl_ref[h] = s_sc[h // 2][e * hd:(e + 1) * hd, e * hd:(e + 1) * hd]
        tl_ref[...] = cs_sc[0:SUBLANES, :]


def _mixer(x3, tail8, h0, s0, weights, tb, pad, b0=0, bsz=None):
    t = x3.shape[1]
    bsz = x3.shape[0] if bsz is None else bsz
    assert tb % CHUNK == 0 and t % tb == 0 and (pad == 0 or t == tb)
    fix2 = lambda b, i: (0, 0)
    fix3 = lambda b, i: (0, 0, 0)
    blk = lambda b, i: (b, i, 0)
    per_b = lambda b, i: (b, 0, 0)
    w_specs = [pl.BlockSpec(w.shape, fix2 if w.ndim == 2 else fix3) for w in weights]
    return pl.pallas_call(
        functools.partial(_mixer_kernel, tb=tb, pad=pad),
        grid=(bsz, t // tb),
        in_specs=[pl.BlockSpec((None, tb, D_MODEL), lambda b, i: (b + b0, i, 0)),
                  pl.BlockSpec((SUBLANES, CONV_COLS), fix2),
                  pl.BlockSpec((SUBLANES, LRU_WIDTH), fix2),
                  pl.BlockSpec((DN_HEADS, DN_HEAD_DIM, DN_HEAD_DIM), fix3)] + w_specs,
        out_specs=[pl.BlockSpec((None, tb, D_MODEL), blk),
                   pl.BlockSpec((None, tb, D_MODEL // 2), blk),
                   pl.BlockSpec((None, SUBLANES, tb), lambda b, i: (b, 0, i)),
                   pl.BlockSpec((None, N_EXPERTS, tb), per_b),
                   pl.BlockSpec((None, SUBLANES, LRU_WIDTH), per_b),
                   pl.BlockSpec((None, DN_HEADS, DN_HEAD_DIM, DN_HEAD_DIM),
                                lambda b, i: (b, 0, 0, 0)),
                   pl.BlockSpec((None, SUBLANES, CONV_COLS), per_b)],
        out_shape=[jax.ShapeDtypeStruct((bsz, t, D_MODEL), F32),
                   jax.ShapeDtypeStruct((bsz, t, D_MODEL // 2), jnp.uint32),
                   jax.ShapeDtypeStruct((bsz, SUBLANES, t), F32),
                   jax.ShapeDtypeStruct((bsz, N_EXPERTS, tb), F32),
                   jax.ShapeDtypeStruct((bsz, SUBLANES, LRU_WIDTH), F32),
                   jax.ShapeDtypeStruct((bsz, DN_HEADS, DN_HEAD_DIM, DN_HEAD_DIM), F32),
                   jax.ShapeDtypeStruct((bsz, SUBLANES, CONV_COLS), F32)],
        scratch_shapes=[pltpu.VMEM((SUBLANES, LRU_WIDTH), F32),
                        pltpu.VMEM((DN_HEADS // 2, 2 * DN_HEAD_DIM, 2 * DN_HEAD_DIM), F32),
                        pltpu.VMEM((tb + SUBLANES, CONV_COLS), F32),
                        pltpu.VMEM((tb, DN_WIDTH), F32),
                        pltpu.VMEM((N_EXPERTS, tb), F32)],
        compiler_params=_cparams("parallel", "arbitrary"),
        name="mixer",
    )(x3, tail8, h0, s0, *weights)


def _expert_kernel(be_ref, nu_ref, x_ref, wg_ref, wu_ref, wd_ref, y_ref, wgu_sc, wd_sc):
    i = pl.program_id(0)
    prev = be_ref[jnp.maximum(i - 1, 0)]

    @pl.when((i == 0) | (be_ref[i] != prev))
    def _():
        wgu_sc[:, :D_EXPERT] = wg_ref[...].astype(BF16)
        wgu_sc[:, D_EXPERT:] = wu_ref[...].astype(BF16)
        wd_sc[...] = wd_ref[...].astype(BF16)

    @pl.when(i < nu_ref[0])
    def _():
        x = _unpack_bf16_pairs(x_ref[...]).astype(BF16)
        gu = jnp.dot(x, wgu_sc[...], preferred_element_type=F32)
        g, u = gu[:, :D_EXPERT], gu[:, D_EXPERT:]
        hmid = (g * _sigmoid(g)) * u
        y_ref[...] = _pack_bf16_pairs(jnp.dot(hmid.astype(BF16), wd_sc[...],
                                              preferred_element_type=F32))

    @pl.when(i >= nu_ref[0])
    def _():
        y_ref[...] = jnp.zeros_like(y_ref)


def _experts(block_expert, n_used, x_buf, w_gate, w_up, w_down):
    cap = x_buf.shape[0]
    by_expert = lambda i, be, nu: (be[i], 0, 0)
    grid_spec = pltpu.PrefetchScalarGridSpec(
        num_scalar_prefetch=2,
        grid=(cap // MOE_ROWS,),
        in_specs=[pl.BlockSpec((MOE_ROWS, D_MODEL // 2), lambda i, be, nu: (i, 0)),
                  pl.BlockSpec((None, D_MODEL, D_EXPERT), by_expert),
                  pl.BlockSpec((None, D_MODEL, D_EXPERT), by_expert),
                  pl.BlockSpec((None, D_EXPERT, D_MODEL), by_expert)],
        out_specs=pl.BlockSpec((MOE_ROWS, D_MODEL // 2), lambda i, be, nu: (i, 0)),
        scratch_shapes=[pltpu.VMEM((D_MODEL, 2 * D_EXPERT), BF16),
                        pltpu.VMEM((D_EXPERT, D_MODEL), BF16)],
    )
    return pl.pallas_call(
        _expert_kernel,
        grid_spec=grid_spec,
        out_shape=jax.ShapeDtypeStruct((cap, D_MODEL // 2), jnp.uint32),
        compiler_params=_cparams("arbitrary"),
        name="experts",
    )(block_expert, n_used, x_buf, w_gate, w_up, w_down)


def _row_gather(table, idx):
    n_idx = idx.shape[0]
    d = table.shape[1]
    sc = plsc.get_sparse_core_info()
    n_workers = sc.num_cores * sc.num_subcores
    w = GATHER_WINDOW
    per_w = n_idx // n_workers
    n_steps = per_w // w
    assert per_w * n_workers == n_idx and n_steps * w == per_w and n_steps % 2 == 0
    mesh = plsc.VectorSubcoreMesh(core_axis_name="core", subcore_axis_name="subcore")

    @functools.partial(
        pl.kernel, out_type=jax.ShapeDtypeStruct((n_idx, d), table.dtype), mesh=mesh,
        scratch_types=[pltpu.VMEM((per_w,), jnp.int32),
                       pltpu.VMEM((2, w, d), table.dtype),
                       pltpu.SemaphoreType.DMA((2,)),
                       pltpu.SemaphoreType.DMA((2,))])
    def gather(x_hbm, i_hbm, o_hbm, idx_v, rows_v, g_sem, w_sem):
        wid = lax.axis_index("subcore") * sc.num_cores + lax.axis_index("core")
        base = wid * per_w
        pltpu.sync_copy(i_hbm.at[pl.ds(base, per_w)], idx_v)

        def fetch(s, b):
            return pltpu.make_async_copy(x_hbm.at[idx_v.at[pl.ds(s * w, w)]], rows_v.at[b],
                                         g_sem.at[b])

        def flush(s, b):
            return pltpu.make_async_copy(rows_v.at[b], o_hbm.at[pl.ds(base + s * w, w)],
                                         w_sem.at[b])

        fetch(0, 0).start()

        @pl.loop(0, n_steps, step=2)
        def _(s2):
            for b in range(2):
                s = s2 + b

                @pl.when(s + 1 < n_steps)
                def _():
                    @pl.when(s >= 1)
                    def _():
                        flush(s - 1, 1 - b).wait()
                    fetch(s + 1, 1 - b).start()

                fetch(s, b).wait()
                flush(s, b).start()

        flush(n_steps - 2, 0).wait()
        flush(n_steps - 1, 1).wait()

    return gather(table, idx)


def _combine_kernel(*refs, n_parts, steps):
    fg_ref, o_ref = refs[4 * n_parts], refs[4 * n_parts + 1]
    i = pl.program_id(0)
    for p in range(n_parts):
        h1_ref, y0_ref, y1_ref, gt_ref = refs[4 * p:4 * p + 4]

        @pl.when((i >= p * steps) & (i < (p + 1) * steps))
        def _():
            gt = gt_ref[...]
            h = (h1_ref[...] + gt[:, 0:1] * _unpack_bf16_pairs(y0_ref[...])
                 + gt[:, 1:2] * _unpack_bf16_pairs(y1_ref[...]))
            ms = jnp.mean(h * h, axis=-1, keepdims=True)
            o_ref[...] = (h * lax.rsqrt(ms + EPS)) * fg_ref[...]


def _combine(parts, final_g, tm):
    m = parts[0][0].shape[0]
    steps = m // tm
    n_parts = len(parts)
    in_specs, args = [], []
    for p, (h1, y_tok, gates) in enumerate(parts):
        local = lambda i, p=p: jnp.clip(i - p * steps, 0, steps - 1)
        in_specs += [pl.BlockSpec((tm, D_MODEL), lambda i, f=local: (f(i), 0)),
                     pl.BlockSpec((tm, D_MODEL // 2), lambda i, f=local: (f(i), 0)),
                     pl.BlockSpec((tm, D_MODEL // 2), lambda i, f=local: (f(i) + steps, 0)),
                     pl.BlockSpec((tm, 2), lambda i, f=local: (f(i), 0))]
        args += [h1, y_tok, y_tok, gates]
    in_specs.append(pl.BlockSpec((1, D_MODEL), lambda i: (0, 0)))
    return pl.pallas_call(
        functools.partial(_combine_kernel, n_parts=n_parts, steps=steps),
        grid=(n_parts * steps,),
        in_specs=in_specs,
        out_specs=pl.BlockSpec((tm, D_MODEL), lambda i: (i, 0)),
        out_shape=jax.ShapeDtypeStruct((n_parts * m, D_MODEL), F32),
        compiler_params=_cparams("arbitrary"),
        name="combine",
    )(*args, final_g)


def _block_diag(blocks):
    n, r, c = blocks.shape
    out = jnp.zeros((n * r, n * c), blocks.dtype)
    for i in range(n):
        out = out.at[i * r:(i + 1) * r, i * c:(i + 1) * c].set(blocks[i])
    return out


def _pad_rows(a, rows):
    return jnp.pad(a, ((0, rows - a.shape[0]), (0, 0)))


def _slot_kernel(off_ref, e_ref, o_ref):
    e = e_ref[...]
    rows, lanes = e.shape
    j = (lax.broadcasted_iota(jnp.int32, e.shape, 0) * lanes
         + lax.broadcasted_iota(jnp.int32, e.shape, 1))
    off = jnp.zeros_like(e)
    for k in range(N_EXPERTS):
        off = jnp.where(e == k, off_ref[k], off)
    o_ref[...] = j + off


def _sorted_slots(sorted_e, offset):
    rows = sorted_e.shape[0] // LANES
    grid_spec = pltpu.PrefetchScalarGridSpec(
        num_scalar_prefetch=1, grid=(1,),
        in_specs=[pl.BlockSpec((rows, LANES), lambda i, off: (0, 0))],
        out_specs=pl.BlockSpec((rows, LANES), lambda i, off: (0, 0)))
    return pl.pallas_call(
        _slot_kernel, grid_spec=grid_spec,
        out_shape=jax.ShapeDtypeStruct((rows, LANES), jnp.int32),
        compiler_params=_cparams("arbitrary"), name="sorted_slots",
    )(offset, sorted_e.reshape(rows, LANES)).reshape(-1)


def _dispatch_tables(eflat, counts, n):
    n_assign = 2 * n
    iota = jnp.arange(n_assign, dtype=jnp.int32)
    sorted_e, order = lax.sort((eflat, iota), num_keys=1, is_stable=True)
    starts = jnp.cumsum(counts) - counts
    padded = (counts + MOE_ROWS - 1) // MOE_ROWS * MOE_ROWS
    pend = jnp.cumsum(padded)
    pstart = pend - padded
    dest_sorted = _sorted_slots(sorted_e, pstart - starts)
    _, dest = lax.sort((order, dest_sorted), num_keys=1, is_stable=False)
    n_blocks = -(-(n_assign + N_EXPERTS * (MOE_ROWS - 1)) // MOE_ROWS)
    block_start = jnp.arange(n_blocks, dtype=jnp.int32) * MOE_ROWS
    block_expert = jnp.minimum(jnp.sum(block_start[:, None] >= pend[None, :], axis=1),
                               N_EXPERTS - 1).astype(jnp.int32)
    in_e = (block_start - pstart[block_expert])[:, None] + jnp.arange(MOE_ROWS, dtype=jnp.int32)
    runs = order[jnp.clip(in_e + starts[block_expert][:, None], 0, n_assign - 1)]
    slot = block_start[:, None] + jnp.arange(MOE_ROWS, dtype=jnp.int32)
    buf_tok = jnp.where(in_e < counts[block_expert][:, None], runs, slot) % n
    n_used = (pend[N_EXPERTS - 1:] // MOE_ROWS).astype(jnp.int32)
    return buf_tok.reshape(-1), dest, block_expert, n_used


def kernel(x, meta_tokens, mix_norm, w_in, lru_conv_w, lru_conv_b, lru_w_r, lru_b_r, lru_w_i,
           lru_b_i, lru_lambda, lru_out_norm, dn_conv_w, dn_a_log, dn_dt_bias, dn_out_norm, w_out,
           ffn_norm, router_group_w, router_group_b, router_expert_w, router_expert_b, moe_w_gate,
           moe_w_up, moe_w_down, final_norm):
    bsz, seq, d = x.shape
    n = bsz * seq
    l = 0

    lw = LRU_WIDTH
    w = w_in[l]
    w_main = jnp.concatenate([w[:, :lw], w[:, 2 * lw:2 * lw + 3 * DN_WIDTH], w[:, lw:2 * lw],
                              w[:, 2 * lw + 3 * DN_WIDTH:2 * lw + 4 * DN_WIDTH]], axis=1).astype(BF16)
    w_ba = jnp.pad(w[:, 2 * lw + 4 * DN_WIDTH:], ((0, 0), (0, LANES - 2 * DN_HEADS))).astype(BF16)
    mix_g = mix_norm[l][None, :]
    conv_w = _pad_rows(jnp.concatenate([lru_conv_w[l], dn_conv_w[l]], axis=1), SUBLANES)
    lru_p = _pad_rows(jnp.stack([lru_conv_b[l], lru_b_r[l], lru_b_i[l], lru_lambda[l],
                                 lru_out_norm[l]]), SUBLANES)
    hb = LRU_BLOCKS // 2
    w_gate = jnp.stack([
        jnp.concatenate([_block_diag(lru_w_r[l][h * hb:(h + 1) * hb]),
                         _block_diag(lru_w_i[l][h * hb:(h + 1) * hb])], axis=1)
        for h in range(2)]).astype(BF16)
    head_p = jnp.zeros((SUBLANES, LANES), F32)
    head_p = head_p.at[0, DN_HEADS:2 * DN_HEADS].set(dn_a_log[l])
    head_p = head_p.at[1, DN_HEADS:2 * DN_HEADS].set(dn_dt_bias[l])
    dn_on = dn_out_norm[l][None, :]
    w_o = w_out[l].astype(BF16)
    ffn_g = ffn_norm[l][None, :]
    w_router = jnp.zeros((ROUTE_ROWS, D_MODEL), F32)
    w_router = w_router.at[:N_GROUPS].set(router_group_w[l].T)
    w_router = w_router.at[SUBLANES:SUBLANES + N_EXPERTS].set(router_expert_w[l].T).astype(BF16)
    b_router = jnp.zeros((ROUTE_ROWS,), F32).at[:N_GROUPS].set(router_group_b[l])
    b_router = b_router.at[SUBLANES:SUBLANES + N_EXPERTS].set(router_expert_b[l])
    b_router = jnp.broadcast_to(b_router[:, None], (ROUTE_ROWS, LANES))
    final_g = final_norm[None, :]
    weights = (mix_g, w_main, w_ba, conv_w, lru_p, w_gate, head_p, dn_on, w_o, ffn_g, w_router,
               b_router)

    meta_pad = CHUNK - N_META
    prefix = jnp.pad(meta_tokens, ((meta_pad, 0), (0, 0)))[None]
    zeros = lambda *s: jnp.zeros(s, F32)
    *_, h_meta, s_meta, tail_meta = _mixer(
        prefix, zeros(SUBLANES, CONV_COLS), zeros(SUBLANES, LRU_WIDTH),
        zeros(DN_HEADS, DN_HEAD_DIM, DN_HEAD_DIM), weights, CHUNK, meta_pad)

    n_parts = BATCH_PARTS if bsz % BATCH_PARTS == 0 else 1
    pb = bsz // n_parts
    m = pb * seq
    parts = []
    for p in range(n_parts):
        h1, xn, route, cnt, _, _, _ = _mixer(x, tail_meta[0], h_meta[0], s_meta[0], weights,
                                             256, 0, b0=p * pb, bsz=pb)
        eflat = jnp.transpose(route[:, 0:2, :], (1, 0, 2)).reshape(2 * m).astype(jnp.int32)
        gates = jnp.transpose(route[:, 2:4, :], (0, 2, 1)).reshape(m, 2)
        counts = jnp.sum(cnt, axis=(0, 2)).astype(jnp.int32)
        buf_tok, dest, block_expert, n_used = _dispatch_tables(eflat, counts, m)
        x_buf = _row_gather(xn.reshape(m, d // 2), buf_tok)
        y_buf = _experts(block_expert, n_used, x_buf, moe_w_gate[l], moe_w_up[l], moe_w_down[l])
        y_tok = _row_gather(y_buf, dest)
        parts.append((h1.reshape(m, d), y_tok, gates))
    out = _combine(parts, final_g, 512)
    return out.reshape(bsz, seq, d)
```

```python
import functools

import jax
import jax.numpy as jnp
from jax import lax
from jax.experimental import pallas as pl
from jax.experimental.pallas import tpu as pltpu
from jax.experimental.pallas import tpu_sc as plsc

F32 = jnp.float32
BF16 = jnp.bfloat16

D_MODEL = 1024
N_META = 16
CHUNK = 64
CONV_WIDTH = 4
LRU_WIDTH = 512
LRU_BLOCKS = 8
LRU_C = 8.0
DN_HEADS = 4
DN_HEAD_DIM = 128
DN_WIDTH = DN_HEADS * DN_HEAD_DIM
N_GROUPS = 4
EXPERTS_PER_GROUP = 8
N_EXPERTS = N_GROUPS * EXPERTS_PER_GROUP
D_EXPERT = 256
EPS = 1e-6
CONV_COLS = LRU_WIDTH + 3 * DN_WIDTH
GATE_COLS = LRU_WIDTH + DN_WIDTH
LANES = 128
SUBLANES = 8
INV_BLOCK = 16
MOE_ROWS = 512
ROUTE_ROWS = 48
VMEM_LIMIT = 56 * 1024 * 1024
GATHER_WINDOW = 64


def _cparams(*sem):
    return pltpu.CompilerParams(dimension_semantics=sem, vmem_limit_bytes=VMEM_LIMIT)


def _sigmoid(x):
    return 0.5 * jnp.tanh(0.5 * x) + 0.5


def _pack_bf16_pairs(x):
    c = x.shape[1] // 2

    def rne(v):
        b = lax.bitcast_convert_type(v, jnp.uint32)
        return b + jnp.uint32(0x7FFF) + ((b >> 16) & jnp.uint32(1))

    return (rne(x[:, :c]) >> 16) | (rne(x[:, c:]) & jnp.uint32(0xFFFF0000))


def _unpack_bf16_pairs(w):
    lo = lax.bitcast_convert_type(w << 16, F32)
    hi = lax.bitcast_convert_type(w & jnp.uint32(0xFFFF0000), F32)
    return jnp.concatenate([lo, hi], axis=1)


def _softplus(x):
    return jnp.maximum(x, 0.0) + jnp.log(1.0 + jnp.exp(-jnp.abs(x)))


def _gelu_tanh(x):
    return 0.5 * x * (1.0 + jnp.tanh(0.7978845608028654 * (x + 0.044715 * (x * x * x))))


def _mm(a, b):
    return jnp.dot(a.astype(BF16), b.astype(BF16), preferred_element_type=F32)


def _mm_nt(a, b):
    return lax.dot_general(a.astype(BF16), b.astype(BF16), (((1,), (1,)), ((), ())),
                           preferred_element_type=F32)


def _mm_tn(a, b):
    return lax.dot_general(a.astype(BF16), b.astype(BF16), (((0,), (0,)), ((), ())),
                           preferred_element_type=F32)


def _lru_branch(xc, gate, p_ref, wg_ref, h_sc, valid):
    tb = xc.shape[0]
    b_r, b_i = p_ref[1:2, :], p_ref[2:3, :]
    lam, out_g = p_ref[3:4, :], p_ref[4:5, :]
    half = LRU_WIDTH // 2
    xcb = xc.astype(BF16)
    g0 = jnp.dot(xcb[:, :half], wg_ref[0], preferred_element_type=F32)
    g1 = jnp.dot(xcb[:, half:], wg_ref[1], preferred_element_type=F32)
    r = _sigmoid(jnp.concatenate([g0[:, :half], g1[:, :half]], axis=1) + b_r)
    i = _sigmoid(jnp.concatenate([g0[:, half:], g1[:, half:]], axis=1) + b_i)
    log_a = (-LRU_C) * r * _softplus(-lam)
    a = jnp.exp(log_a)
    var = 1.0 - jnp.exp(2.0 * log_a)
    b = jnp.where(var > 0.0, var * lax.rsqrt(var), 0.0) * (i * xc)
    if valid is not None:
        b = jnp.where(valid, b, 0.0)
        a = jnp.where(valid, a, 1.0)

    sub = lax.broadcasted_iota(jnp.int32, (tb, LRU_WIDTH), 0) & (SUBLANES - 1)
    for s in (1, 2, 4):
        keep = sub >= s
        b = jnp.where(keep, a * pltpu.roll(b, s, axis=0), 0.0) + b
        a = jnp.where(keep, a * pltpu.roll(a, s, axis=0), a)
    h = h_sc[...]
    hs = []
    for g in range(tb // SUBLANES):
        hg = a[g * SUBLANES:(g + 1) * SUBLANES] * h + b[g * SUBLANES:(g + 1) * SUBLANES]
        hs.append(hg)
        h = jnp.broadcast_to(hg[SUBLANES - 1:SUBLANES, :], (SUBLANES, LRU_WIDTH))
    h_sc[...] = h
    out = jnp.concatenate(hs, axis=0) * _gelu_tanh(gate)
    ms = jnp.mean(out * out, axis=-1, keepdims=True)
    return (out * lax.rsqrt(ms + EPS)) * out_g


def _pair_mm(x, y, half_masks):
    yb = y.astype(BF16)
    rhs = jnp.concatenate([yb * half_masks[0], yb * half_masks[1]], axis=0)
    return jnp.dot(x.astype(BF16), rhs, preferred_element_type=F32)


def _pair_inverse(a_list, eye, same16, half_masks):
    def mm(xs, ys):
        return [_pair_mm(x, y, half_masks) for x, y in zip(xs, ys)]

    def plus(xs):
        return [eye + x for x in xs]

    def minus(xs):
        return [eye - x for x in xs]

    d = [jnp.where(same16, a, 0.0) for a in a_list]
    d2 = mm(d, d)
    d4 = mm(d2, d2)
    d8 = mm(d4, d4)
    p = mm(mm(mm(minus(d), plus(d2)), plus(d4)), plus(d8))
    m = mm(p, [a - x for a, x in zip(a_list, d)])
    q = mm(minus(m), plus(mm(m, m)))
    return mm(q, p)


def _dn_branch(act, z, ba, hp_ref, on_ref, s_sc, y_sc, valid):
    tb = act.shape[0]
    hd = DN_HEAD_DIM
    act = act * _sigmoid(act)
    beta_t = _sigmoid(ba)
    g_t = pltpu.roll(-jnp.exp(hp_ref[0:1, :]) * _softplus(ba + hp_ref[1:2, :]),
                     LANES - DN_HEADS, axis=1)
    if valid is not None:
        act = jnp.where(valid, act, 0.0)
        beta_t = jnp.where(valid, beta_t, 0.0)
        g_t = jnp.where(valid, g_t, 0.0)

    heads = []
    for i in range(3 * DN_HEADS):
        seg = act[:, i * hd:(i + 1) * hd]
        if i < 2 * DN_HEADS:
            nrm = lax.rsqrt(jnp.sum(seg * seg, axis=-1, keepdims=True) + EPS)
            if i < DN_HEADS:
                nrm = nrm * (hd ** -0.5)
            seg = seg * nrm
        heads.append(seg)
    qn, kn, vv = heads[:DN_HEADS], heads[DN_HEADS:2 * DN_HEADS], heads[2 * DN_HEADS:]

    row_c = lax.broadcasted_iota(jnp.int32, (tb, LANES), 0) & (CHUNK - 1)
    cum = g_t
    s = 1
    while s < CHUNK:
        cum = cum + jnp.where(row_c >= s, pltpu.roll(cum, s, axis=0), 0.0)
        s *= 2
    e_cum = jnp.exp(cum)

    ri = lax.broadcasted_iota(jnp.int32, (CHUNK, LANES), 0)
    li = lax.broadcasted_iota(jnp.int32, (CHUNK, LANES), 1)
    lo_half = li < CHUNK
    half_masks = (lo_half.astype(BF16), (li >= CHUNK).astype(BF16))
    cj = li & (CHUNK - 1)
    eye_b = ri == cj
    eye = eye_b.astype(F32)
    causal = ri >= cj
    strict = ri > cj
    same16 = (ri // INV_BLOCK) == (cj // INV_BLOCK)
    bd_mask = ((lax.broadcasted_iota(jnp.int32, (2 * hd, 2 * hd), 0) >= hd)
               == (lax.broadcasted_iota(jnp.int32, (2 * hd, 2 * hd), 1) >= hd))
    zero_c = jnp.zeros((CHUNK, hd), F32)
    out_g = on_ref[...]

    def bd_rows(x0, x1):
        z0 = jnp.zeros_like(x0)
        return jnp.concatenate([jnp.concatenate([x0, z0], axis=1),
                                jnp.concatenate([z0, x1], axis=1)], axis=0)

    n_ch = tb // CHUNK
    n_hp = DN_HEADS // 2
    probs = [(ch, hp) for ch in range(n_ch) for hp in range(n_hp)]
    qkm, q_dec, k_dec, rhs, a_list, dec_row = [], [], [], [], [], []
    for ch, hp in probs:
        rows = slice(ch * CHUNK, (ch + 1) * CHUNK)
        h0, h1 = 2 * hp, 2 * hp + 1
        cum_c, beta_c, ecum_c = cum[rows], beta_t[rows], e_cum[rows]
        last = cum_c[CHUNK - 1:CHUNK, :]

        def tile(arr):
            return jnp.where(lo_half, jnp.broadcast_to(arr[:, h0:h0 + 1], (CHUNK, LANES)),
                             jnp.broadcast_to(arr[:, h1:h1 + 1], (CHUNK, LANES)))

        def wide(arr):
            return jnp.concatenate([jnp.broadcast_to(arr[:, h0:h0 + 1], (arr.shape[0], hd)),
                                    jnp.broadcast_to(arr[:, h1:h1 + 1], (arr.shape[0], hd))],
                                   axis=1)

        cum_cp = tile(cum_c)
        cum_rp = jnp.sum(jnp.where(eye_b, cum_cp, 0.0), axis=0, keepdims=True)
        decay = jnp.where(causal, jnp.exp(jnp.where(causal, cum_cp - cum_rp, 0.0)), 0.0)
        q_p = jnp.concatenate([qn[h0][rows], qn[h1][rows]], axis=1)
        k_p = jnp.concatenate([kn[h0][rows], kn[h1][rows]], axis=1)
        v_p = jnp.concatenate([vv[h0][rows], vv[h1][rows]], axis=1)
        qkk = _mm_nt(jnp.concatenate([q_p, k_p], axis=0),
                     bd_rows(kn[h0][rows], kn[h1][rows]))
        qkm.append(qkk[:CHUNK] * decay)
        a_list.append(jnp.where(strict, qkk[CHUNK:] * decay, 0.0) * tile(beta_c))
        vb = v_p * wide(beta_c)
        kb = k_p * wide(beta_c * ecum_c)
        rhs.append(jnp.concatenate(
            [jnp.concatenate([vb[:, :hd], zero_c, kb[:, :hd], zero_c], axis=1),
             jnp.concatenate([zero_c, vb[:, hd:], zero_c, kb[:, hd:]], axis=1)], axis=0))
        q_dec.append(q_p * wide(ecum_c))
        k_dec.append(k_p * wide(jnp.exp(last - cum_c)))
        dec_row.append(wide(jnp.exp(last)))

    t_inv = _pair_inverse(a_list, eye, same16, half_masks)
    sol = [_mm(ti, r) for ti, r in zip(t_inv, rhs)]
    ktuw = [_mm_tn(kd, so) for kd, so in zip(k_dec, sol)]
    quw = [_mm(qm, jnp.concatenate(
               [jnp.concatenate([so[:, :hd], zero_c, so[:, 2 * hd:3 * hd], zero_c], axis=1),
                jnp.concatenate([zero_c, so[:, hd:2 * hd], zero_c, so[:, 3 * hd:]], axis=1)], axis=0))
           for qm, so in zip(qkm, sol)]

    state = [s_sc[hp] for hp in range(n_hp)]
    for ch in range(n_ch):
        rows = slice(ch * CHUNK, (ch + 1) * CHUNK)
        res = []
        for hp in range(n_hp):
            i = ch * n_hp + hp
            k_w = jnp.where(bd_mask, ktuw[i][:, 2 * hd:], 0.0)
            q_eff = q_dec[i] - quw[i][:, 2 * hd:]
            res.append(_mm(jnp.concatenate([k_w, q_eff], axis=0), state[hp]))
        for hp in range(n_hp):
            i = ch * n_hp + hp
            o_p = res[hp][2 * hd:] + quw[i][:, :2 * hd]
            state[hp] = (state[hp] * dec_row[i] - res[hp][:2 * hd]
                         + jnp.where(bd_mask, ktuw[i][:, :2 * hd], 0.0))
            for e in range(2):
                lo = (2 * hp + e) * hd
                o = o_p[:, e * hd:(e + 1) * hd]
                ms = jnp.mean(o * o, axis=-1, keepdims=True)
                zz = z[rows, lo:lo + hd]
                y_sc[rows, lo:lo + hd] = ((o * lax.rsqrt(ms + EPS)) * out_g) * (zz * _sigmoid(zz))
    for hp in range(n_hp):
        s_sc[hp] = state[hp]


def _route(logits_t):
    tm = logits_t.shape[1]
    row = lax.broadcasted_iota(jnp.int32, (SUBLANES, tm), 0).astype(F32)
    neg = -jnp.inf
    big = 1e9
    lg = jnp.where(row < N_GROUPS, logits_t[0:SUBLANES], neg)
    mg = jnp.max(lg, axis=0, keepdims=True)
    p_sel = 1.0 / jnp.sum(jnp.exp(lg - mg), axis=0, keepdims=True)
    g_sel = jnp.min(jnp.where(lg == mg, row, big), axis=0, keepdims=True)
    le = logits_t[SUBLANES:2 * SUBLANES]
    for g in range(1, N_GROUPS):
        le = jnp.where(g_sel == float(g), logits_t[(g + 1) * SUBLANES:(g + 2) * SUBLANES], le)
    m1 = jnp.max(le, axis=0, keepdims=True)
    i1 = jnp.min(jnp.where(le == m1, row, big), axis=0, keepdims=True)
    le2 = jnp.where(row == i1, neg, le)
    m2 = jnp.max(le2, axis=0, keepdims=True)
    i2 = jnp.min(jnp.where(le2 == m2, row, big), axis=0, keepdims=True)
    e21 = jnp.exp(m2 - m1)
    w1 = p_sel / (1.0 + e21)
    w2 = p_sel * e21 / (1.0 + e21)
    base = g_sel * float(EXPERTS_PER_GROUP)
    return jnp.where(row == 0.0, base + i1,
                     jnp.where(row == 1.0, base + i2,
                               jnp.where(row == 2.0, w1, jnp.where(row == 3.0, w2, 0.0))))


def _mixer_kernel(x_ref, tail_ref, h0_ref, s0_ref, mg_ref, wm_ref, wba_ref, cw_ref, lp_ref, wg_ref,
                  hp_ref, on_ref, wo_ref, fg_ref, wr_ref, br_ref,
                  h1_ref, xn_ref, rt_ref, cn_ref, hl_ref, sl_ref, tl_ref,
                  h_sc, s_sc, cs_sc, y_sc, cnt_sc, *, tb, pad):
    t = pl.program_id(1)
    hd = DN_HEAD_DIM

    @pl.when(t == 0)
    def _():
        zero_hd = jnp.zeros((hd, hd), F32)
        h_sc[...] = h0_ref[...]
        for hp in range(DN_HEADS // 2):
            s_sc[hp] = jnp.concatenate(
                [jnp.concatenate([s0_ref[2 * hp], zero_hd], axis=1),
                 jnp.concatenate([zero_hd, s0_ref[2 * hp + 1]], axis=1)], axis=0)
        cs_sc[0:SUBLANES, :] = tail_ref[...]

    x = x_ref[...]
    ms = jnp.mean(x * x, axis=-1, keepdims=True)
    u = ((x * lax.rsqrt(ms + EPS)) * mg_ref[...]).astype(BF16)
    valid = None
    if pad:
        valid = lax.broadcasted_iota(jnp.int32, (tb, 1), 0) >= pad

    for n in range(0, CONV_COLS, 512):
        cs_sc[SUBLANES:SUBLANES + tb, n:n + 512] = jnp.dot(u, wm_ref[:, n:n + 512],
                                                           preferred_element_type=F32)
    gate = jnp.dot(u, wm_ref[:, CONV_COLS:CONV_COLS + LRU_WIDTH], preferred_element_type=F32)
    z = jnp.dot(u, wm_ref[:, CONV_COLS + LRU_WIDTH:], preferred_element_type=F32)
    ba = jnp.dot(u, wba_ref[...], preferred_element_type=F32)

    def conv(lo, width):
        acc = cs_sc[SUBLANES:SUBLANES + tb, lo:lo + width] * cw_ref[3:4, lo:lo + width]
        for d in range(1, CONV_WIDTH):
            acc = acc + (cs_sc[SUBLANES - d:SUBLANES - d + tb, lo:lo + width]
                         * cw_ref[3 - d:4 - d, lo:lo + width])
        return acc

    xc = conv(0, LRU_WIDTH) + lp_ref[0:1, :]
    act = conv(LRU_WIDTH, 3 * DN_WIDTH)
    cs_sc[0:SUBLANES, :] = cs_sc[tb:tb + SUBLANES, :]

    y_lru = _lru_branch(xc, gate, lp_ref, wg_ref, h_sc, valid)
    _dn_branch(act, z, ba, hp_ref, on_ref, s_sc, y_sc, valid)

    mix = jnp.dot(y_lru.astype(BF16), wo_ref[:LRU_WIDTH, :], preferred_element_type=F32)
    mix = mix + jnp.dot(y_sc[...].astype(BF16), wo_ref[LRU_WIDTH:, :], preferred_element_type=F32)
    h1 = x + mix
    h1_ref[...] = h1
    ms1 = jnp.mean(h1 * h1, axis=-1, keepdims=True)
    xn = (h1 * lax.rsqrt(ms1 + EPS)) * fg_ref[...]
    xn_ref[...] = _pack_bf16_pairs(xn)
    logits_t = lax.dot_general(wr_ref[...], xn.astype(BF16), (((1,), (1,)), ((), ())),
                               preferred_element_type=F32) + br_ref[:, 0:1]
    route = _route(logits_t)
    rt_ref[...] = route

    erow = lax.broadcasted_iota(jnp.int32, (N_EXPERTS, tb), 0).astype(F32)
    hist = (erow == route[0:1, :]).astype(F32) + (erow == route[1:2, :]).astype(F32)
    if pad:
        hist = jnp.where(lax.broadcasted_iota(jnp.int32, (1, tb), 1) >= pad, hist, 0.0)

    @pl.when(t == 0)
    def _():
        cnt_sc[...] = hist

    @pl.when(t > 0)
    def _():
        cnt_sc[...] = cnt_sc[...] + hist

    @pl.when(t == pl.num_programs(1) - 1)
    def _():
        cn_ref[...] = cnt_sc[...]
        hl_ref[...] = h_sc[...]
        for h in range(DN_HEADS):
            e = h % 2
            sl_ref[h] = s_sc[h // 2][e * hd:(e + 1) * hd, e * hd:(e + 1) * hd]
        tl_ref[...] = cs_sc[0:SUBLANES, :]


def _mixer(x3, tail8, h0, s0, weights, tb, pad):
    bsz, t, _ = x3.shape
    assert tb % CHUNK == 0 and t % tb == 0 and (pad == 0 or t == tb)
    fix2 = lambda b, i: (0, 0)
    fix3 = lambda b, i: (0, 0, 0)
    blk = lambda b, i: (b, i, 0)
    per_b = lambda b, i: (b, 0, 0)
    w_specs = [pl.BlockSpec(w.shape, fix2 if w.ndim == 2 else fix3) for w in weights]
    return pl.pallas_call(
        functools.partial(_mixer_kernel, tb=tb, pad=pad),
        grid=(bsz, t // tb),
        in_specs=[pl.BlockSpec((None, tb, D_MODEL), blk),
                  pl.BlockSpec((SUBLANES, CONV_COLS), fix2),
                  pl.BlockSpec((SUBLANES, LRU_WIDTH), fix2),
                  pl.BlockSpec((DN_HEADS, DN_HEAD_DIM, DN_HEAD_DIM), fix3)] + w_specs,
        out_specs=[pl.BlockSpec((None, tb, D_MODEL), blk),
                   pl.BlockSpec((None, tb, D_MODEL // 2), blk),
                   pl.BlockSpec((None, SUBLANES, tb), lambda b, i: (b, 0, i)),
                   pl.BlockSpec((None, N_EXPERTS, tb), per_b),
                   pl.BlockSpec((None, SUBLANES, LRU_WIDTH), per_b),
                   pl.BlockSpec((None, DN_HEADS, DN_HEAD_DIM, DN_HEAD_DIM),
                                lambda b, i: (b, 0, 0, 0)),
                   pl.BlockSpec((None, SUBLANES, CONV_COLS), per_b)],
        out_shape=[jax.ShapeDtypeStruct((bsz, t, D_MODEL), F32),
                   jax.ShapeDtypeStruct((bsz, t, D_MODEL // 2), jnp.uint32),
                   jax.ShapeDtypeStruct((bsz, SUBLANES, t), F32),
                   jax.ShapeDtypeStruct((bsz, N_EXPERTS, tb), F32),
                   jax.ShapeDtypeStruct((bsz, SUBLANES, LRU_WIDTH), F32),
                   jax.ShapeDtypeStruct((bsz, DN_HEADS, DN_HEAD_DIM, DN_HEAD_DIM), F32),
                   jax.ShapeDtypeStruct((bsz, SUBLANES, CONV_COLS), F32)],
        scratch_shapes=[pltpu.VMEM((SUBLANES, LRU_WIDTH), F32),
                        pltpu.VMEM((DN_HEADS // 2, 2 * DN_HEAD_DIM, 2 * DN_HEAD_DIM), F32),
                        pltpu.VMEM((tb + SUBLANES, CONV_COLS), F32),
                        pltpu.VMEM((tb, DN_WIDTH), F32),
                        pltpu.VMEM((N_EXPERTS, tb), F32)],
        compiler_params=_cparams("parallel", "arbitrary"),
        name="mixer",
    )(x3, tail8, h0, s0, *weights)


def _expert_kernel(be_ref, x_ref, wg_ref, wu_ref, wd_ref, y_ref, wgu_sc, wd_sc):
    i = pl.program_id(0)
    prev = be_ref[jnp.maximum(i - 1, 0)]

    @pl.when((i == 0) | (be_ref[i] != prev))
    def _():
        wgu_sc[:, :D_EXPERT] = wg_ref[...].astype(BF16)
        wgu_sc[:, D_EXPERT:] = wu_ref[...].astype(BF16)
        wd_sc[...] = wd_ref[...].astype(BF16)

    x = _unpack_bf16_pairs(x_ref[...]).astype(BF16)
    gu = jnp.dot(x, wgu_sc[...], preferred_element_type=F32)
    g, u = gu[:, :D_EXPERT], gu[:, D_EXPERT:]
    hmid = (g * _sigmoid(g)) * u
    y_ref[...] = _pack_bf16_pairs(jnp.dot(hmid.astype(BF16), wd_sc[...],
                                          preferred_element_type=F32))


def _experts(block_expert, x_buf, w_gate, w_up, w_down):
    cap = x_buf.shape[0]
    by_expert = lambda i, be: (be[i], 0, 0)
    grid_spec = pltpu.PrefetchScalarGridSpec(
        num_scalar_prefetch=1,
        grid=(cap // MOE_ROWS,),
        in_specs=[pl.BlockSpec((MOE_ROWS, D_MODEL // 2), lambda i, be: (i, 0)),
                  pl.BlockSpec((None, D_MODEL, D_EXPERT), by_expert),
                  pl.BlockSpec((None, D_MODEL, D_EXPERT), by_expert),
                  pl.BlockSpec((None, D_EXPERT, D_MODEL), by_expert)],
        out_specs=pl.BlockSpec((MOE_ROWS, D_MODEL // 2), lambda i, be: (i, 0)),
        scratch_shapes=[pltpu.VMEM((D_MODEL, 2 * D_EXPERT), BF16),
                        pltpu.VMEM((D_EXPERT, D_MODEL), BF16)],
    )
    return pl.pallas_call(
        _expert_kernel,
        grid_spec=grid_spec,
        out_shape=jax.ShapeDtypeStruct((cap, D_MODEL // 2), jnp.uint32),
        compiler_params=_cparams("arbitrary"),
        name="experts",
    )(block_expert, x_buf, w_gate, w_up, w_down)


def _row_gather(table, idx):
    n_idx = idx.shape[0]
    d = table.shape[1]
    sc = plsc.get_sparse_core_info()
    n_workers = sc.num_cores * sc.num_subcores
    w = GATHER_WINDOW
    per_w = n_idx // n_workers
    n_steps = per_w // w
    assert per_w * n_workers == n_idx and n_steps * w == per_w and n_steps % 2 == 0
    mesh = plsc.VectorSubcoreMesh(core_axis_name="core", subcore_axis_name="subcore")

    @functools.partial(
        pl.kernel, out_type=jax.ShapeDtypeStruct((n_idx, d), table.dtype), mesh=mesh,
        scratch_types=[pltpu.VMEM((per_w,), jnp.int32),
                       pltpu.VMEM((2, w, d), table.dtype),
                       pltpu.SemaphoreType.DMA((2,)),
                       pltpu.SemaphoreType.DMA((2,))])
    def gather(x_hbm, i_hbm, o_hbm, idx_v, rows_v, g_sem, w_sem):
        wid = lax.axis_index("subcore") * sc.num_cores + lax.axis_index("core")
        base = wid * per_w
        pltpu.sync_copy(i_hbm.at[pl.ds(base, per_w)], idx_v)

        def fetch(s, b):
            return pltpu.make_async_copy(x_hbm.at[idx_v.at[pl.ds(s * w, w)]], rows_v.at[b],
                                         g_sem.at[b])

        def flush(s, b):
            return pltpu.make_async_copy(rows_v.at[b], o_hbm.at[pl.ds(base + s * w, w)],
                                         w_sem.at[b])

        fetch(0, 0).start()

        @pl.loop(0, n_steps, step=2)
        def _(s2):
            for b in range(2):
                s = s2 + b

                @pl.when(s + 1 < n_steps)
                def _():
                    @pl.when(s >= 1)
                    def _():
                        flush(s - 1, 1 - b).wait()
                    fetch(s + 1, 1 - b).start()

                fetch(s, b).wait()
                flush(s, b).start()

        flush(n_steps - 2, 0).wait()
        flush(n_steps - 1, 1).wait()

    return gather(table, idx)


def _combine_kernel(h1_ref, y0_ref, y1_ref, gt_ref, fg_ref, o_ref):
    gt = gt_ref[...]
    h = (h1_ref[...] + gt[:, 0:1] * _unpack_bf16_pairs(y0_ref[...])
         + gt[:, 1:2] * _unpack_bf16_pairs(y1_ref[...]))
    ms = jnp.mean(h * h, axis=-1, keepdims=True)
    o_ref[...] = (h * lax.rsqrt(ms + EPS)) * fg_ref[...]


def _combine(h1, y_tok, gates, final_g, tm):
    n = h1.shape[0]
    row = lambda i: (i, 0)
    return pl.pallas_call(
        _combine_kernel,
        grid=(n // tm,),
        in_specs=[pl.BlockSpec((tm, D_MODEL), row),
                  pl.BlockSpec((tm, D_MODEL // 2), row),
                  pl.BlockSpec((tm, D_MODEL // 2), lambda i: (i + n // tm, 0)),
                  pl.BlockSpec((tm, 2), row),
                  pl.BlockSpec((1, D_MODEL), lambda i: (0, 0))],
        out_specs=pl.BlockSpec((tm, D_MODEL), row),
        out_shape=jax.ShapeDtypeStruct((n, D_MODEL), F32),
        compiler_params=_cparams("parallel"),
        name="combine",
    )(h1, y_tok, y_tok, gates, final_g)


def _block_diag(blocks):
    n, r, c = blocks.shape
    out = jnp.zeros((n * r, n * c), blocks.dtype)
    for i in range(n):
        out = out.at[i * r:(i + 1) * r, i * c:(i + 1) * c].set(blocks[i])
    return out


def _pad_rows(a, rows):
    return jnp.pad(a, ((0, rows - a.shape[0]), (0, 0)))


def _slot_kernel(off_ref, e_ref, o_ref):
    e = e_ref[...]
    rows, lanes = e.shape
    j = (lax.broadcasted_iota(jnp.int32, e.shape, 0) * lanes
         + lax.broadcasted_iota(jnp.int32, e.shape, 1))
    off = jnp.zeros_like(e)
    for k in range(N_EXPERTS):
        off = jnp.where(e == k, off_ref[k], off)
    o_ref[...] = j + off


def _sorted_slots(sorted_e, offset):
    rows = sorted_e.shape[0] // LANES
    grid_spec = pltpu.PrefetchScalarGridSpec(
        num_scalar_prefetch=1, grid=(1,),
        in_specs=[pl.BlockSpec((rows, LANES), lambda i, off: (0, 0))],
        out_specs=pl.BlockSpec((rows, LANES), lambda i, off: (0, 0)))
    return pl.pallas_call(
        _slot_kernel, grid_spec=grid_spec,
        out_shape=jax.ShapeDtypeStruct((rows, LANES), jnp.int32),
        compiler_params=_cparams("arbitrary"), name="sorted_slots",
    )(offset, sorted_e.reshape(rows, LANES)).reshape(-1)


def _dispatch_tables(eflat, counts, n):
    n_assign = 2 * n
    iota = jnp.arange(n_assign, dtype=jnp.int32)
    sorted_e, order = lax.sort((eflat, iota), num_keys=1, is_stable=True)
    starts = jnp.cumsum(counts) - counts
    padded = (counts + MOE_ROWS - 1) // MOE_ROWS * MOE_ROWS
    pend = jnp.cumsum(padded)
    pstart = pend - padded

    def slot_of_assignment():
        dest_sorted = _sorted_slots(sorted_e, pstart - starts)
        return lax.sort((order, dest_sorted), num_keys=1, is_stable=False)[1]

    n_blocks = -(-(n_assign + N_EXPERTS * (MOE_ROWS - 1)) // MOE_ROWS)
    block_start = jnp.arange(n_blocks, dtype=jnp.int32) * MOE_ROWS
    block_expert = jnp.minimum(jnp.sum(block_start[:, None] >= pend[None, :], axis=1),
                               N_EXPERTS - 1).astype(jnp.int32)
    in_e = (block_start - pstart[block_expert])[:, None] + jnp.arange(MOE_ROWS, dtype=jnp.int32)
    runs = order[jnp.clip(in_e + starts[block_expert][:, None], 0, n_assign - 1)]
    slot = block_start[:, None] + jnp.arange(MOE_ROWS, dtype=jnp.int32)
    buf_tok = jnp.where(in_e < counts[block_expert][:, None], runs, slot) % n
    return buf_tok.reshape(-1), block_expert, slot_of_assignment


def kernel(x, meta_tokens, mix_norm, w_in, lru_conv_w, lru_conv_b, lru_w_r, lru_b_r, lru_w_i,
           lru_b_i, lru_lambda, lru_out_norm, dn_conv_w, dn_a_log, dn_dt_bias, dn_out_norm, w_out,
           ffn_norm, router_group_w, router_group_b, router_expert_w, router_expert_b, moe_w_gate,
           moe_w_up, moe_w_down, final_norm):
    bsz, seq, d = x.shape
    n = bsz * seq
    l = 0

    lw = LRU_WIDTH
    w = w_in[l]
    w_main = jnp.concatenate([w[:, :lw], w[:, 2 * lw:2 * lw + 3 * DN_WIDTH], w[:, lw:2 * lw],
                              w[:, 2 * lw + 3 * DN_WIDTH:2 * lw + 4 * DN_WIDTH]], axis=1).astype(BF16)
    w_ba = jnp.pad(w[:, 2 * lw + 4 * DN_WIDTH:], ((0, 0), (0, LANES - 2 * DN_HEADS))).astype(BF16)
    mix_g = mix_norm[l][None, :]
    conv_w = _pad_rows(jnp.concatenate([lru_conv_w[l], dn_conv_w[l]], axis=1), SUBLANES)
    lru_p = _pad_rows(jnp.stack([lru_conv_b[l], lru_b_r[l], lru_b_i[l], lru_lambda[l],
                                 lru_out_norm[l]]), SUBLANES)
    hb = LRU_BLOCKS // 2
    w_gate = jnp.stack([
        jnp.concatenate([_block_diag(lru_w_r[l][h * hb:(h + 1) * hb]),
                         _block_diag(lru_w_i[l][h * hb:(h + 1) * hb])], axis=1)
        for h in range(2)]).astype(BF16)
    head_p = jnp.zeros((SUBLANES, LANES), F32)
    head_p = head_p.at[0, DN_HEADS:2 * DN_HEADS].set(dn_a_log[l])
    head_p = head_p.at[1, DN_HEADS:2 * DN_HEADS].set(dn_dt_bias[l])
    dn_on = dn_out_norm[l][None, :]
    w_o = w_out[l].astype(BF16)
    ffn_g = ffn_norm[l][None, :]
    w_router = jnp.zeros((ROUTE_ROWS, D_MODEL), F32)
    w_router = w_router.at[:N_GROUPS].set(router_group_w[l].T)
    w_router = w_router.at[SUBLANES:SUBLANES + N_EXPERTS].set(router_expert_w[l].T).astype(BF16)
    b_router = jnp.zeros((ROUTE_ROWS,), F32).at[:N_GROUPS].set(router_group_b[l])
    b_router = b_router.at[SUBLANES:SUBLANES + N_EXPERTS].set(router_expert_b[l])
    b_router = jnp.broadcast_to(b_router[:, None], (ROUTE_ROWS, LANES))
    final_g = final_norm[None, :]
    weights = (mix_g, w_main, w_ba, conv_w, lru_p, w_gate, head_p, dn_on, w_o, ffn_g, w_router,
               b_router)

    meta_pad = CHUNK - N_META
    prefix = jnp.pad(meta_tokens, ((meta_pad, 0), (0, 0)))[None]
    zeros = lambda *s: jnp.zeros(s, F32)
    *_, h_meta, s_meta, tail_meta = _mixer(
        prefix, zeros(SUBLANES, CONV_COLS), zeros(SUBLANES, LRU_WIDTH),
        zeros(DN_HEADS, DN_HEAD_DIM, DN_HEAD_DIM), weights, CHUNK, meta_pad)

    h1, xn, route, cnt, _, _, _ = _mixer(x, tail_meta[0], h_meta[0], s_meta[0], weights, 256, 0)
    h1 = h1.reshape(n, d)
    eflat = jnp.transpose(route[:, 0:2, :], (1, 0, 2)).reshape(2 * n).astype(jnp.int32)
    gates = jnp.transpose(route[:, 2:4, :], (0, 2, 1)).reshape(n, 2)
    counts = jnp.sum(cnt, axis=(0, 2)).astype(jnp.int32)

    buf_tok, block_expert, slot_of_assignment = _dispatch_tables(eflat, counts, n)
    x_buf = _row_gather(xn.reshape(n, d // 2), buf_tok)
    dest = slot_of_assignment()
    y_buf = _experts(block_expert, x_buf, moe_w_gate[l], moe_w_up[l], moe_w_down[l])
    y_tok = _row_gather(y_buf, dest)
    out = _combine(h1, y_tok, gates, final_g, 1024)
    return out.reshape(bsz, seq, d)
```

```python
import functools

import jax
import jax.numpy as jnp
from jax import lax
from jax.experimental import pallas as pl
from jax.experimental.pallas import tpu as pltpu
from jax.experimental.pallas import tpu_sc as plsc

F32 = jnp.float32
BF16 = jnp.bfloat16

D_MODEL = 1024
N_META = 16
CHUNK = 64
CONV_WIDTH = 4
LRU_WIDTH = 512
LRU_BLOCKS = 8
LRU_C = 8.0
DN_HEADS = 4
DN_HEAD_DIM = 128
DN_WIDTH = DN_HEADS * DN_HEAD_DIM
N_GROUPS = 4
EXPERTS_PER_GROUP = 8
N_EXPERTS = N_GROUPS * EXPERTS_PER_GROUP
D_EXPERT = 256
EPS = 1e-6
CONV_COLS = LRU_WIDTH + 3 * DN_WIDTH
GATE_COLS = LRU_WIDTH + DN_WIDTH
LANES = 128
SUBLANES = 8
INV_BLOCK = 16
MOE_ROWS = 512
ROUTE_ROWS = 48
VMEM_LIMIT = 56 * 1024 * 1024
GATHER_WINDOW = 64


def _cparams(*sem):
    return pltpu.CompilerParams(dimension_semantics=sem, vmem_limit_bytes=VMEM_LIMIT)


def _sigmoid(x):
    return 0.5 * jnp.tanh(0.5 * x) + 0.5


def _pack_bf16_pairs(x):
    c = x.shape[1] // 2

    def rne(v):
        b = lax.bitcast_convert_type(v, jnp.uint32)
        return b + jnp.uint32(0x7FFF) + ((b >> 16) & jnp.uint32(1))

    return (rne(x[:, :c]) >> 16) | (rne(x[:, c:]) & jnp.uint32(0xFFFF0000))


def _unpack_bf16_pairs(w):
    lo = lax.bitcast_convert_type(w << 16, F32)
    hi = lax.bitcast_convert_type(w & jnp.uint32(0xFFFF0000), F32)
    return jnp.concatenate([lo, hi], axis=1)


def _softplus(x):
    return jnp.maximum(x, 0.0) + jnp.log(1.0 + jnp.exp(-jnp.abs(x)))


def _gelu_tanh(x):
    return 0.5 * x * (1.0 + jnp.tanh(0.7978845608028654 * (x + 0.044715 * (x * x * x))))


def _mm(a, b):
    return jnp.dot(a.astype(BF16), b.astype(BF16), preferred_element_type=F32)


def _mm_nt(a, b):
    return lax.dot_general(a.astype(BF16), b.astype(BF16), (((1,), (1,)), ((), ())),
                           preferred_element_type=F32)


def _mm_tn(a, b):
    return lax.dot_general(a.astype(BF16), b.astype(BF16), (((0,), (0,)), ((), ())),
                           preferred_element_type=F32)


def _lru_branch(xc, gate, p_ref, wg_ref, h_sc, valid):
    tb = xc.shape[0]
    b_r, b_i = p_ref[1:2, :], p_ref[2:3, :]
    lam, out_g = p_ref[3:4, :], p_ref[4:5, :]
    half = LRU_WIDTH // 2
    xcb = xc.astype(BF16)
    g0 = jnp.dot(xcb[:, :half], wg_ref[0], preferred_element_type=F32)
    g1 = jnp.dot(xcb[:, half:], wg_ref[1], preferred_element_type=F32)
    r = _sigmoid(jnp.concatenate([g0[:, :half], g1[:, :half]], axis=1) + b_r)
    i = _sigmoid(jnp.concatenate([g0[:, half:], g1[:, half:]], axis=1) + b_i)
    log_a = (-LRU_C) * r * _softplus(-lam)
    a = jnp.exp(log_a)
    var = 1.0 - jnp.exp(2.0 * log_a)
    b = jnp.where(var > 0.0, var * lax.rsqrt(var), 0.0) * (i * xc)
    if valid is not None:
        b = jnp.where(valid, b, 0.0)
        a = jnp.where(valid, a, 1.0)

    sub = lax.broadcasted_iota(jnp.int32, (tb, LRU_WIDTH), 0) & (SUBLANES - 1)
    for s in (1, 2, 4):
        keep = sub >= s
        b = jnp.where(keep, a * pltpu.roll(b, s, axis=0), 0.0) + b
        a = jnp.where(keep, a * pltpu.roll(a, s, axis=0), a)
    h = h_sc[...]
    hs = []
    for g in range(tb // SUBLANES):
        hg = a[g * SUBLANES:(g + 1) * SUBLANES] * h + b[g * SUBLANES:(g + 1) * SUBLANES]
        hs.append(hg)
        h = jnp.broadcast_to(hg[SUBLANES - 1:SUBLANES, :], (SUBLANES, LRU_WIDTH))
    h_sc[...] = h
    out = jnp.concatenate(hs, axis=0) * _gelu_tanh(gate)
    ms = jnp.mean(out * out, axis=-1, keepdims=True)
    return (out * lax.rsqrt(ms + EPS)) * out_g


def _pair_mm(x, y, half_masks):
    yb = y.astype(BF16)
    rhs = jnp.concatenate([yb * half_masks[0], yb * half_masks[1]], axis=0)
    return jnp.dot(x.astype(BF16), rhs, preferred_element_type=F32)


def _pair_inverse(a_list, eye, same16, half_masks):
    def mm(xs, ys):
        return [_pair_mm(x, y, half_masks) for x, y in zip(xs, ys)]

    def plus(xs):
        return [eye + x for x in xs]

    def minus(xs):
        return [eye - x for x in xs]

    d = [jnp.where(same16, a, 0.0) for a in a_list]
    d2 = mm(d, d)
    d4 = mm(d2, d2)
    d8 = mm(d4, d4)
    p = mm(mm(mm(minus(d), plus(d2)), plus(d4)), plus(d8))
    m = mm(p, [a - x for a, x in zip(a_list, d)])
    q = mm(minus(m), plus(mm(m, m)))
    return mm(q, p)


def _dn_branch(act, z, ba, hp_ref, on_ref, s_sc, y_sc, valid):
    tb = act.shape[0]
    hd = DN_HEAD_DIM
    act = act * _sigmoid(act)
    beta_t = _sigmoid(ba)
    g_t = pltpu.roll(-jnp.exp(hp_ref[0:1, :]) * _softplus(ba + hp_ref[1:2, :]),
                     LANES - DN_HEADS, axis=1)
    if valid is not None:
        act = jnp.where(valid, act, 0.0)
        beta_t = jnp.where(valid, beta_t, 0.0)
        g_t = jnp.where(valid, g_t, 0.0)

    heads = []
    for i in range(3 * DN_HEADS):
        seg = act[:, i * hd:(i + 1) * hd]
        if i < 2 * DN_HEADS:
            nrm = lax.rsqrt(jnp.sum(seg * seg, axis=-1, keepdims=True) + EPS)
            if i < DN_HEADS:
                nrm = nrm * (hd ** -0.5)
            seg = seg * nrm
        heads.append(seg)
    qn, kn, vv = heads[:DN_HEADS], heads[DN_HEADS:2 * DN_HEADS], heads[2 * DN_HEADS:]

    row_c = lax.broadcasted_iota(jnp.int32, (tb, LANES), 0) & (CHUNK - 1)
    cum = g_t
    s = 1
    while s < CHUNK:
        cum = cum + jnp.where(row_c >= s, pltpu.roll(cum, s, axis=0), 0.0)
        s *= 2
    e_cum = jnp.exp(cum)

    ri = lax.broadcasted_iota(jnp.int32, (CHUNK, LANES), 0)
    li = lax.broadcasted_iota(jnp.int32, (CHUNK, LANES), 1)
    lo_half = li < CHUNK
    half_masks = (lo_half.astype(BF16), (li >= CHUNK).astype(BF16))
    cj = li & (CHUNK - 1)
    eye_b = ri == cj
    eye = eye_b.astype(F32)
    causal = ri >= cj
    strict = ri > cj
    same16 = (ri // INV_BLOCK) == (cj // INV_BLOCK)
    zero_c = jnp.zeros((CHUNK, hd), F32)
    out_g = on_ref[...]

    def bd_rows(x0, x1):
        z0 = jnp.zeros_like(x0)
        return jnp.concatenate([jnp.concatenate([x0, z0], axis=1),
                                jnp.concatenate([z0, x1], axis=1)], axis=0)

    n_ch = tb // CHUNK
    n_hp = DN_HEADS // 2
    probs = [(ch, hp) for ch in range(n_ch) for hp in range(n_hp)]
    qkm, q_dec, k_dec, rhs, a_list, dec_row = [], [], [], [], [], []
    for ch, hp in probs:
        rows = slice(ch * CHUNK, (ch + 1) * CHUNK)
        h0, h1 = 2 * hp, 2 * hp + 1
        cum_c, beta_c, ecum_c = cum[rows], beta_t[rows], e_cum[rows]
        last = cum_c[CHUNK - 1:CHUNK, :]

        def tile(arr):
            return jnp.where(lo_half, jnp.broadcast_to(arr[:, h0:h0 + 1], (CHUNK, LANES)),
                             jnp.broadcast_to(arr[:, h1:h1 + 1], (CHUNK, LANES)))

        def wide(arr):
            return jnp.concatenate([jnp.broadcast_to(arr[:, h0:h0 + 1], (arr.shape[0], hd)),
                                    jnp.broadcast_to(arr[:, h1:h1 + 1], (arr.shape[0], hd))],
                                   axis=1)

        cum_cp = tile(cum_c)
        cum_rp = jnp.sum(jnp.where(eye_b, cum_cp, 0.0), axis=0, keepdims=True)
        decay = jnp.where(causal, jnp.exp(jnp.where(causal, cum_cp - cum_rp, 0.0)), 0.0)
        q_p = jnp.concatenate([qn[h0][rows], qn[h1][rows]], axis=1)
        k_p = jnp.concatenate([kn[h0][rows], kn[h1][rows]], axis=1)
        v_p = jnp.concatenate([vv[h0][rows], vv[h1][rows]], axis=1)
        qkk = _mm_nt(jnp.concatenate([q_p, k_p], axis=0),
                     bd_rows(kn[h0][rows], kn[h1][rows]))
        qkm.append(qkk[:CHUNK] * decay)
        a_list.append(jnp.where(strict, qkk[CHUNK:] * decay, 0.0) * tile(beta_c))
        vb = v_p * wide(beta_c)
        kb = k_p * wide(beta_c * ecum_c)
        rhs.append(jnp.concatenate(
            [jnp.concatenate([vb[:, :hd], zero_c, kb[:, :hd], zero_c], axis=1),
             jnp.concatenate([zero_c, vb[:, hd:], zero_c, kb[:, hd:]], axis=1)], axis=0))
        q_dec.append(q_p * wide(ecum_c))
        k_dec.append(k_p * wide(jnp.exp(last - cum_c)))
        dec_row.append(wide(jnp.exp(last)))

    t_inv = _pair_inverse(a_list, eye, same16, half_masks)
    sol = [_mm(ti, r) for ti, r in zip(t_inv, rhs)]
    ktuw = [[_mm_tn(kd[:, e * hd:(e + 1) * hd],
                    jnp.concatenate([so[:, e * hd:(e + 1) * hd],
                                     so[:, (2 + e) * hd:(3 + e) * hd]], axis=1))
             for e in range(2)]
            for kd, so in zip(k_dec, sol)]
    quw = [_mm(qm, jnp.concatenate(
               [jnp.concatenate([so[:, :hd], zero_c, so[:, 2 * hd:3 * hd], zero_c], axis=1),
                jnp.concatenate([zero_c, so[:, hd:2 * hd], zero_c, so[:, 3 * hd:]], axis=1)], axis=0))
           for qm, so in zip(qkm, sol)]

    state = [s_sc[h] for h in range(DN_HEADS)]
    for ch in range(n_ch):
        rows = slice(ch * CHUNK, (ch + 1) * CHUNK)
        res = []
        for h in range(DN_HEADS):
            i, e = ch * n_hp + h // 2, h % 2
            q_eff = q_dec[i][:, e * hd:(e + 1) * hd] - quw[i][:, (2 + e) * hd:(3 + e) * hd]
            res.append(_mm(jnp.concatenate([ktuw[i][e][:, hd:], q_eff], axis=0), state[h]))
        for h in range(DN_HEADS):
            i, e = ch * n_hp + h // 2, h % 2
            o = res[h][hd:] + quw[i][:, e * hd:(e + 1) * hd]
            state[h] = (state[h] * dec_row[i][:, e * hd:(e + 1) * hd] - res[h][:hd]
                        + ktuw[i][e][:, :hd])
            ms = jnp.mean(o * o, axis=-1, keepdims=True)
            zz = z[rows, h * hd:(h + 1) * hd]
            y_sc[rows, h * hd:(h + 1) * hd] = (((o * lax.rsqrt(ms + EPS)) * out_g)
                                               * (zz * _sigmoid(zz)))
    for h in range(DN_HEADS):
        s_sc[h] = state[h]


def _route(logits_t):
    tm = logits_t.shape[1]
    row = lax.broadcasted_iota(jnp.int32, (SUBLANES, tm), 0).astype(F32)
    neg = -jnp.inf
    big = 1e9
    lg = jnp.where(row < N_GROUPS, logits_t[0:SUBLANES], neg)
    mg = jnp.max(lg, axis=0, keepdims=True)
    p_sel = 1.0 / jnp.sum(jnp.exp(lg - mg), axis=0, keepdims=True)
    g_sel = jnp.min(jnp.where(lg == mg, row, big), axis=0, keepdims=True)
    le = logits_t[SUBLANES:2 * SUBLANES]
    for g in range(1, N_GROUPS):
        le = jnp.where(g_sel == float(g), logits_t[(g + 1) * SUBLANES:(g + 2) * SUBLANES], le)
    m1 = jnp.max(le, axis=0, keepdims=True)
    i1 = jnp.min(jnp.where(le == m1, row, big), axis=0, keepdims=True)
    le2 = jnp.where(row == i1, neg, le)
    m2 = jnp.max(le2, axis=0, keepdims=True)
    i2 = jnp.min(jnp.where(le2 == m2, row, big), axis=0, keepdims=True)
    e21 = jnp.exp(m2 - m1)
    w1 = p_sel / (1.0 + e21)
    w2 = p_sel * e21 / (1.0 + e21)
    base = g_sel * float(EXPERTS_PER_GROUP)
    return jnp.where(row == 0.0, base + i1,
                     jnp.where(row == 1.0, base + i2,
                               jnp.where(row == 2.0, w1, jnp.where(row == 3.0, w2, 0.0))))


def _mixer_kernel(x_ref, tail_ref, h0_ref, s0_ref, mg_ref, wm_ref, wba_ref, cw_ref, lp_ref, wg_ref,
                  hp_ref, on_ref, wo_ref, fg_ref, wr_ref, br_ref,
                  h1_ref, xn_ref, rt_ref, cn_ref, hl_ref, sl_ref, tl_ref,
                  h_sc, s_sc, cs_sc, y_sc, cnt_sc, *, tb, pad):
    t = pl.program_id(1)

    @pl.when(t == 0)
    def _():
        h_sc[...] = h0_ref[...]
        s_sc[...] = s0_ref[...]
        cs_sc[0:SUBLANES, :] = tail_ref[...]

    x = x_ref[...]
    ms = jnp.mean(x * x, axis=-1, keepdims=True)
    u = ((x * lax.rsqrt(ms + EPS)) * mg_ref[...]).astype(BF16)
    valid = None
    if pad:
        valid = lax.broadcasted_iota(jnp.int32, (tb, 1), 0) >= pad

    for n in range(0, CONV_COLS, 512):
        cs_sc[SUBLANES:SUBLANES + tb, n:n + 512] = jnp.dot(u, wm_ref[:, n:n + 512],
                                                           preferred_element_type=F32)
    gate = jnp.dot(u, wm_ref[:, CONV_COLS:CONV_COLS + LRU_WIDTH], preferred_element_type=F32)
    z = jnp.dot(u, wm_ref[:, CONV_COLS + LRU_WIDTH:], preferred_element_type=F32)
    ba = jnp.dot(u, wba_ref[...], preferred_element_type=F32)

    def conv(lo, width):
        acc = cs_sc[SUBLANES:SUBLANES + tb, lo:lo + width] * cw_ref[3:4, lo:lo + width]
        for d in range(1, CONV_WIDTH):
            acc = acc + (cs_sc[SUBLANES - d:SUBLANES - d + tb, lo:lo + width]
                         * cw_ref[3 - d:4 - d, lo:lo + width])
        return acc

    xc = conv(0, LRU_WIDTH) + lp_ref[0:1, :]
    act = conv(LRU_WIDTH, 3 * DN_WIDTH)
    cs_sc[0:SUBLANES, :] = cs_sc[tb:tb + SUBLANES, :]

    y_lru = _lru_branch(xc, gate, lp_ref, wg_ref, h_sc, valid)
    _dn_branch(act, z, ba, hp_ref, on_ref, s_sc, y_sc, valid)

    mix = jnp.dot(y_lru.astype(BF16), wo_ref[:LRU_WIDTH, :], preferred_element_type=F32)
    mix = mix + jnp.dot(y_sc[...].astype(BF16), wo_ref[LRU_WIDTH:, :], preferred_element_type=F32)
    h1 = x + mix
    h1_ref[...] = h1
    ms1 = jnp.mean(h1 * h1, axis=-1, keepdims=True)
    xn = (h1 * lax.rsqrt(ms1 + EPS)) * fg_ref[...]
    xn_ref[...] = _pack_bf16_pairs(xn)
    logits_t = lax.dot_general(wr_ref[...], xn.astype(BF16), (((1,), (1,)), ((), ())),
                               preferred_element_type=F32) + br_ref[:, 0:1]
    route = _route(logits_t)
    rt_ref[...] = route

    erow = lax.broadcasted_iota(jnp.int32, (N_EXPERTS, tb), 0).astype(F32)
    hist = (erow == route[0:1, :]).astype(F32) + (erow == route[1:2, :]).astype(F32)
    if pad:
        hist = jnp.where(lax.broadcasted_iota(jnp.int32, (1, tb), 1) >= pad, hist, 0.0)

    @pl.when(t == 0)
    def _():
        cnt_sc[...] = hist

    @pl.when(t > 0)
    def _():
        cnt_sc[...] = cnt_sc[...] + hist

    @pl.when(t == pl.num_programs(1) - 1)
    def _():
        cn_ref[...] = cnt_sc[...]
        hl_ref[...] = h_sc[...]
        sl_ref[...] = s_sc[...]
        tl_ref[...] = cs_sc[0:SUBLANES, :]


def _mixer(x3, tail8, h0, s0, weights, tb, pad):
    bsz, t, _ = x3.shape
    assert tb % CHUNK == 0 and t % tb == 0 and (pad == 0 or t == tb)
    fix2 = lambda b, i: (0, 0)
    fix3 = lambda b, i: (0, 0, 0)
    blk = lambda b, i: (b, i, 0)
    per_b = lambda b, i: (b, 0, 0)
    w_specs = [pl.BlockSpec(w.shape, fix2 if w.ndim == 2 else fix3) for w in weights]
    return pl.pallas_call(
        functools.partial(_mixer_kernel, tb=tb, pad=pad),
        grid=(bsz, t // tb),
        in_specs=[pl.BlockSpec((None, tb, D_MODEL), blk),
                  pl.BlockSpec((SUBLANES, CONV_COLS), fix2),
                  pl.BlockSpec((SUBLANES, LRU_WIDTH), fix2),
                  pl.BlockSpec((DN_HEADS, DN_HEAD_DIM, DN_HEAD_DIM), fix3)] + w_specs,
        out_specs=[pl.BlockSpec((None, tb, D_MODEL), blk),
                   pl.BlockSpec((None, tb, D_MODEL // 2), blk),
                   pl.BlockSpec((None, SUBLANES, tb), lambda b, i: (b, 0, i)),
                   pl.BlockSpec((None, N_EXPERTS, tb), per_b),
                   pl.BlockSpec((None, SUBLANES, LRU_WIDTH), per_b),
                   pl.BlockSpec((None, DN_HEADS, DN_HEAD_DIM, DN_HEAD_DIM),
                                lambda b, i: (b, 0, 0, 0)),
                   pl.BlockSpec((None, SUBLANES, CONV_COLS), per_b)],
        out_shape=[jax.ShapeDtypeStruct((bsz, t, D_MODEL), F32),
                   jax.ShapeDtypeStruct((bsz, t, D_MODEL // 2), jnp.uint32),
                   jax.ShapeDtypeStruct((bsz, SUBLANES, t), F32),
                   jax.ShapeDtypeStruct((bsz, N_EXPERTS, tb), F32),
                   jax.ShapeDtypeStruct((bsz, SUBLANES, LRU_WIDTH), F32),
                   jax.ShapeDtypeStruct((bsz, DN_HEADS, DN_HEAD_DIM, DN_HEAD_DIM), F32),
                   jax.ShapeDtypeStruct((bsz, SUBLANES, CONV_COLS), F32)],
        scratch_shapes=[pltpu.VMEM((SUBLANES, LRU_WIDTH), F32),
                        pltpu.VMEM((DN_HEADS, DN_HEAD_DIM, DN_HEAD_DIM), F32),
                        pltpu.VMEM((tb + SUBLANES, CONV_COLS), F32),
                        pltpu.VMEM((tb, DN_WIDTH), F32),
                        pltpu.VMEM((N_EXPERTS, tb), F32)],
        compiler_params=_cparams("parallel", "arbitrary"),
        name="mixer",
    )(x3, tail8, h0, s0, *weights)


def _expert_kernel(be_ref, x_ref, wg_ref, wu_ref, wd_ref, y_ref, wgu_sc, wd_sc):
    i = pl.program_id(0)
    prev = be_ref[jnp.maximum(i - 1, 0)]

    @pl.when((i == 0) | (be_ref[i] != prev))
    def _():
        wgu_sc[:, :D_EXPERT] = wg_ref[...].astype(BF16)
        wgu_sc[:, D_EXPERT:] = wu_ref[...].astype(BF16)
        wd_sc[...] = wd_ref[...].astype(BF16)

    x = _unpack_bf16_pairs(x_ref[...]).astype(BF16)
    gu = jnp.dot(x, wgu_sc[...], preferred_element_type=F32)
    g, u = gu[:, :D_EXPERT], gu[:, D_EXPERT:]
    hmid = (g * _sigmoid(g)) * u
    y_ref[...] = _pack_bf16_pairs(jnp.dot(hmid.astype(BF16), wd_sc[...],
                                          preferred_element_type=F32))


def _experts(block_expert, x_buf, w_gate, w_up, w_down):
    cap = x_buf.shape[0]
    by_expert = lambda i, be: (be[i], 0, 0)
    grid_spec = pltpu.PrefetchScalarGridSpec(
        num_scalar_prefetch=1,
        grid=(cap // MOE_ROWS,),
        in_specs=[pl.BlockSpec((MOE_ROWS, D_MODEL // 2), lambda i, be: (i, 0)),
                  pl.BlockSpec((None, D_MODEL, D_EXPERT), by_expert),
                  pl.BlockSpec((None, D_MODEL, D_EXPERT), by_expert),
                  pl.BlockSpec((None, D_EXPERT, D_MODEL), by_expert)],
        out_specs=pl.BlockSpec((MOE_ROWS, D_MODEL // 2), lambda i, be: (i, 0)),
        scratch_shapes=[pltpu.VMEM((D_MODEL, 2 * D_EXPERT), BF16),
                        pltpu.VMEM((D_EXPERT, D_MODEL), BF16)],
    )
    return pl.pallas_call(
        _expert_kernel,
        grid_spec=grid_spec,
        out_shape=jax.ShapeDtypeStruct((cap, D_MODEL // 2), jnp.uint32),
        compiler_params=_cparams("arbitrary"),
        name="experts",
    )(block_expert, x_buf, w_gate, w_up, w_down)


def _row_gather(table, idx):
    n_idx = idx.shape[0]
    d = table.shape[1]
    sc = plsc.get_sparse_core_info()
    n_workers = sc.num_cores * sc.num_subcores
    w = GATHER_WINDOW
    per_w = n_idx // n_workers
    n_steps = per_w // w
    assert per_w * n_workers == n_idx and n_steps * w == per_w and n_steps % 2 == 0
    mesh = plsc.VectorSubcoreMesh(core_axis_name="core", subcore_axis_name="subcore")

    @functools.partial(
        pl.kernel, out_type=jax.ShapeDtypeStruct((n_idx, d), table.dtype), mesh=mesh,
        scratch_types=[pltpu.VMEM((per_w,), jnp.int32),
                       pltpu.VMEM((2, w, d), table.dtype),
                       pltpu.SemaphoreType.DMA((2,)),
                       pltpu.SemaphoreType.DMA((2,))])
    def gather(x_hbm, i_hbm, o_hbm, idx_v, rows_v, g_sem, w_sem):
        wid = lax.axis_index("subcore") * sc.num_cores + lax.axis_index("core")
        base = wid * per_w
        pltpu.sync_copy(i_hbm.at[pl.ds(base, per_w)], idx_v)

        def fetch(s, b):
            return pltpu.make_async_copy(x_hbm.at[idx_v.at[pl.ds(s * w, w)]], rows_v.at[b],
                                         g_sem.at[b])

        def flush(s, b):
            return pltpu.make_async_copy(rows_v.at[b], o_hbm.at[pl.ds(base + s * w, w)],
                                         w_sem.at[b])

        fetch(0, 0).start()

        @pl.loop(0, n_steps, step=2)
        def _(s2):
            for b in range(2):
                s = s2 + b

                @pl.when(s + 1 < n_steps)
                def _():
                    @pl.when(s >= 1)
                    def _():
                        flush(s - 1, 1 - b).wait()
                    fetch(s + 1, 1 - b).start()

                fetch(s, b).wait()
                flush(s, b).start()

        flush(n_steps - 2, 0).wait()
        flush(n_steps - 1, 1).wait()

    return gather(table, idx)


def _combine_kernel(h1_ref, y0_ref, y1_ref, gt_ref, fg_ref, o_ref):
    gt = gt_ref[...]
    h = (h1_ref[...] + gt[:, 0:1] * _unpack_bf16_pairs(y0_ref[...])
         + gt[:, 1:2] * _unpack_bf16_pairs(y1_ref[...]))
    ms = jnp.mean(h * h, axis=-1, keepdims=True)
    o_ref[...] = (h * lax.rsqrt(ms + EPS)) * fg_ref[...]


def _combine(h1, y_tok, gates, final_g, tm):
    n = h1.shape[0]
    row = lambda i: (i, 0)
    return pl.pallas_call(
        _combine_kernel,
        grid=(n // tm,),
        in_specs=[pl.BlockSpec((tm, D_MODEL), row),
                  pl.BlockSpec((tm, D_MODEL // 2), row),
                  pl.BlockSpec((tm, D_MODEL // 2), lambda i: (i + n // tm, 0)),
                  pl.BlockSpec((tm, 2), row),
                  pl.BlockSpec((1, D_MODEL), lambda i: (0, 0))],
        out_specs=pl.BlockSpec((tm, D_MODEL), row),
        out_shape=jax.ShapeDtypeStruct((n, D_MODEL), F32),
        compiler_params=_cparams("parallel"),
        name="combine",
    )(h1, y_tok, y_tok, gates, final_g)


def _block_diag(blocks):
    n, r, c = blocks.shape
    out = jnp.zeros((n * r, n * c), blocks.dtype)
    for i in range(n):
        out = out.at[i * r:(i + 1) * r, i * c:(i + 1) * c].set(blocks[i])
    return out


def _pad_rows(a, rows):
    return jnp.pad(a, ((0, rows - a.shape[0]), (0, 0)))


def _slot_kernel(off_ref, e_ref, o_ref):
    e = e_ref[...]
    rows, lanes = e.shape
    j = (lax.broadcasted_iota(jnp.int32, e.shape, 0) * lanes
         + lax.broadcasted_iota(jnp.int32, e.shape, 1))
    off = jnp.zeros_like(e)
    for k in range(N_EXPERTS):
        off = jnp.where(e == k, off_ref[k], off)
    o_ref[...] = j + off


def _sorted_slots(sorted_e, offset):
    rows = sorted_e.shape[0] // LANES
    grid_spec = pltpu.PrefetchScalarGridSpec(
        num_scalar_prefetch=1, grid=(1,),
        in_specs=[pl.BlockSpec((rows, LANES), lambda i, off: (0, 0))],
        out_specs=pl.BlockSpec((rows, LANES), lambda i, off: (0, 0)))
    return pl.pallas_call(
        _slot_kernel, grid_spec=grid_spec,
        out_shape=jax.ShapeDtypeStruct((rows, LANES), jnp.int32),
        compiler_params=_cparams("arbitrary"), name="sorted_slots",
    )(offset, sorted_e.reshape(rows, LANES)).reshape(-1)


def _dispatch_tables(eflat, counts, n):
    n_assign = 2 * n
    iota = jnp.arange(n_assign, dtype=jnp.int32)
    sorted_e, order = lax.sort((eflat, iota), num_keys=1, is_stable=True)
    starts = jnp.cumsum(counts) - counts
    padded = (counts + MOE_ROWS - 1) // MOE_ROWS * MOE_ROWS
    pend = jnp.cumsum(padded)
    pstart = pend - padded
    dest_sorted = _sorted_slots(sorted_e, pstart - starts)
    _, dest = lax.sort((order, dest_sorted), num_keys=1, is_stable=False)
    n_blocks = -(-(n_assign + N_EXPERTS * (MOE_ROWS - 1)) // MOE_ROWS)
    block_start = jnp.arange(n_blocks, dtype=jnp.int32) * MOE_ROWS
    block_expert = jnp.minimum(jnp.sum(block_start[:, None] >= pend[None, :], axis=1),
                               N_EXPERTS - 1).astype(jnp.int32)
    in_e = (block_start - pstart[block_expert])[:, None] + jnp.arange(MOE_ROWS, dtype=jnp.int32)
    runs = order[jnp.clip(in_e + starts[block_expert][:, None], 0, n_assign - 1)]
    slot = block_start[:, None] + jnp.arange(MOE_ROWS, dtype=jnp.int32)
    buf_tok = jnp.where(in_e < counts[block_expert][:, None], runs, slot) % n
    return buf_tok.reshape(-1), dest, block_expert


def kernel(x, meta_tokens, mix_norm, w_in, lru_conv_w, lru_conv_b, lru_w_r, lru_b_r, lru_w_i,
           lru_b_i, lru_lambda, lru_out_norm, dn_conv_w, dn_a_log, dn_dt_bias, dn_out_norm, w_out,
           ffn_norm, router_group_w, router_group_b, router_expert_w, router_expert_b, moe_w_gate,
           moe_w_up, moe_w_down, final_norm):
    bsz, seq, d = x.shape
    n = bsz * seq
    l = 0

    lw = LRU_WIDTH
    w = w_in[l]
    w_main = jnp.concatenate([w[:, :lw], w[:, 2 * lw:2 * lw + 3 * DN_WIDTH], w[:, lw:2 * lw],
                              w[:, 2 * lw + 3 * DN_WIDTH:2 * lw + 4 * DN_WIDTH]], axis=1).astype(BF16)
    w_ba = jnp.pad(w[:, 2 * lw + 4 * DN_WIDTH:], ((0, 0), (0, LANES - 2 * DN_HEADS))).astype(BF16)
    mix_g = mix_norm[l][None, :]
    conv_w = _pad_rows(jnp.concatenate([lru_conv_w[l], dn_conv_w[l]], axis=1), SUBLANES)
    lru_p = _pad_rows(jnp.stack([lru_conv_b[l], lru_b_r[l], lru_b_i[l], lru_lambda[l],
                                 lru_out_norm[l]]), SUBLANES)
    hb = LRU_BLOCKS // 2
    w_gate = jnp.stack([
        jnp.concatenate([_block_diag(lru_w_r[l][h * hb:(h + 1) * hb]),
                         _block_diag(lru_w_i[l][h * hb:(h + 1) * hb])], axis=1)
        for h in range(2)]).astype(BF16)
    head_p = jnp.zeros((SUBLANES, LANES), F32)
    head_p = head_p.at[0, DN_HEADS:2 * DN_HEADS].set(dn_a_log[l])
    head_p = head_p.at[1, DN_HEADS:2 * DN_HEADS].set(dn_dt_bias[l])
    dn_on = dn_out_norm[l][None, :]
    w_o = w_out[l].astype(BF16)
    ffn_g = ffn_norm[l][None, :]
    w_router = jnp.zeros((ROUTE_ROWS, D_MODEL), F32)
    w_router = w_router.at[:N_GROUPS].set(router_group_w[l].T)
    w_router = w_router.at[SUBLANES:SUBLANES + N_EXPERTS].set(router_expert_w[l].T).astype(BF16)
    b_router = jnp.zeros((ROUTE_ROWS,), F32).at[:N_GROUPS].set(router_group_b[l])
    b_router = b_router.at[SUBLANES:SUBLANES + N_EXPERTS].set(router_expert_b[l])
    b_router = jnp.broadcast_to(b_router[:, None], (ROUTE_ROWS, LANES))
    final_g = final_norm[None, :]
    weights = (mix_g, w_main, w_ba, conv_w, lru_p, w_gate, head_p, dn_on, w_o, ffn_g, w_router,
               b_router)

    meta_pad = CHUNK - N_META
    prefix = jnp.pad(meta_tokens, ((meta_pad, 0), (0, 0)))[None]
    zeros = lambda *s: jnp.zeros(s, F32)
    *_, h_meta, s_meta, tail_meta = _mixer(
        prefix, zeros(SUBLANES, CONV_COLS), zeros(SUBLANES, LRU_WIDTH),
        zeros(DN_HEADS, DN_HEAD_DIM, DN_HEAD_DIM), weights, CHUNK, meta_pad)

    h1, xn, route, cnt, _, _, _ = _mixer(x, tail_meta[0], h_meta[0], s_meta[0], weights, 256, 0)
    h1 = h1.reshape(n, d)
    eflat = jnp.transpose(route[:, 0:2, :], (1, 0, 2)).reshape(2 * n).astype(jnp.int32)
    gates = jnp.transpose(route[:, 2:4, :], (0, 2, 1)).reshape(n, 2)
    counts = jnp.sum(cnt, axis=(0, 2)).astype(jnp.int32)

    buf_tok, dest, block_expert = _dispatch_tables(eflat, counts, n)
    x_buf = _row_gather(xn.reshape(n, d // 2), buf_tok)
    y_buf = _experts(block_expert, x_buf, moe_w_gate[l], moe_w_up[l], moe_w_down[l])
    y_tok = _row_gather(y_buf, dest)
    out = _combine(h1, y_tok, gates, final_g, 512)
    return out.reshape(bsz, seq, d)
```

```python
import functools

import jax
import jax.numpy as jnp
from jax import lax
from jax.experimental import pallas as pl
from jax.experimental.pallas import tpu as pltpu
from jax.experimental.pallas import tpu_sc as plsc

F32 = jnp.float32
BF16 = jnp.bfloat16

D_MODEL = 1024
N_META = 16
CHUNK = 64
CONV_WIDTH = 4
LRU_WIDTH = 512
LRU_BLOCKS = 8
LRU_C = 8.0
DN_HEADS = 4
DN_HEAD_DIM = 128
DN_WIDTH = DN_HEADS * DN_HEAD_DIM
N_GROUPS = 4
EXPERTS_PER_GROUP = 8
N_EXPERTS = N_GROUPS * EXPERTS_PER_GROUP
D_EXPERT = 256
EPS = 1e-6
CONV_COLS = LRU_WIDTH + 3 * DN_WIDTH
GATE_COLS = LRU_WIDTH + DN_WIDTH
LANES = 128
SUBLANES = 8
INV_BLOCK = 16
MOE_ROWS = 512
ROUTE_ROWS = 48
VMEM_LIMIT = 56 * 1024 * 1024
GATHER_WINDOW = 64


def _cparams(*sem):
    return pltpu.CompilerParams(dimension_semantics=sem, vmem_limit_bytes=VMEM_LIMIT)


def _sigmoid(x):
    return 0.5 * jnp.tanh(0.5 * x) + 0.5


def _pack_bf16_pairs(x):
    c = x.shape[1] // 2

    def rne(v):
        b = lax.bitcast_convert_type(v, jnp.uint32)
        return b + jnp.uint32(0x7FFF) + ((b >> 16) & jnp.uint32(1))

    return (rne(x[:, :c]) >> 16) | (rne(x[:, c:]) & jnp.uint32(0xFFFF0000))


def _unpack_bf16_pairs(w):
    lo = lax.bitcast_convert_type(w << 16, F32)
    hi = lax.bitcast_convert_type(w & jnp.uint32(0xFFFF0000), F32)
    return jnp.concatenate([lo, hi], axis=1)


def _softplus(x):
    return jnp.maximum(x, 0.0) + jnp.log(1.0 + jnp.exp(-jnp.abs(x)))


def _gelu_tanh(x):
    return 0.5 * x * (1.0 + jnp.tanh(0.7978845608028654 * (x + 0.044715 * (x * x * x))))


def _mm(a, b):
    return jnp.dot(a.astype(BF16), b.astype(BF16), preferred_element_type=F32)


def _mm_nt(a, b):
    return lax.dot_general(a.astype(BF16), b.astype(BF16), (((1,), (1,)), ((), ())),
                           preferred_element_type=F32)


def _mm_tn(a, b):
    return lax.dot_general(a.astype(BF16), b.astype(BF16), (((0,), (0,)), ((), ())),
                           preferred_element_type=F32)


def _lru_branch(xc, gate, p_ref, wg_ref, h_sc, valid):
    tb = xc.shape[0]
    b_r, b_i = p_ref[1:2, :], p_ref[2:3, :]
    lam, out_g = p_ref[3:4, :], p_ref[4:5, :]
    half = LRU_WIDTH // 2
    xcb = xc.astype(BF16)
    g0 = jnp.dot(xcb[:, :half], wg_ref[0], preferred_element_type=F32)
    g1 = jnp.dot(xcb[:, half:], wg_ref[1], preferred_element_type=F32)
    r = _sigmoid(jnp.concatenate([g0[:, :half], g1[:, :half]], axis=1) + b_r)
    i = _sigmoid(jnp.concatenate([g0[:, half:], g1[:, half:]], axis=1) + b_i)
    log_a = (-LRU_C) * r * _softplus(-lam)
    a = jnp.exp(log_a)
    var = 1.0 - jnp.exp(2.0 * log_a)
    b = jnp.where(var > 0.0, var * lax.rsqrt(var), 0.0) * (i * xc)
    if valid is not None:
        b = jnp.where(valid, b, 0.0)
        a = jnp.where(valid, a, 1.0)

    sub = lax.broadcasted_iota(jnp.int32, (tb, LRU_WIDTH), 0) & (SUBLANES - 1)
    for s in (1, 2, 4):
        keep = sub >= s
        b = jnp.where(keep, a * pltpu.roll(b, s, axis=0), 0.0) + b
        a = jnp.where(keep, a * pltpu.roll(a, s, axis=0), a)
    h = h_sc[...]
    hs = []
    for g in range(tb // SUBLANES):
        hg = a[g * SUBLANES:(g + 1) * SUBLANES] * h + b[g * SUBLANES:(g + 1) * SUBLANES]
        hs.append(hg)
        h = jnp.broadcast_to(hg[SUBLANES - 1:SUBLANES, :], (SUBLANES, LRU_WIDTH))
    h_sc[...] = h
    out = jnp.concatenate(hs, axis=0) * _gelu_tanh(gate)
    ms = jnp.mean(out * out, axis=-1, keepdims=True)
    return (out * lax.rsqrt(ms + EPS)) * out_g


def _pair_mm(x, y, half_masks):
    yb = y.astype(BF16)
    rhs = jnp.concatenate([yb * half_masks[0], yb * half_masks[1]], axis=0)
    return jnp.dot(x.astype(BF16), rhs, preferred_element_type=F32)


def _pair_inverse(a_list, eye, same16, half_masks):
    def mm(xs, ys):
        return [_pair_mm(x, y, half_masks) for x, y in zip(xs, ys)]

    def plus(xs):
        return [eye + x for x in xs]

    def minus(xs):
        return [eye - x for x in xs]

    d = [jnp.where(same16, a, 0.0) for a in a_list]
    d2 = mm(d, d)
    d4 = mm(d2, d2)
    d8 = mm(d4, d4)
    p = mm(mm(mm(minus(d), plus(d2)), plus(d4)), plus(d8))
    m = mm(p, [a - x for a, x in zip(a_list, d)])
    q = mm(minus(m), plus(mm(m, m)))
    return mm(q, p)


def _dn_branch(act, z, ba, hp_ref, on_ref, s_sc, valid):
    tb = act.shape[0]
    hd = DN_HEAD_DIM
    act = act * _sigmoid(act)
    beta_t = _sigmoid(ba)
    g_t = pltpu.roll(-jnp.exp(hp_ref[0:1, :]) * _softplus(ba + hp_ref[1:2, :]),
                     LANES - DN_HEADS, axis=1)
    if valid is not None:
        act = jnp.where(valid, act, 0.0)
        beta_t = jnp.where(valid, beta_t, 0.0)
        g_t = jnp.where(valid, g_t, 0.0)

    heads = []
    for i in range(3 * DN_HEADS):
        seg = act[:, i * hd:(i + 1) * hd]
        if i < 2 * DN_HEADS:
            nrm = lax.rsqrt(jnp.sum(seg * seg, axis=-1, keepdims=True) + EPS)
            if i < DN_HEADS:
                nrm = nrm * (hd ** -0.5)
            seg = seg * nrm
        heads.append(seg)
    qn, kn, vv = heads[:DN_HEADS], heads[DN_HEADS:2 * DN_HEADS], heads[2 * DN_HEADS:]

    row_c = lax.broadcasted_iota(jnp.int32, (tb, LANES), 0) & (CHUNK - 1)
    cum = g_t
    s = 1
    while s < CHUNK:
        cum = cum + jnp.where(row_c >= s, pltpu.roll(cum, s, axis=0), 0.0)
        s *= 2
    e_cum = jnp.exp(cum)

    ri = lax.broadcasted_iota(jnp.int32, (CHUNK, LANES), 0)
    li = lax.broadcasted_iota(jnp.int32, (CHUNK, LANES), 1)
    lo_half = li < CHUNK
    half_masks = (lo_half.astype(BF16), (li >= CHUNK).astype(BF16))
    cj = li & (CHUNK - 1)
    eye_b = ri == cj
    eye = eye_b.astype(F32)
    causal = ri >= cj
    strict = ri > cj
    same16 = (ri // INV_BLOCK) == (cj // INV_BLOCK)
    zero_c = jnp.zeros((CHUNK, hd), F32)
    out_g = on_ref[...]

    def bd_rows(x0, x1):
        z0 = jnp.zeros_like(x0)
        return jnp.concatenate([jnp.concatenate([x0, z0], axis=1),
                                jnp.concatenate([z0, x1], axis=1)], axis=0)

    n_ch = tb // CHUNK
    n_hp = DN_HEADS // 2
    probs = [(ch, hp) for ch in range(n_ch) for hp in range(n_hp)]
    qkm, q_dec, k_dec, rhs, a_list, dec_row = [], [], [], [], [], []
    for ch, hp in probs:
        rows = slice(ch * CHUNK, (ch + 1) * CHUNK)
        h0, h1 = 2 * hp, 2 * hp + 1
        cum_c, beta_c, ecum_c = cum[rows], beta_t[rows], e_cum[rows]
        last = cum_c[CHUNK - 1:CHUNK, :]

        def tile(arr):
            return jnp.where(lo_half, jnp.broadcast_to(arr[:, h0:h0 + 1], (CHUNK, LANES)),
                             jnp.broadcast_to(arr[:, h1:h1 + 1], (CHUNK, LANES)))

        def wide(arr):
            return jnp.concatenate([jnp.broadcast_to(arr[:, h0:h0 + 1], (arr.shape[0], hd)),
                                    jnp.broadcast_to(arr[:, h1:h1 + 1], (arr.shape[0], hd))],
                                   axis=1)

        cum_cp = tile(cum_c)
        cum_rp = jnp.sum(jnp.where(eye_b, cum_cp, 0.0), axis=0, keepdims=True)
        decay = jnp.where(causal, jnp.exp(jnp.where(causal, cum_cp - cum_rp, 0.0)), 0.0)
        q_p = jnp.concatenate([qn[h0][rows], qn[h1][rows]], axis=1)
        k_p = jnp.concatenate([kn[h0][rows], kn[h1][rows]], axis=1)
        v_p = jnp.concatenate([vv[h0][rows], vv[h1][rows]], axis=1)
        qkk = _mm_nt(jnp.concatenate([q_p, k_p], axis=0),
                     bd_rows(kn[h0][rows], kn[h1][rows]))
        qkm.append(qkk[:CHUNK] * decay)
        a_list.append(jnp.where(strict, qkk[CHUNK:] * decay, 0.0) * tile(beta_c))
        vb = v_p * wide(beta_c)
        kb = k_p * wide(beta_c * ecum_c)
        rhs.append(jnp.concatenate(
            [jnp.concatenate([vb[:, :hd], zero_c, kb[:, :hd], zero_c], axis=1),
             jnp.concatenate([zero_c, vb[:, hd:], zero_c, kb[:, hd:]], axis=1)], axis=0))
        q_dec.append(q_p * wide(ecum_c))
        k_dec.append(k_p * wide(jnp.exp(last - cum_c)))
        dec_row.append(wide(jnp.exp(last)))

    t_inv = _pair_inverse(a_list, eye, same16, half_masks)
    sol = [_mm(ti, r) for ti, r in zip(t_inv, rhs)]
    ktuw = [[_mm_tn(kd[:, e * hd:(e + 1) * hd],
                    jnp.concatenate([so[:, e * hd:(e + 1) * hd],
                                     so[:, (2 + e) * hd:(3 + e) * hd]], axis=1))
             for e in range(2)]
            for kd, so in zip(k_dec, sol)]
    quw = [_mm(qm, jnp.concatenate(
               [jnp.concatenate([so[:, :hd], zero_c, so[:, 2 * hd:3 * hd], zero_c], axis=1),
                jnp.concatenate([zero_c, so[:, hd:2 * hd], zero_c, so[:, 3 * hd:]], axis=1)], axis=0))
           for qm, so in zip(qkm, sol)]

    state = [s_sc[h] for h in range(DN_HEADS)]
    y_rows = []
    for ch in range(n_ch):
        rows = slice(ch * CHUNK, (ch + 1) * CHUNK)
        res = []
        y_heads = []
        for h in range(DN_HEADS):
            i, e = ch * n_hp + h // 2, h % 2
            q_eff = q_dec[i][:, e * hd:(e + 1) * hd] - quw[i][:, (2 + e) * hd:(3 + e) * hd]
            res.append(_mm(jnp.concatenate([ktuw[i][e][:, hd:], q_eff], axis=0), state[h]))
        for h in range(DN_HEADS):
            i, e = ch * n_hp + h // 2, h % 2
            o = res[h][hd:] + quw[i][:, e * hd:(e + 1) * hd]
            state[h] = (state[h] * dec_row[i][:, e * hd:(e + 1) * hd] - res[h][:hd]
                        + ktuw[i][e][:, :hd])
            ms = jnp.mean(o * o, axis=-1, keepdims=True)
            zz = z[rows, h * hd:(h + 1) * hd]
            y_heads.append(((o * lax.rsqrt(ms + EPS)) * out_g) * (zz * _sigmoid(zz)))
        y_rows.append(jnp.concatenate(y_heads, axis=1))
    for h in range(DN_HEADS):
        s_sc[h] = state[h]
    return jnp.concatenate(y_rows, axis=0)


def _route(logits_t):
    tm = logits_t.shape[1]
    row = lax.broadcasted_iota(jnp.int32, (SUBLANES, tm), 0).astype(F32)
    neg = -jnp.inf
    big = 1e9
    lg = jnp.where(row < N_GROUPS, logits_t[0:SUBLANES], neg)
    mg = jnp.max(lg, axis=0, keepdims=True)
    p_sel = 1.0 / jnp.sum(jnp.exp(lg - mg), axis=0, keepdims=True)
    g_sel = jnp.min(jnp.where(lg == mg, row, big), axis=0, keepdims=True)
    le = logits_t[SUBLANES:2 * SUBLANES]
    for g in range(1, N_GROUPS):
        le = jnp.where(g_sel == float(g), logits_t[(g + 1) * SUBLANES:(g + 2) * SUBLANES], le)
    m1 = jnp.max(le, axis=0, keepdims=True)
    i1 = jnp.min(jnp.where(le == m1, row, big), axis=0, keepdims=True)
    le2 = jnp.where(row == i1, neg, le)
    m2 = jnp.max(le2, axis=0, keepdims=True)
    i2 = jnp.min(jnp.where(le2 == m2, row, big), axis=0, keepdims=True)
    e21 = jnp.exp(m2 - m1)
    w1 = p_sel / (1.0 + e21)
    w2 = p_sel * e21 / (1.0 + e21)
    base = g_sel * float(EXPERTS_PER_GROUP)
    return jnp.where(row == 0.0, base + i1,
                     jnp.where(row == 1.0, base + i2,
                               jnp.where(row == 2.0, w1, jnp.where(row == 3.0, w2, 0.0))))


def _mixer_kernel(x_ref, tail_ref, h0_ref, s0_ref, mg_ref, wm_ref, wba_ref, cw_ref, lp_ref, wg_ref,
                  hp_ref, on_ref, wo_ref, fg_ref, wr_ref, br_ref,
                  h1_ref, xn_ref, rt_ref, cn_ref, hl_ref, sl_ref, tl_ref,
                  h_sc, s_sc, cs_sc, cnt_sc, *, tb, pad):
    t = pl.program_id(1)

    @pl.when(t == 0)
    def _():
        h_sc[...] = h0_ref[...]
        s_sc[...] = s0_ref[...]
        cs_sc[0:SUBLANES, :] = tail_ref[...]

    x = x_ref[...]
    ms = jnp.mean(x * x, axis=-1, keepdims=True)
    u = ((x * lax.rsqrt(ms + EPS)) * mg_ref[...]).astype(BF16)
    valid = None
    if pad:
        valid = lax.broadcasted_iota(jnp.int32, (tb, 1), 0) >= pad

    for n in range(0, CONV_COLS, 512):
        cs_sc[SUBLANES:SUBLANES + tb, n:n + 512] = jnp.dot(u, wm_ref[:, n:n + 512],
                                                           preferred_element_type=F32)
    gate = jnp.dot(u, wm_ref[:, CONV_COLS:CONV_COLS + LRU_WIDTH], preferred_element_type=F32)
    z = jnp.dot(u, wm_ref[:, CONV_COLS + LRU_WIDTH:], preferred_element_type=F32)
    ba = jnp.dot(u, wba_ref[...], preferred_element_type=F32)

    def conv(lo, width):
        acc = cs_sc[SUBLANES:SUBLANES + tb, lo:lo + width] * cw_ref[3:4, lo:lo + width]
        for d in range(1, CONV_WIDTH):
            acc = acc + (cs_sc[SUBLANES - d:SUBLANES - d + tb, lo:lo + width]
                         * cw_ref[3 - d:4 - d, lo:lo + width])
        return acc

    xc = conv(0, LRU_WIDTH) + lp_ref[0:1, :]
    act = conv(LRU_WIDTH, 3 * DN_WIDTH)
    cs_sc[0:SUBLANES, :] = cs_sc[tb:tb + SUBLANES, :]

    y_lru = _lru_branch(xc, gate, lp_ref, wg_ref, h_sc, valid)
    y_dn = _dn_branch(act, z, ba, hp_ref, on_ref, s_sc, valid)

    mix = jnp.dot(y_lru.astype(BF16), wo_ref[:LRU_WIDTH, :], preferred_element_type=F32)
    mix = mix + jnp.dot(y_dn.astype(BF16), wo_ref[LRU_WIDTH:, :], preferred_element_type=F32)
    h1 = x + mix
    h1_ref[...] = h1
    ms1 = jnp.mean(h1 * h1, axis=-1, keepdims=True)
    xn = (h1 * lax.rsqrt(ms1 + EPS)) * fg_ref[...]
    xn_ref[...] = _pack_bf16_pairs(xn)
    logits_t = lax.dot_general(wr_ref[...], xn.astype(BF16), (((1,), (1,)), ((), ())),
                               preferred_element_type=F32) + br_ref[:, 0:1]
    route = _route(logits_t)
    rt_ref[...] = route

    erow = lax.broadcasted_iota(jnp.int32, (N_EXPERTS, tb), 0).astype(F32)
    hist = (erow == route[0:1, :]).astype(F32) + (erow == route[1:2, :]).astype(F32)
    if pad:
        hist = jnp.where(lax.broadcasted_iota(jnp.int32, (1, tb), 1) >= pad, hist, 0.0)

    @pl.when(t == 0)
    def _():
        cnt_sc[...] = hist

    @pl.when(t > 0)
    def _():
        cnt_sc[...] = cnt_sc[...] + hist

    @pl.when(t == pl.num_programs(1) - 1)
    def _():
        cn_ref[...] = cnt_sc[...]
        hl_ref[...] = h_sc[...]
        sl_ref[...] = s_sc[...]
        tl_ref[...] = cs_sc[0:SUBLANES, :]


def _mixer(x3, tail8, h0, s0, weights, tb, pad):
    bsz, t, _ = x3.shape
    assert tb % CHUNK == 0 and t % tb == 0 and (pad == 0 or t == tb)
    fix2 = lambda b, i: (0, 0)
    fix3 = lambda b, i: (0, 0, 0)
    blk = lambda b, i: (b, i, 0)
    per_b = lambda b, i: (b, 0, 0)
    w_specs = [pl.BlockSpec(w.shape, fix2 if w.ndim == 2 else fix3) for w in weights]
    return pl.pallas_call(
        functools.partial(_mixer_kernel, tb=tb, pad=pad),
        grid=(bsz, t // tb),
        in_specs=[pl.BlockSpec((None, tb, D_MODEL), blk),
                  pl.BlockSpec((SUBLANES, CONV_COLS), fix2),
                  pl.BlockSpec((SUBLANES, LRU_WIDTH), fix2),
                  pl.BlockSpec((DN_HEADS, DN_HEAD_DIM, DN_HEAD_DIM), fix3)] + w_specs,
        out_specs=[pl.BlockSpec((None, tb, D_MODEL), blk),
                   pl.BlockSpec((None, tb, D_MODEL // 2), blk),
                   pl.BlockSpec((None, SUBLANES, tb), lambda b, i: (b, 0, i)),
                   pl.BlockSpec((None, N_EXPERTS, tb), per_b),
                   pl.BlockSpec((None, SUBLANES, LRU_WIDTH), per_b),
                   pl.BlockSpec((None, DN_HEADS, DN_HEAD_DIM, DN_HEAD_DIM),
                                lambda b, i: (b, 0, 0, 0)),
                   pl.BlockSpec((None, SUBLANES, CONV_COLS), per_b)],
        out_shape=[jax.ShapeDtypeStruct((bsz, t, D_MODEL), F32),
                   jax.ShapeDtypeStruct((bsz, t, D_MODEL // 2), jnp.uint32),
                   jax.ShapeDtypeStruct((bsz, SUBLANES, t), F32),
                   jax.ShapeDtypeStruct((bsz, N_EXPERTS, tb), F32),
                   jax.ShapeDtypeStruct((bsz, SUBLANES, LRU_WIDTH), F32),
                   jax.ShapeDtypeStruct((bsz, DN_HEADS, DN_HEAD_DIM, DN_HEAD_DIM), F32),
                   jax.ShapeDtypeStruct((bsz, SUBLANES, CONV_COLS), F32)],
        scratch_shapes=[pltpu.VMEM((SUBLANES, LRU_WIDTH), F32),
                        pltpu.VMEM((DN_HEADS, DN_HEAD_DIM, DN_HEAD_DIM), F32),
                        pltpu.VMEM((tb + SUBLANES, CONV_COLS), F32),
                        pltpu.VMEM((N_EXPERTS, tb), F32)],
        compiler_params=_cparams("parallel", "arbitrary"),
        name="mixer",
    )(x3, tail8, h0, s0, *weights)


def _expert_kernel(be_ref, x_ref, wg_ref, wu_ref, wd_ref, y_ref, wgu_sc, wd_sc):
    i = pl.program_id(0)
    prev = be_ref[jnp.maximum(i - 1, 0)]

    @pl.when((i == 0) | (be_ref[i] != prev))
    def _():
        wgu_sc[:, :D_EXPERT] = wg_ref[...].astype(BF16)
        wgu_sc[:, D_EXPERT:] = wu_ref[...].astype(BF16)
        wd_sc[...] = wd_ref[...].astype(BF16)

    x = _unpack_bf16_pairs(x_ref[...]).astype(BF16)
    gu = jnp.dot(x, wgu_sc[...], preferred_element_type=F32)
    g, u = gu[:, :D_EXPERT], gu[:, D_EXPERT:]
    hmid = (g * _sigmoid(g)) * u
    y_ref[...] = _pack_bf16_pairs(jnp.dot(hmid.astype(BF16), wd_sc[...],
                                          preferred_element_type=F32))


def _experts(block_expert, x_buf, w_gate, w_up, w_down):
    cap = x_buf.shape[0]
    by_expert = lambda i, be: (be[i], 0, 0)
    grid_spec = pltpu.PrefetchScalarGridSpec(
        num_scalar_prefetch=1,
        grid=(cap // MOE_ROWS,),
        in_specs=[pl.BlockSpec((MOE_ROWS, D_MODEL // 2), lambda i, be: (i, 0)),
                  pl.BlockSpec((None, D_MODEL, D_EXPERT), by_expert),
                  pl.BlockSpec((None, D_MODEL, D_EXPERT), by_expert),
                  pl.BlockSpec((None, D_EXPERT, D_MODEL), by_expert)],
        out_specs=pl.BlockSpec((MOE_ROWS, D_MODEL // 2), lambda i, be: (i, 0)),
        scratch_shapes=[pltpu.VMEM((D_MODEL, 2 * D_EXPERT), BF16),
                        pltpu.VMEM((D_EXPERT, D_MODEL), BF16)],
    )
    return pl.pallas_call(
        _expert_kernel,
        grid_spec=grid_spec,
        out_shape=jax.ShapeDtypeStruct((cap, D_MODEL // 2), jnp.uint32),
        compiler_params=_cparams("arbitrary"),
        name="experts",
    )(block_expert, x_buf, w_gate, w_up, w_down)


def _row_gather(table, idx):
    n_idx = idx.shape[0]
    d = table.shape[1]
    sc = plsc.get_sparse_core_info()
    n_workers = sc.num_cores * sc.num_subcores
    w = GATHER_WINDOW
    per_w = n_idx // n_workers
    n_steps = per_w // w
    assert per_w * n_workers == n_idx and n_steps * w == per_w and n_steps % 2 == 0
    mesh = plsc.VectorSubcoreMesh(core_axis_name="core", subcore_axis_name="subcore")

    @functools.partial(
        pl.kernel, out_type=jax.ShapeDtypeStruct((n_idx, d), table.dtype), mesh=mesh,
        scratch_types=[pltpu.VMEM((per_w,), jnp.int32),
                       pltpu.VMEM((2, w, d), table.dtype),
                       pltpu.SemaphoreType.DMA((2,)),
                       pltpu.SemaphoreType.DMA((2,))])
    def gather(x_hbm, i_hbm, o_hbm, idx_v, rows_v, g_sem, w_sem):
        wid = lax.axis_index("subcore") * sc.num_cores + lax.axis_index("core")
        base = wid * per_w
        pltpu.sync_copy(i_hbm.at[pl.ds(base, per_w)], idx_v)

        def fetch(s, b):
            return pltpu.make_async_copy(x_hbm.at[idx_v.at[pl.ds(s * w, w)]], rows_v.at[b],
                                         g_sem.at[b])

        def flush(s, b):
            return pltpu.make_async_copy(rows_v.at[b], o_hbm.at[pl.ds(base + s * w, w)],
                                         w_sem.at[b])

        fetch(0, 0).start()

        @pl.loop(0, n_steps, step=2)
        def _(s2):
            for b in range(2):
                s = s2 + b

                @pl.when(s + 1 < n_steps)
                def _():
                    @pl.when(s >= 1)
                    def _():
                        flush(s - 1, 1 - b).wait()
                    fetch(s + 1, 1 - b).start()

                fetch(s, b).wait()
                flush(s, b).start()

        flush(n_steps - 2, 0).wait()
        flush(n_steps - 1, 1).wait()

    return gather(table, idx)


def _combine_kernel(h1_ref, y0_ref, y1_ref, gt_ref, fg_ref, o_ref):
    gt = gt_ref[...]
    h = (h1_ref[...] + gt[:, 0:1] * _unpack_bf16_pairs(y0_ref[...])
         + gt[:, 1:2] * _unpack_bf16_pairs(y1_ref[...]))
    ms = jnp.mean(h * h, axis=-1, keepdims=True)
    o_ref[...] = (h * lax.rsqrt(ms + EPS)) * fg_ref[...]


def _combine(h1, y_tok, gates, final_g, tm):
    n = h1.shape[0]
    row = lambda i: (i, 0)
    return pl.pallas_call(
        _combine_kernel,
        grid=(n // tm,),
        in_specs=[pl.BlockSpec((tm, D_MODEL), row),
                  pl.BlockSpec((tm, D_MODEL // 2), row),
                  pl.BlockSpec((tm, D_MODEL // 2), lambda i: (i + n // tm, 0)),
                  pl.BlockSpec((tm, 2), row),
                  pl.BlockSpec((1, D_MODEL), lambda i: (0, 0))],
        out_specs=pl.BlockSpec((tm, D_MODEL), row),
        out_shape=jax.ShapeDtypeStruct((n, D_MODEL), F32),
        compiler_params=_cparams("parallel"),
        name="combine",
    )(h1, y_tok, y_tok, gates, final_g)


def _block_diag(blocks):
    n, r, c = blocks.shape
    out = jnp.zeros((n * r, n * c), blocks.dtype)
    for i in range(n):
        out = out.at[i * r:(i + 1) * r, i * c:(i + 1) * c].set(blocks[i])
    return out


def _pad_rows(a, rows):
    return jnp.pad(a, ((0, rows - a.shape[0]), (0, 0)))


def _slot_kernel(off_ref, e_ref, o_ref):
    e = e_ref[...]
    rows, lanes = e.shape
    j = (lax.broadcasted_iota(jnp.int32, e.shape, 0) * lanes
         + lax.broadcasted_iota(jnp.int32, e.shape, 1))
    off = jnp.zeros_like(e)
    for k in range(N_EXPERTS):
        off = jnp.where(e == k, off_ref[k], off)
    o_ref[...] = j + off


def _sorted_slots(sorted_e, offset):
    rows = sorted_e.shape[0] // LANES
    grid_spec = pltpu.PrefetchScalarGridSpec(
        num_scalar_prefetch=1, grid=(1,),
        in_specs=[pl.BlockSpec((rows, LANES), lambda i, off: (0, 0))],
        out_specs=pl.BlockSpec((rows, LANES), lambda i, off: (0, 0)))
    return pl.pallas_call(
        _slot_kernel, grid_spec=grid_spec,
        out_shape=jax.ShapeDtypeStruct((rows, LANES), jnp.int32),
        compiler_params=_cparams("arbitrary"), name="sorted_slots",
    )(offset, sorted_e.reshape(rows, LANES)).reshape(-1)


def _dispatch_tables(eflat, counts, n):
    n_assign = 2 * n
    iota = jnp.arange(n_assign, dtype=jnp.int32)
    sorted_e, order = lax.sort((eflat, iota), num_keys=1, is_stable=True)
    starts = jnp.cumsum(counts) - counts
    padded = (counts + MOE_ROWS - 1) // MOE_ROWS * MOE_ROWS
    pend = jnp.cumsum(padded)
    pstart = pend - padded
    dest_sorted = _sorted_slots(sorted_e, pstart - starts)
    _, dest = lax.sort((order, dest_sorted), num_keys=1, is_stable=False)
    n_blocks = -(-(n_assign + N_EXPERTS * (MOE_ROWS - 1)) // MOE_ROWS)
    block_start = jnp.arange(n_blocks, dtype=jnp.int32) * MOE_ROWS
    block_expert = jnp.minimum(jnp.sum(block_start[:, None] >= pend[None, :], axis=1),
                               N_EXPERTS - 1).astype(jnp.int32)
    in_e = (block_start - pstart[block_expert])[:, None] + jnp.arange(MOE_ROWS, dtype=jnp.int32)
    runs = order[jnp.clip(in_e + starts[block_expert][:, None], 0, n_assign - 1)]
    slot = block_start[:, None] + jnp.arange(MOE_ROWS, dtype=jnp.int32)
    buf_tok = jnp.where(in_e < counts[block_expert][:, None], runs, slot) % n
    return buf_tok.reshape(-1), dest, block_expert


def kernel(x, meta_tokens, mix_norm, w_in, lru_conv_w, lru_conv_b, lru_w_r, lru_b_r, lru_w_i,
           lru_b_i, lru_lambda, lru_out_norm, dn_conv_w, dn_a_log, dn_dt_bias, dn_out_norm, w_out,
           ffn_norm, router_group_w, router_group_b, router_expert_w, router_expert_b, moe_w_gate,
           moe_w_up, moe_w_down, final_norm):
    bsz, seq, d = x.shape
    n = bsz * seq
    l = 0

    lw = LRU_WIDTH
    w = w_in[l]
    w_main = jnp.concatenate([w[:, :lw], w[:, 2 * lw:2 * lw + 3 * DN_WIDTH], w[:, lw:2 * lw],
                              w[:, 2 * lw + 3 * DN_WIDTH:2 * lw + 4 * DN_WIDTH]], axis=1).astype(BF16)
    w_ba = jnp.pad(w[:, 2 * lw + 4 * DN_WIDTH:], ((0, 0), (0, LANES - 2 * DN_HEADS))).astype(BF16)
    mix_g = mix_norm[l][None, :]
    conv_w = _pad_rows(jnp.concatenate([lru_conv_w[l], dn_conv_w[l]], axis=1), SUBLANES)
    lru_p = _pad_rows(jnp.stack([lru_conv_b[l], lru_b_r[l], lru_b_i[l], lru_lambda[l],
                                 lru_out_norm[l]]), SUBLANES)
    hb = LRU_BLOCKS // 2
    w_gate = jnp.stack([
        jnp.concatenate([_block_diag(lru_w_r[l][h * hb:(h + 1) * hb]),
                         _block_diag(lru_w_i[l][h * hb:(h + 1) * hb])], axis=1)
        for h in range(2)]).astype(BF16)
    head_p = jnp.zeros((SUBLANES, LANES), F32)
    head_p = head_p.at[0, DN_HEADS:2 * DN_HEADS].set(dn_a_log[l])
    head_p = head_p.at[1, DN_HEADS:2 * DN_HEADS].set(dn_dt_bias[l])
    dn_on = dn_out_norm[l][None, :]
    w_o = w_out[l].astype(BF16)
    ffn_g = ffn_norm[l][None, :]
    w_router = jnp.zeros((ROUTE_ROWS, D_MODEL), F32)
    w_router = w_router.at[:N_GROUPS].set(router_group_w[l].T)
    w_router = w_router.at[SUBLANES:SUBLANES + N_EXPERTS].set(router_expert_w[l].T).astype(BF16)
    b_router = jnp.zeros((ROUTE_ROWS,), F32).at[:N_GROUPS].set(router_group_b[l])
    b_router = b_router.at[SUBLANES:SUBLANES + N_EXPERTS].set(router_expert_b[l])
    b_router = jnp.broadcast_to(b_router[:, None], (ROUTE_ROWS, LANES))
    final_g = final_norm[None, :]
    weights = (mix_g, w_main, w_ba, conv_w, lru_p, w_gate, head_p, dn_on, w_o, ffn_g, w_router,
               b_router)

    meta_pad = CHUNK - N_META
    prefix = jnp.pad(meta_tokens, ((meta_pad, 0), (0, 0)))[None]
    zeros = lambda *s: jnp.zeros(s, F32)
    *_, h_meta, s_meta, tail_meta = _mixer(
        prefix, zeros(SUBLANES, CONV_COLS), zeros(SUBLANES, LRU_WIDTH),
        zeros(DN_HEADS, DN_HEAD_DIM, DN_HEAD_DIM), weights, CHUNK, meta_pad)

    h1, xn, route, cnt, _, _, _ = _mixer(x, tail_meta[0], h_meta[0], s_meta[0], weights, 256, 0)
    h1 = h1.reshape(n, d)
    eflat = jnp.transpose(route[:, 0:2, :], (1, 0, 2)).reshape(2 * n).astype(jnp.int32)
    gates = jnp.transpose(route[:, 2:4, :], (0, 2, 1)).reshape(n, 2)
    counts = jnp.sum(cnt, axis=(0, 2)).astype(jnp.int32)

    buf_tok, dest, block_expert = _dispatch_tables(eflat, counts, n)
    x_buf = _row_gather(xn.reshape(n, d // 2), buf_tok)
    y_buf = _experts(block_expert, x_buf, moe_w_gate[l], moe_w_up[l], moe_w_down[l])
    y_tok = _row_gather(y_buf, dest)
    out = _combine(h1, y_tok, gates, final_g, 1024)
    return out.reshape(bsz, seq, d)
```

```python
import functools

import jax
import jax.numpy as jnp
from jax import lax
from jax.experimental import pallas as pl
from jax.experimental.pallas import tpu as pltpu
from jax.experimental.pallas import tpu_sc as plsc

F32 = jnp.float32
BF16 = jnp.bfloat16

D_MODEL = 1024
N_META = 16
CHUNK = 64
CONV_WIDTH = 4
LRU_WIDTH = 512
LRU_BLOCKS = 8
LRU_C = 8.0
DN_HEADS = 4
DN_HEAD_DIM = 128
DN_WIDTH = DN_HEADS * DN_HEAD_DIM
N_GROUPS = 4
EXPERTS_PER_GROUP = 8
N_EXPERTS = N_GROUPS * EXPERTS_PER_GROUP
D_EXPERT = 256
EPS = 1e-6
CONV_COLS = LRU_WIDTH + 3 * DN_WIDTH
GATE_COLS = LRU_WIDTH + DN_WIDTH
LANES = 128
SUBLANES = 8
INV_BLOCK = 16
MOE_ROWS = 512
ROUTE_ROWS = 48
VMEM_LIMIT = 56 * 1024 * 1024
GATHER_WINDOW = 64


def _cparams(*sem):
    return pltpu.CompilerParams(dimension_semantics=sem, vmem_limit_bytes=VMEM_LIMIT)


def _sigmoid(x):
    return 0.5 * jnp.tanh(0.5 * x) + 0.5


def _pack_bf16_pairs(x):
    c = x.shape[1] // 2

    def rne(v):
        b = lax.bitcast_convert_type(v, jnp.uint32)
        return b + jnp.uint32(0x7FFF) + ((b >> 16) & jnp.uint32(1))

    return (rne(x[:, :c]) >> 16) | (rne(x[:, c:]) & jnp.uint32(0xFFFF0000))


def _unpack_bf16_pairs(w):
    lo = lax.bitcast_convert_type(w << 16, F32)
    hi = lax.bitcast_convert_type(w & jnp.uint32(0xFFFF0000), F32)
    return jnp.concatenate([lo, hi], axis=1)


def _softplus(x):
    return jnp.maximum(x, 0.0) + jnp.log(1.0 + jnp.exp(-jnp.abs(x)))


def _gelu_tanh(x):
    return 0.5 * x * (1.0 + jnp.tanh(0.7978845608028654 * (x + 0.044715 * (x * x * x))))


def _mm(a, b):
    return jnp.dot(a.astype(BF16), b.astype(BF16), preferred_element_type=F32)


def _mm_nt(a, b):
    return lax.dot_general(a.astype(BF16), b.astype(BF16), (((1,), (1,)), ((), ())),
                           preferred_element_type=F32)


def _mm_tn(a, b):
    return lax.dot_general(a.astype(BF16), b.astype(BF16), (((0,), (0,)), ((), ())),
                           preferred_element_type=F32)


def _lru_branch(xc, gate, p_ref, wg_ref, h_sc, valid):
    tb = xc.shape[0]
    b_r, b_i = p_ref[1:2, :], p_ref[2:3, :]
    lam, out_g = p_ref[3:4, :], p_ref[4:5, :]
    half = LRU_WIDTH // 2
    xcb = xc.astype(BF16)
    g0 = jnp.dot(xcb[:, :half], wg_ref[0], preferred_element_type=F32)
    g1 = jnp.dot(xcb[:, half:], wg_ref[1], preferred_element_type=F32)
    r = _sigmoid(jnp.concatenate([g0[:, :half], g1[:, :half]], axis=1) + b_r)
    i = _sigmoid(jnp.concatenate([g0[:, half:], g1[:, half:]], axis=1) + b_i)
    log_a = (-LRU_C) * r * _softplus(-lam)
    a = jnp.exp(log_a)
    var = 1.0 - jnp.exp(2.0 * log_a)
    b = jnp.where(var > 0.0, var * lax.rsqrt(var), 0.0) * (i * xc)
    if valid is not None:
        b = jnp.where(valid, b, 0.0)
        a = jnp.where(valid, a, 1.0)

    sub = lax.broadcasted_iota(jnp.int32, (tb, LRU_WIDTH), 0) & (SUBLANES - 1)
    for s in (1, 2, 4):
        keep = sub >= s
        b = jnp.where(keep, a * pltpu.roll(b, s, axis=0), 0.0) + b
        a = jnp.where(keep, a * pltpu.roll(a, s, axis=0), a)
    h = h_sc[...]
    hs = []
    for g in range(tb // SUBLANES):
        hg = a[g * SUBLANES:(g + 1) * SUBLANES] * h + b[g * SUBLANES:(g + 1) * SUBLANES]
        hs.append(hg)
        h = jnp.broadcast_to(hg[SUBLANES - 1:SUBLANES, :], (SUBLANES, LRU_WIDTH))
    h_sc[...] = h
    out = jnp.concatenate(hs, axis=0) * _gelu_tanh(gate)
    ms = jnp.mean(out * out, axis=-1, keepdims=True)
    return (out * lax.rsqrt(ms + EPS)) * out_g


def _pair_mm(x, y, half_masks):
    yb = y.astype(BF16)
    rhs = jnp.concatenate([yb * half_masks[0], yb * half_masks[1]], axis=0)
    return jnp.dot(x.astype(BF16), rhs, preferred_element_type=F32)


def _pair_inverse(a_list, eye, same16, half_masks):
    def mm(xs, ys):
        return [_pair_mm(x, y, half_masks) for x, y in zip(xs, ys)]

    def plus(xs):
        return [eye + x for x in xs]

    def minus(xs):
        return [eye - x for x in xs]

    d = [jnp.where(same16, a, 0.0) for a in a_list]
    d2 = mm(d, d)
    d4 = mm(d2, d2)
    d8 = mm(d4, d4)
    p = mm(mm(mm(minus(d), plus(d2)), plus(d4)), plus(d8))
    m = mm(p, [a - x for a, x in zip(a_list, d)])
    q = mm(minus(m), plus(mm(m, m)))
    return mm(q, p)


def _dn_branch(act, z_fn, ba, hp_ref, on_ref, s_sc, valid):
    tb = act.shape[0]
    hd = DN_HEAD_DIM
    act = act * _sigmoid(act)
    beta_t = _sigmoid(ba)
    g_t = pltpu.roll(-jnp.exp(hp_ref[0:1, :]) * _softplus(ba + hp_ref[1:2, :]),
                     LANES - DN_HEADS, axis=1)
    if valid is not None:
        act = jnp.where(valid, act, 0.0)
        beta_t = jnp.where(valid, beta_t, 0.0)
        g_t = jnp.where(valid, g_t, 0.0)

    heads = []
    for i in range(3 * DN_HEADS):
        seg = act[:, i * hd:(i + 1) * hd]
        if i < 2 * DN_HEADS:
            nrm = lax.rsqrt(jnp.sum(seg * seg, axis=-1, keepdims=True) + EPS)
            if i < DN_HEADS:
                nrm = nrm * (hd ** -0.5)
            seg = seg * nrm
        heads.append(seg)
    qn, kn, vv = heads[:DN_HEADS], heads[DN_HEADS:2 * DN_HEADS], heads[2 * DN_HEADS:]

    row_c = lax.broadcasted_iota(jnp.int32, (tb, LANES), 0) & (CHUNK - 1)
    cum = g_t
    s = 1
    while s < CHUNK:
        cum = cum + jnp.where(row_c >= s, pltpu.roll(cum, s, axis=0), 0.0)
        s *= 2
    e_cum = jnp.exp(cum)

    ri = lax.broadcasted_iota(jnp.int32, (CHUNK, LANES), 0)
    li = lax.broadcasted_iota(jnp.int32, (CHUNK, LANES), 1)
    lo_half = li < CHUNK
    half_masks = (lo_half.astype(BF16), (li >= CHUNK).astype(BF16))
    cj = li & (CHUNK - 1)
    eye_b = ri == cj
    eye = eye_b.astype(F32)
    causal = ri >= cj
    strict = ri > cj
    same16 = (ri // INV_BLOCK) == (cj // INV_BLOCK)
    zero_c = jnp.zeros((CHUNK, hd), F32)
    out_g = on_ref[...]

    def bd_rows(x0, x1):
        z0 = jnp.zeros_like(x0)
        return jnp.concatenate([jnp.concatenate([x0, z0], axis=1),
                                jnp.concatenate([z0, x1], axis=1)], axis=0)

    n_ch = tb // CHUNK
    n_hp = DN_HEADS // 2
    probs = [(ch, hp) for ch in range(n_ch) for hp in range(n_hp)]
    qkm, q_dec, k_dec, rhs, a_list, dec_row = [], [], [], [], [], []
    for ch, hp in probs:
        rows = slice(ch * CHUNK, (ch + 1) * CHUNK)
        h0, h1 = 2 * hp, 2 * hp + 1
        cum_c, beta_c, ecum_c = cum[rows], beta_t[rows], e_cum[rows]
        last = cum_c[CHUNK - 1:CHUNK, :]

        def tile(arr):
            return jnp.where(lo_half, jnp.broadcast_to(arr[:, h0:h0 + 1], (CHUNK, LANES)),
                             jnp.broadcast_to(arr[:, h1:h1 + 1], (CHUNK, LANES)))

        def wide(arr):
            return jnp.concatenate([jnp.broadcast_to(arr[:, h0:h0 + 1], (arr.shape[0], hd)),
                                    jnp.broadcast_to(arr[:, h1:h1 + 1], (arr.shape[0], hd))],
                                   axis=1)

        cum_cp = tile(cum_c)
        cum_rp = jnp.sum(jnp.where(eye_b, cum_cp, 0.0), axis=0, keepdims=True)
        decay = jnp.where(causal, jnp.exp(jnp.where(causal, cum_cp - cum_rp, 0.0)), 0.0)
        q_p = jnp.concatenate([qn[h0][rows], qn[h1][rows]], axis=1)
        k_p = jnp.concatenate([kn[h0][rows], kn[h1][rows]], axis=1)
        v_p = jnp.concatenate([vv[h0][rows], vv[h1][rows]], axis=1)
        qkk = _mm_nt(jnp.concatenate([q_p, k_p], axis=0),
                     bd_rows(kn[h0][rows], kn[h1][rows]))
        qkm.append(qkk[:CHUNK] * decay)
        a_list.append(jnp.where(strict, qkk[CHUNK:] * decay, 0.0) * tile(beta_c))
        vb = v_p * wide(beta_c)
        kb = k_p * wide(beta_c * ecum_c)
        rhs.append(jnp.concatenate(
            [jnp.concatenate([vb[:, :hd], zero_c, kb[:, :hd], zero_c], axis=1),
             jnp.concatenate([zero_c, vb[:, hd:], zero_c, kb[:, hd:]], axis=1)], axis=0))
        q_dec.append(q_p * wide(ecum_c))
        k_dec.append(k_p * wide(jnp.exp(last - cum_c)))
        dec_row.append(wide(jnp.exp(last)))

    t_inv = _pair_inverse(a_list, eye, same16, half_masks)
    sol = [_mm(ti, r) for ti, r in zip(t_inv, rhs)]
    ktuw = [[_mm_tn(kd[:, e * hd:(e + 1) * hd],
                    jnp.concatenate([so[:, e * hd:(e + 1) * hd],
                                     so[:, (2 + e) * hd:(3 + e) * hd]], axis=1))
             for e in range(2)]
            for kd, so in zip(k_dec, sol)]
    quw = [_mm(qm, jnp.concatenate(
               [jnp.concatenate([so[:, :hd], zero_c, so[:, 2 * hd:3 * hd], zero_c], axis=1),
                jnp.concatenate([zero_c, so[:, hd:2 * hd], zero_c, so[:, 3 * hd:]], axis=1)], axis=0))
           for qm, so in zip(qkm, sol)]

    z = z_fn()
    state = [s_sc[h] for h in range(DN_HEADS)]
    y_rows = []
    for ch in range(n_ch):
        rows = slice(ch * CHUNK, (ch + 1) * CHUNK)
        res = []
        y_heads = []
        for h in range(DN_HEADS):
            i, e = ch * n_hp + h // 2, h % 2
            q_eff = q_dec[i][:, e * hd:(e + 1) * hd] - quw[i][:, (2 + e) * hd:(3 + e) * hd]
            res.append(_mm(jnp.concatenate([ktuw[i][e][:, hd:], q_eff], axis=0), state[h]))
        for h in range(DN_HEADS):
            i, e = ch * n_hp + h // 2, h % 2
            o = res[h][hd:] + quw[i][:, e * hd:(e + 1) * hd]
            state[h] = (state[h] * dec_row[i][:, e * hd:(e + 1) * hd] - res[h][:hd]
                        + ktuw[i][e][:, :hd])
            ms = jnp.mean(o * o, axis=-1, keepdims=True)
            zz = z[rows, h * hd:(h + 1) * hd]
            y_heads.append(((o * lax.rsqrt(ms + EPS)) * out_g) * (zz * _sigmoid(zz)))
        y_rows.append(jnp.concatenate(y_heads, axis=1))
    for h in range(DN_HEADS):
        s_sc[h] = state[h]
    return jnp.concatenate(y_rows, axis=0)


def _route(logits_t):
    tm = logits_t.shape[1]
    row = lax.broadcasted_iota(jnp.int32, (SUBLANES, tm), 0).astype(F32)
    neg = -jnp.inf
    big = 1e9
    lg = jnp.where(row < N_GROUPS, logits_t[0:SUBLANES], neg)
    mg = jnp.max(lg, axis=0, keepdims=True)
    p_sel = 1.0 / jnp.sum(jnp.exp(lg - mg), axis=0, keepdims=True)
    g_sel = jnp.min(jnp.where(lg == mg, row, big), axis=0, keepdims=True)
    le = logits_t[SUBLANES:2 * SUBLANES]
    for g in range(1, N_GROUPS):
        le = jnp.where(g_sel == float(g), logits_t[(g + 1) * SUBLANES:(g + 2) * SUBLANES], le)
    m1 = jnp.max(le, axis=0, keepdims=True)
    i1 = jnp.min(jnp.where(le == m1, row, big), axis=0, keepdims=True)
    le2 = jnp.where(row == i1, neg, le)
    m2 = jnp.max(le2, axis=0, keepdims=True)
    i2 = jnp.min(jnp.where(le2 == m2, row, big), axis=0, keepdims=True)
    e21 = jnp.exp(m2 - m1)
    w1 = p_sel / (1.0 + e21)
    w2 = p_sel * e21 / (1.0 + e21)
    base = g_sel * float(EXPERTS_PER_GROUP)
    return jnp.where(row == 0.0, base + i1,
                     jnp.where(row == 1.0, base + i2,
                               jnp.where(row == 2.0, w1, jnp.where(row == 3.0, w2, 0.0))))


def _mixer_kernel(x_ref, tail_ref, h0_ref, s0_ref, mg_ref, wm_ref, wba_ref, cw_ref, lp_ref, wg_ref,
                  hp_ref, on_ref, wo_ref, fg_ref, wr_ref, br_ref,
                  h1_ref, xn_ref, rt_ref, cn_ref, hl_ref, sl_ref, tl_ref,
                  h_sc, s_sc, cs_sc, cnt_sc, *, tb, pad):
    t = pl.program_id(1)

    @pl.when(t == 0)
    def _():
        h_sc[...] = h0_ref[...]
        s_sc[...] = s0_ref[...]
        cs_sc[0:SUBLANES, :] = tail_ref[...]

    x = x_ref[...]
    ms = jnp.mean(x * x, axis=-1, keepdims=True)
    u = ((x * lax.rsqrt(ms + EPS)) * mg_ref[...]).astype(BF16)
    valid = None
    if pad:
        valid = lax.broadcasted_iota(jnp.int32, (tb, 1), 0) >= pad

    for n in range(0, CONV_COLS, 512):
        cs_sc[SUBLANES:SUBLANES + tb, n:n + 512] = jnp.dot(u, wm_ref[:, n:n + 512],
                                                           preferred_element_type=F32)
    gate = jnp.dot(u, wm_ref[:, CONV_COLS:CONV_COLS + LRU_WIDTH], preferred_element_type=F32)
    z_fn = lambda: jnp.dot(u, wm_ref[:, CONV_COLS + LRU_WIDTH:], preferred_element_type=F32)
    ba = jnp.dot(u, wba_ref[...], preferred_element_type=F32)

    def conv(lo, width):
        acc = cs_sc[SUBLANES:SUBLANES + tb, lo:lo + width] * cw_ref[3:4, lo:lo + width]
        for d in range(1, CONV_WIDTH):
            acc = acc + (cs_sc[SUBLANES - d:SUBLANES - d + tb, lo:lo + width]
                         * cw_ref[3 - d:4 - d, lo:lo + width])
        return acc

    xc = conv(0, LRU_WIDTH) + lp_ref[0:1, :]
    act = conv(LRU_WIDTH, 3 * DN_WIDTH)
    cs_sc[0:SUBLANES, :] = cs_sc[tb:tb + SUBLANES, :]

    y_lru = _lru_branch(xc, gate, lp_ref, wg_ref, h_sc, valid)
    y_dn = _dn_branch(act, z_fn, ba, hp_ref, on_ref, s_sc, valid)

    mix = jnp.dot(y_lru.astype(BF16), wo_ref[:LRU_WIDTH, :], preferred_element_type=F32)
    mix = mix + jnp.dot(y_dn.astype(BF16), wo_ref[LRU_WIDTH:, :], preferred_element_type=F32)
    h1 = x + mix
    h1_ref[...] = h1
    ms1 = jnp.mean(h1 * h1, axis=-1, keepdims=True)
    xn = (h1 * lax.rsqrt(ms1 + EPS)) * fg_ref[...]
    xn_ref[...] = _pack_bf16_pairs(xn)
    logits_t = lax.dot_general(wr_ref[...], xn.astype(BF16), (((1,), (1,)), ((), ())),
                               preferred_element_type=F32) + br_ref[:, 0:1]
    route = _route(logits_t)
    rt_ref[...] = route

    erow = lax.broadcasted_iota(jnp.int32, (N_EXPERTS, tb), 0).astype(F32)
    hist = (erow == route[0:1, :]).astype(F32) + (erow == route[1:2, :]).astype(F32)
    if pad:
        hist = jnp.where(lax.broadcasted_iota(jnp.int32, (1, tb), 1) >= pad, hist, 0.0)

    @pl.when(t == 0)
    def _():
        cnt_sc[...] = hist

    @pl.when(t > 0)
    def _():
        cnt_sc[...] = cnt_sc[...] + hist

    @pl.when(t == pl.num_programs(1) - 1)
    def _():
        cn_ref[...] = cnt_sc[...]
        hl_ref[...] = h_sc[...]
        sl_ref[...] = s_sc[...]
        tl_ref[...] = cs_sc[0:SUBLANES, :]


def _mixer(x3, tail8, h0, s0, weights, tb, pad):
    bsz, t, _ = x3.shape
    assert tb % CHUNK == 0 and t % tb == 0 and (pad == 0 or t == tb)
    fix2 = lambda b, i: (0, 0)
    fix3 = lambda b, i: (0, 0, 0)
    blk = lambda b, i: (b, i, 0)
    per_b = lambda b, i: (b, 0, 0)
    w_specs = [pl.BlockSpec(w.shape, fix2 if w.ndim == 2 else fix3) for w in weights]
    return pl.pallas_call(
        functools.partial(_mixer_kernel, tb=tb, pad=pad),
        grid=(bsz, t // tb),
        in_specs=[pl.BlockSpec((None, tb, D_MODEL), blk),
                  pl.BlockSpec((SUBLANES, CONV_COLS), fix2),
                  pl.BlockSpec((SUBLANES, LRU_WIDTH), fix2),
                  pl.BlockSpec((DN_HEADS, DN_HEAD_DIM, DN_HEAD_DIM), fix3)] + w_specs,
        out_specs=[pl.BlockSpec((None, tb, D_MODEL), blk),
                   pl.BlockSpec((None, tb, D_MODEL // 2), blk),
                   pl.BlockSpec((None, SUBLANES, tb), lambda b, i: (b, 0, i)),
                   pl.BlockSpec((None, N_EXPERTS, tb), per_b),
                   pl.BlockSpec((None, SUBLANES, LRU_WIDTH), per_b),
                   pl.BlockSpec((None, DN_HEADS, DN_HEAD_DIM, DN_HEAD_DIM),
                                lambda b, i: (b, 0, 0, 0)),
                   pl.BlockSpec((None, SUBLANES, CONV_COLS), per_b)],
        out_shape=[jax.ShapeDtypeStruct((bsz, t, D_MODEL), F32),
                   jax.ShapeDtypeStruct((bsz, t, D_MODEL // 2), jnp.uint32),
                   jax.ShapeDtypeStruct((bsz, SUBLANES, t), F32),
                   jax.ShapeDtypeStruct((bsz, N_EXPERTS, tb), F32),
                   jax.ShapeDtypeStruct((bsz, SUBLANES, LRU_WIDTH), F32),
                   jax.ShapeDtypeStruct((bsz, DN_HEADS, DN_HEAD_DIM, DN_HEAD_DIM), F32),
                   jax.ShapeDtypeStruct((bsz, SUBLANES, CONV_COLS), F32)],
        scratch_shapes=[pltpu.VMEM((SUBLANES, LRU_WIDTH), F32),
                        pltpu.VMEM((DN_HEADS, DN_HEAD_DIM, DN_HEAD_DIM), F32),
                        pltpu.VMEM((tb + SUBLANES, CONV_COLS), F32),
                        pltpu.VMEM((N_EXPERTS, tb), F32)],
        compiler_params=_cparams("parallel", "arbitrary"),
        name="mixer",
    )(x3, tail8, h0, s0, *weights)


def _expert_kernel(be_ref, x_ref, wg_ref, wu_ref, wd_ref, y_ref, wgu_sc, wd_sc):
    i = pl.program_id(0)
    prev = be_ref[jnp.maximum(i - 1, 0)]

    @pl.when((i == 0) | (be_ref[i] != prev))
    def _():
        wgu_sc[:, :D_EXPERT] = wg_ref[...].astype(BF16)
        wgu_sc[:, D_EXPERT:] = wu_ref[...].astype(BF16)
        wd_sc[...] = wd_ref[...].astype(BF16)

    x = _unpack_bf16_pairs(x_ref[...]).astype(BF16)
    gu = jnp.dot(x, wgu_sc[...], preferred_element_type=F32)
    g, u = gu[:, :D_EXPERT], gu[:, D_EXPERT:]
    hmid = (g * _sigmoid(g)) * u
    y_ref[...] = _pack_bf16_pairs(jnp.dot(hmid.astype(BF16), wd_sc[...],
                                          preferred_element_type=F32))


def _experts(block_expert, x_buf, w_gate, w_up, w_down):
    cap = x_buf.shape[0]
    by_expert = lambda i, be: (be[i], 0, 0)
    grid_spec = pltpu.PrefetchScalarGridSpec(
        num_scalar_prefetch=1,
        grid=(cap // MOE_ROWS,),
        in_specs=[pl.BlockSpec((MOE_ROWS, D_MODEL // 2), lambda i, be: (i, 0)),
                  pl.BlockSpec((None, D_MODEL, D_EXPERT), by_expert),
                  pl.BlockSpec((None, D_MODEL, D_EXPERT), by_expert),
                  pl.BlockSpec((None, D_EXPERT, D_MODEL), by_expert)],
        out_specs=pl.BlockSpec((MOE_ROWS, D_MODEL // 2), lambda i, be: (i, 0)),
        scratch_shapes=[pltpu.VMEM((D_MODEL, 2 * D_EXPERT), BF16),
                        pltpu.VMEM((D_EXPERT, D_MODEL), BF16)],
    )
    return pl.pallas_call(
        _expert_kernel,
        grid_spec=grid_spec,
        out_shape=jax.ShapeDtypeStruct((cap, D_MODEL // 2), jnp.uint32),
        compiler_params=_cparams("arbitrary"),
        name="experts",
    )(block_expert, x_buf, w_gate, w_up, w_down)


def _row_gather(table, idx):
    n_idx = idx.shape[0]
    d = table.shape[1]
    sc = plsc.get_sparse_core_info()
    n_workers = sc.num_cores * sc.num_subcores
    w = GATHER_WINDOW
    per_w = n_idx // n_workers
    n_steps = per_w // w
    assert per_w * n_workers == n_idx and n_steps * w == per_w and n_steps % 2 == 0
    mesh = plsc.VectorSubcoreMesh(core_axis_name="core", subcore_axis_name="subcore")

    @functools.partial(
        pl.kernel, out_type=jax.ShapeDtypeStruct((n_idx, d), table.dtype), mesh=mesh,
        scratch_types=[pltpu.VMEM((per_w,), jnp.int32),
                       pltpu.VMEM((2, w, d), table.dtype),
                       pltpu.SemaphoreType.DMA((2,)),
                       pltpu.SemaphoreType.DMA((2,))])
    def gather(x_hbm, i_hbm, o_hbm, idx_v, rows_v, g_sem, w_sem):
        wid = lax.axis_index("subcore") * sc.num_cores + lax.axis_index("core")
        base = wid * per_w
        pltpu.sync_copy(i_hbm.at[pl.ds(base, per_w)], idx_v)

        def fetch(s, b):
            return pltpu.make_async_copy(x_hbm.at[idx_v.at[pl.ds(s * w, w)]], rows_v.at[b],
                                         g_sem.at[b])

        def flush(s, b):
            return pltpu.make_async_copy(rows_v.at[b], o_hbm.at[pl.ds(base + s * w, w)],
                                         w_sem.at[b])

        fetch(0, 0).start()

        @pl.loop(0, n_steps, step=2)
        def _(s2):
            for b in range(2):
                s = s2 + b

                @pl.when(s + 1 < n_steps)
                def _():
                    @pl.when(s >= 1)
                    def _():
                        flush(s - 1, 1 - b).wait()
                    fetch(s + 1, 1 - b).start()

                fetch(s, b).wait()
                flush(s, b).start()

        flush(n_steps - 2, 0).wait()
        flush(n_steps - 1, 1).wait()

    return gather(table, idx)


def _combine_kernel(h1_ref, y0_ref, y1_ref, gt_ref, fg_ref, o_ref):
    gt = gt_ref[...]
    h = (h1_ref[...] + gt[:, 0:1] * _unpack_bf16_pairs(y0_ref[...])
         + gt[:, 1:2] * _unpack_bf16_pairs(y1_ref[...]))
    ms = jnp.mean(h * h, axis=-1, keepdims=True)
    o_ref[...] = (h * lax.rsqrt(ms + EPS)) * fg_ref[...]


def _combine(h1, y_tok, gates, final_g, tm):
    n = h1.shape[0]
    row = lambda i: (i, 0)
    return pl.pallas_call(
        _combine_kernel,
        grid=(n // tm,),
        in_specs=[pl.BlockSpec((tm, D_MODEL), row),
                  pl.BlockSpec((tm, D_MODEL // 2), row),
                  pl.BlockSpec((tm, D_MODEL // 2), lambda i: (i + n // tm, 0)),
                  pl.BlockSpec((tm, 2), row),
                  pl.BlockSpec((1, D_MODEL), lambda i: (0, 0))],
        out_specs=pl.BlockSpec((tm, D_MODEL), row),
        out_shape=jax.ShapeDtypeStruct((n, D_MODEL), F32),
        compiler_params=_cparams("parallel"),
        name="combine",
    )(h1, y_tok, y_tok, gates, final_g)


def _block_diag(blocks):
    n, r, c = blocks.shape
    out = jnp.zeros((n * r, n * c), blocks.dtype)
    for i in range(n):
        out = out.at[i * r:(i + 1) * r, i * c:(i + 1) * c].set(blocks[i])
    return out


def _pad_rows(a, rows):
    return jnp.pad(a, ((0, rows - a.shape[0]), (0, 0)))


def _slot_kernel(off_ref, e_ref, o_ref):
    e = e_ref[...]
    rows, lanes = e.shape
    j = (lax.broadcasted_iota(jnp.int32, e.shape, 0) * lanes
         + lax.broadcasted_iota(jnp.int32, e.shape, 1))
    off = jnp.zeros_like(e)
    for k in range(N_EXPERTS):
        off = jnp.where(e == k, off_ref[k], off)
    o_ref[...] = j + off


def _sorted_slots(sorted_e, offset):
    rows = sorted_e.shape[0] // LANES
    grid_spec = pltpu.PrefetchScalarGridSpec(
        num_scalar_prefetch=1, grid=(1,),
        in_specs=[pl.BlockSpec((rows, LANES), lambda i, off: (0, 0))],
        out_specs=pl.BlockSpec((rows, LANES), lambda i, off: (0, 0)))
    return pl.pallas_call(
        _slot_kernel, grid_spec=grid_spec,
        out_shape=jax.ShapeDtypeStruct((rows, LANES), jnp.int32),
        compiler_params=_cparams("arbitrary"), name="sorted_slots",
    )(offset, sorted_e.reshape(rows, LANES)).reshape(-1)


def _dispatch_tables(eflat, counts, n):
    n_assign = 2 * n
    iota = jnp.arange(n_assign, dtype=jnp.int32)
    sorted_e, order = lax.sort((eflat, iota), num_keys=1, is_stable=True)
    starts = jnp.cumsum(counts) - counts
    padded = (counts + MOE_ROWS - 1) // MOE_ROWS * MOE_ROWS
    pend = jnp.cumsum(padded)
    pstart = pend - padded
    dest_sorted = _sorted_slots(sorted_e, pstart - starts)
    _, dest = lax.sort((order, dest_sorted), num_keys=1, is_stable=False)
    n_blocks = -(-(n_assign + N_EXPERTS * (MOE_ROWS - 1)) // MOE_ROWS)
    block_start = jnp.arange(n_blocks, dtype=jnp.int32) * MOE_ROWS
    block_expert = jnp.minimum(jnp.sum(block_start[:, None] >= pend[None, :], axis=1),
                               N_EXPERTS - 1).astype(jnp.int32)
    in_e = (block_start - pstart[block_expert])[:, None] + jnp.arange(MOE_ROWS, dtype=jnp.int32)
    runs = order[jnp.clip(in_e + starts[block_expert][:, None], 0, n_assign - 1)]
    slot = block_start[:, None] + jnp.arange(MOE_ROWS, dtype=jnp.int32)
    buf_tok = jnp.where(in_e < counts[block_expert][:, None], runs, slot) % n
    return buf_tok.reshape(-1), dest, block_expert


def kernel(x, meta_tokens, mix_norm, w_in, lru_conv_w, lru_conv_b, lru_w_r, lru_b_r, lru_w_i,
           lru_b_i, lru_lambda, lru_out_norm, dn_conv_w, dn_a_log, dn_dt_bias, dn_out_norm, w_out,
           ffn_norm, router_group_w, router_group_b, router_expert_w, router_expert_b, moe_w_gate,
           moe_w_up, moe_w_down, final_norm):
    bsz, seq, d = x.shape
    n = bsz * seq
    l = 0

    lw = LRU_WIDTH
    w = w_in[l]
    w_main = jnp.concatenate([w[:, :lw], w[:, 2 * lw:2 * lw + 3 * DN_WIDTH], w[:, lw:2 * lw],
                              w[:, 2 * lw + 3 * DN_WIDTH:2 * lw + 4 * DN_WIDTH]], axis=1).astype(BF16)
    w_ba = jnp.pad(w[:, 2 * lw + 4 * DN_WIDTH:], ((0, 0), (0, LANES - 2 * DN_HEADS))).astype(BF16)
    mix_g = mix_norm[l][None, :]
    conv_w = _pad_rows(jnp.concatenate([lru_conv_w[l], dn_conv_w[l]], axis=1), SUBLANES)
    lru_p = _pad_rows(jnp.stack([lru_conv_b[l], lru_b_r[l], lru_b_i[l], lru_lambda[l],
                                 lru_out_norm[l]]), SUBLANES)
    hb = LRU_BLOCKS // 2
    w_gate = jnp.stack([
        jnp.concatenate([_block_diag(lru_w_r[l][h * hb:(h + 1) * hb]),
                         _block_diag(lru_w_i[l][h * hb:(h + 1) * hb])], axis=1)
        for h in range(2)]).astype(BF16)
    head_p = jnp.zeros((SUBLANES, LANES), F32)
    head_p = head_p.at[0, DN_HEADS:2 * DN_HEADS].set(dn_a_log[l])
    head_p = head_p.at[1, DN_HEADS:2 * DN_HEADS].set(dn_dt_bias[l])
    dn_on = dn_out_norm[l][None, :]
    w_o = w_out[l].astype(BF16)
    ffn_g = ffn_norm[l][None, :]
    w_router = jnp.zeros((ROUTE_ROWS, D_MODEL), F32)
    w_router = w_router.at[:N_GROUPS].set(router_group_w[l].T)
    w_router = w_router.at[SUBLANES:SUBLANES + N_EXPERTS].set(router_expert_w[l].T).astype(BF16)
    b_router = jnp.zeros((ROUTE_ROWS,), F32).at[:N_GROUPS].set(router_group_b[l])
    b_router = b_router.at[SUBLANES:SUBLANES + N_EXPERTS].set(router_expert_b[l])
    b_router = jnp.broadcast_to(b_router[:, None], (ROUTE_ROWS, LANES))
    final_g = final_norm[None, :]
    weights = (mix_g, w_main, w_ba, conv_w, lru_p, w_gate, head_p, dn_on, w_o, ffn_g, w_router,
               b_router)

    meta_pad = CHUNK - N_META
    prefix = jnp.pad(meta_tokens, ((meta_pad, 0), (0, 0)))[None]
    zeros = lambda *s: jnp.zeros(s, F32)
    *_, h_meta, s_meta, tail_meta = _mixer(
        prefix, zeros(SUBLANES, CONV_COLS), zeros(SUBLANES, LRU_WIDTH),
        zeros(DN_HEADS, DN_HEAD_DIM, DN_HEAD_DIM), weights, CHUNK, meta_pad)

    h1, xn, route, cnt, _, _, _ = _mixer(x, tail_meta[0], h_meta[0], s_meta[0], weights, 256, 0)
    h1 = h1.reshape(n, d)
    eflat = jnp.transpose(route[:, 0:2, :], (1, 0, 2)).reshape(2 * n).astype(jnp.int32)
    gates = jnp.transpose(route[:, 2:4, :], (0, 2, 1)).reshape(n, 2)
    counts = jnp.sum(cnt, axis=(0, 2)).astype(jnp.int32)

    buf_tok, dest, block_expert = _dispatch_tables(eflat, counts, n)
    x_buf = _row_gather(xn.reshape(n, d // 2), buf_tok)
    y_buf = _experts(block_expert, x_buf, moe_w_gate[l], moe_w_up[l], moe_w_down[l])
    y_tok = _row_gather(y_buf, dest)
    out = _combine(h1, y_tok, gates, final_g, 1024)
    return out.reshape(bsz, seq, d)
```
